```python
import math
import jax, jax.numpy as jnp
from jax import lax
import numpy as np

D_MODEL = 1024
BATCH = 1
SEQ = 16384
DEPTH = 1
DEC_BATCH = 32
DEC_SEQ = 1
PAST_LEN = 16384
PAGE_SIZE = 128

HEAD_DIM = 64
N_FOX_HEADS = 8
N_MLSTM_HEADS = 8
FOX_WIDTH = N_FOX_HEADS * HEAD_DIM
MLSTM_WIDTH = N_MLSTM_HEADS * HEAD_DIM
MIX_WIDTH = FOX_WIDTH + MLSTM_WIDTH
IN_COLS = 3 * FOX_WIDTH + N_FOX_HEADS + 4 * MLSTM_WIDTH + 2 * N_MLSTM_HEADS
CONV_WIDTH = 4
QUERY_BLOCK = 128
MLSTM_CHUNK = 128
N_GROUPS = 4
EXPERTS_PER_GROUP = 8
N_EXPERTS = N_GROUPS * EXPERTS_PER_GROUP
TOP_K_IN_GROUP = 2
D_EXPERT = 256
RMS_EPS = 1e-6
NEG_INF = -1e30

kernel_name = "hymba_fox_mlstm_hmoe_step"


def rms_norm(x, g):
    xf = x.astype(jnp.float32)
    y = xf * lax.rsqrt(jnp.mean(xf * xf, axis=-1, keepdims=True) + RMS_EPS)
    return (y * g.astype(jnp.float32)).astype(x.dtype)


def fox_block_attend(q, k, v, dq, dk, pos_q, pos_k):
    s = jnp.einsum('bqhd,bkhd->bhqk', q, k).astype(jnp.float32) * (HEAD_DIM ** -0.5)
    logits = s + dq[..., :, None] - dk[..., None, :]
    mask = pos_k[None, :] <= pos_q[:, None]
    logits = jnp.where(mask, logits, NEG_INF)
    p = jax.nn.softmax(logits, axis=-1)
    return jnp.einsum('bhqk,bkhd->bqhd', p.astype(v.dtype), v)


def fox_prompt(q, k, v, logf):
    B, S, H, _ = q.shape
    c = jnp.cumsum(logf.astype(jnp.float32), axis=1).transpose(0, 2, 1)
    nblk = S // QUERY_BLOCK
    pos = jnp.arange(S)
    q_blocks = q.reshape(B, nblk, QUERY_BLOCK, H, HEAD_DIM).transpose(1, 0, 2, 3, 4)
    c_blocks = c.reshape(B, H, nblk, QUERY_BLOCK).transpose(2, 0, 1, 3)
    pos_blocks = pos.reshape(nblk, QUERY_BLOCK)
    out = lax.map(lambda a: fox_block_attend(a[0], k, v, a[1], c, a[2], pos), (q_blocks, c_blocks, pos_blocks))
    return out.transpose(1, 0, 2, 3, 4).reshape(B, S, H, HEAD_DIM)


def fox_sample(q, k_new, v_new, logf_new, past_k, past_v, past_logf):
    P = past_k.shape[1]
    T = q.shape[1]
    k_all = jnp.concatenate([past_k.astype(k_new.dtype), k_new], axis=1)
    v_all = jnp.concatenate([past_v.astype(v_new.dtype), v_new], axis=1)
    lf_all = jnp.concatenate([past_logf.astype(jnp.float32), logf_new.astype(jnp.float32)], axis=1)
    suffix_incl = jnp.flip(jnp.cumsum(jnp.flip(lf_all, 1), axis=1), 1)
    R = jnp.concatenate([suffix_incl[:, 1:], jnp.zeros_like(suffix_incl[:, :1])], axis=1)
    d = -R.transpose(0, 2, 1)
    pos_k = jnp.arange(P + T)
    pos_q = P + jnp.arange(T)
    return fox_block_attend(q, k_all, v_all, d[:, :, P:], d, pos_q, pos_k)


def mlstm_chunk(carry, inp):
    C, n, m = carry
    q, k, v, ig, lf = inp
    L = q.shape[2]
    a = jnp.cumsum(lf, axis=-1)
    causal = jnp.tril(jnp.ones((L, L), dtype=bool))
    D = jnp.where(causal, a[..., :, None] - a[..., None, :] + ig[..., None, :], NEG_INF)
    b = a + m[..., None]
    m_t = jnp.maximum(b, jnp.max(D, axis=-1))
    w_intra = jnp.exp(D - m_t[..., None])
    w_inter = jnp.exp(b - m_t)
    scores = jnp.einsum('bhtd,bhsd->bhts', q, k) * w_intra
    num = w_inter[..., None] * jnp.einsum('bhtd,bhde->bhte', q, C) + jnp.einsum('bhts,bhse->bhte', scores, v)
    den = w_inter * jnp.einsum('bhtd,bhd->bht', q, n) + jnp.sum(scores, axis=-1)
    h = num / jnp.maximum(jnp.abs(den), jnp.exp(-m_t))[..., None]
    m_new = m_t[..., -1]
    decay = jnp.exp(a[..., -1] + m - m_new)
    w_write = jnp.exp(a[..., -1:] - a + ig - m_new[..., None])
    C_new = decay[..., None, None] * C + jnp.einsum('bhs,bhsd,bhse->bhde', w_write, k, v)
    n_new = decay[..., None] * n + jnp.einsum('bhs,bhsd->bhd', w_write, k)
    return (C_new, n_new, m_new), h


def mlstm_scan(q, k, v, ig, lf, C0, n0, m0):
    B, T, H, HD = q.shape
    L = math.gcd(T, MLSTM_CHUNK)
    nc = T // L
    to_chunks = lambda a: a.astype(jnp.float32).reshape(B, nc, L, H, HD).transpose(1, 0, 3, 2, 4)
    gate_chunks = lambda a: a.astype(jnp.float32).reshape(B, nc, L, H).transpose(1, 0, 3, 2)
    carry0 = (C0.astype(jnp.float32), n0.astype(jnp.float32), m0.astype(jnp.float32))
    (C, n, m), h = lax.scan(mlstm_chunk, carry0,
                            (to_chunks(q), to_chunks(k), to_chunks(v), gate_chunks(ig), gate_chunks(lf)))
    h = h.transpose(1, 0, 3, 2, 4).reshape(B, T, H, HD)
    return h, C, n, m


def causal_conv(u, buf, w, b):
    T = u.shape[1]
    full = jnp.concatenate([buf.astype(u.dtype), u], axis=1)
    out = b + full[:, 0:T, :] * w[0]
    for j in range(1, CONV_WIDTH):
        out = out + full[:, j:j + T, :] * w[j]
    return jax.nn.silu(out), full[:, T:, :]


def token_mixers(x, conv_buf, C0, n0, m0, fox_fn, g_mix, w_in, b_fox_f, b_ml_i, b_ml_f,
                 w_conv, b_conv, g_fox_out, g_ml_out, w_out):
    B, T, _ = x.shape
    xn = rms_norm(x, g_mix)
    z = xn @ w_in
    sizes = [FOX_WIDTH, FOX_WIDTH, FOX_WIDTH, N_FOX_HEADS, 2 * MLSTM_WIDTH, MLSTM_WIDTH,
             N_MLSTM_HEADS, N_MLSTM_HEADS, MLSTM_WIDTH]
    points = np.cumsum(sizes)[:-1].tolist()
    fq, fk, fv, ff, mqk, mv, mi, mf, mo = jnp.split(z, points, axis=-1)
    heads = lambda a: a.reshape(B, T, -1, HEAD_DIM)
    logf = jax.nn.log_sigmoid(ff.astype(jnp.float32) + b_fox_f)
    k_f, v_f = heads(fk), heads(fv)
    y_fox = fox_fn(heads(fq), k_f, v_f, logf)
    qk_c, new_buf = causal_conv(mqk, conv_buf, w_conv, b_conv)
    mq, mk = jnp.split(qk_c, 2, axis=-1)
    ig = mi.astype(jnp.float32) + b_ml_i
    lf = jax.nn.log_sigmoid(mf.astype(jnp.float32) + b_ml_f)
    h, C, n, m = mlstm_scan(heads(mq), heads(mk) * (HEAD_DIM ** -0.5), heads(mv), ig, lf, C0, n0, m0)
    h = (h * jax.nn.sigmoid(heads(mo).astype(jnp.float32))).astype(x.dtype)
    y_ml = rms_norm(h, g_ml_out.reshape(N_MLSTM_HEADS, HEAD_DIM)).reshape(B, T, MLSTM_WIDTH)
    y_fx = rms_norm(y_fox.reshape(B, T, FOX_WIDTH), g_fox_out)
    y = jnp.concatenate([y_fx, y_ml], axis=-1) @ w_out
    return x + y, k_f, v_f, logf, new_buf, C, n, m


def hier_moe(x, g_ffn, w_rg, b_rg, w_re, b_re, w_eg, w_eu, w_ed):
    B, T, Dm = x.shape
    xn = rms_norm(x, g_ffn).reshape(B * T, Dm)
    N = B * T
    lg = (xn @ w_rg).astype(jnp.float32) + b_rg
    pg = jax.nn.softmax(lg, axis=-1)
    grp = jnp.argmax(lg, axis=-1)
    gate_g = jnp.take_along_axis(pg, grp[:, None], axis=-1)
    le = ((xn @ w_re).astype(jnp.float32) + b_re).reshape(N, N_GROUPS, EXPERTS_PER_GROUP)
    le_sel = jnp.take_along_axis(le, grp[:, None, None], axis=1)[:, 0]
    pe = jax.nn.softmax(le_sel, axis=-1)
    top_p, top_i = lax.top_k(pe, TOP_K_IN_GROUP)
    top_w = top_p / jnp.sum(top_p, axis=-1, keepdims=True) * gate_g
    expert_idx = grp[:, None] * EXPERTS_PER_GROUP + top_i
    combine = jnp.sum(jax.nn.one_hot(expert_idx, N_EXPERTS, dtype=jnp.float32) * top_w[..., None], axis=1)
    combine = combine.astype(x.dtype)
    out = jnp.zeros_like(xn)
    for e in range(N_EXPERTS):
        he = jax.nn.silu(xn @ w_eg[e]) * (xn @ w_eu[e])
        out = out + combine[:, e:e + 1] * (he @ w_ed[e])
    return x + out.reshape(B, T, Dm)


def setup_inputs(seed: int = 0) -> dict:
    key = jax.random.key(seed)
    ks = jax.random.split(key, 40)
    f32 = jnp.float32
    n_pages = PAST_LEN // PAGE_SIZE
    n_used = DEC_BATCH * n_pages
    n_phys = (5 * n_used + 3) // 4
    nrm = lambda k, shape, scale: scale * jax.random.normal(k, shape, f32)
    uni = lambda k, shape, lo, hi: jax.random.uniform(k, shape, f32, lo, hi)
    return {
        "x_prompt": nrm(ks[0], (BATCH, SEQ, D_MODEL), 1.0),
        "x_sample": nrm(ks[1], (DEC_BATCH, DEC_SEQ, D_MODEL), 1.0),
        "cache_k": nrm(ks[2], (DEPTH, n_phys, PAGE_SIZE, N_FOX_HEADS, HEAD_DIM), 1.0),
        "cache_v": nrm(ks[3], (DEPTH, n_phys, PAGE_SIZE, N_FOX_HEADS, HEAD_DIM), 1.0),
        "cache_logf": jax.nn.log_sigmoid(nrm(ks[4], (DEPTH, n_phys, PAGE_SIZE, N_FOX_HEADS), 1.0) + 7.0),
        "state_conv": nrm(ks[5], (DEPTH, DEC_BATCH, CONV_WIDTH - 1, 2 * MLSTM_WIDTH), 1.0),
        "state_C": nrm(ks[6], (DEPTH, DEC_BATCH, N_MLSTM_HEADS, HEAD_DIM, HEAD_DIM), 0.3),
        "state_n": nrm(ks[7], (DEPTH, DEC_BATCH, N_MLSTM_HEADS, HEAD_DIM), 0.3),
        "state_m": nrm(ks[8], (DEPTH, DEC_BATCH, N_MLSTM_HEADS), 1.0),
        "page_table": jax.random.permutation(ks[9], n_phys)[:n_used].reshape(DEC_BATCH, n_pages).astype(jnp.int32),
        "g_mix": 1.0 + nrm(ks[10], (DEPTH, D_MODEL), 0.02),
        "w_in": nrm(ks[11], (DEPTH, D_MODEL, IN_COLS), D_MODEL ** -0.5),
        "b_fox_f": uni(ks[12], (DEPTH, N_FOX_HEADS), 5.0, 9.0),
        "b_ml_i": nrm(ks[13], (DEPTH, N_MLSTM_HEADS), 0.1),
        "b_ml_f": uni(ks[14], (DEPTH, N_MLSTM_HEADS), 3.0, 6.0),
        "w_conv": nrm(ks[15], (DEPTH, CONV_WIDTH, 2 * MLSTM_WIDTH), 0.5),
        "b_conv": nrm(ks[16], (DEPTH, 2 * MLSTM_WIDTH), 0.02),
        "g_fox_out": 1.0 + nrm(ks[17], (DEPTH, FOX_WIDTH), 0.02),
        "g_ml_out": 1.0 + nrm(ks[18], (DEPTH, MLSTM_WIDTH), 0.02),
        "w_out": nrm(ks[19], (DEPTH, MIX_WIDTH, D_MODEL), MIX_WIDTH ** -0.5),
        "g_ffn": 1.0 + nrm(ks[20], (DEPTH, D_MODEL), 0.02),
        "w_router_group": nrm(ks[21], (DEPTH, D_MODEL, N_GROUPS), D_MODEL ** -0.5),
        "b_router_group": nrm(ks[22], (DEPTH, N_GROUPS), 0.01),
        "w_router_expert": nrm(ks[23], (DEPTH, D_MODEL, N_EXPERTS), D_MODEL ** -0.5),
        "b_router_expert": nrm(ks[24], (DEPTH, N_EXPERTS), 0.01),
        "w_exp_gate": nrm(ks[25], (DEPTH, N_EXPERTS, D_MODEL, D_EXPERT), D_MODEL ** -0.5),
        "w_exp_up": nrm(ks[26], (DEPTH, N_EXPERTS, D_MODEL, D_EXPERT), D_MODEL ** -0.5),
        "w_exp_down": nrm(ks[27], (DEPTH, N_EXPERTS, D_EXPERT, D_MODEL), D_EXPERT ** -0.5),
        "g_final": 1.0 + nrm(ks[28], (D_MODEL,), 0.02),
    }


def reference(x_prompt, x_sample, cache_k, cache_v, cache_logf, state_conv, state_C, state_n, state_m,
              page_table, g_mix, w_in, b_fox_f, b_ml_i, b_ml_f, w_conv, b_conv, g_fox_out, g_ml_out,
              w_out, g_ffn, w_router_group, b_router_group, w_router_expert, b_router_expert,
              w_exp_gate, w_exp_up, w_exp_down, g_final):
    B = x_prompt.shape[0]
    DB = x_sample.shape[0]
    xp, xs = x_prompt, x_sample
    kp_l, vp_l, lfp_l, cvp_l, Cp_l, np_l, mp_l = [], [], [], [], [], [], []
    ks_l, vs_l, lfs_l, cvs_l, Cs_l, ns_l, ms_l = [], [], [], [], [], [], []
    for l in range(DEPTH):
        mix_w = (g_mix[l], w_in[l], b_fox_f[l], b_ml_i[l], b_ml_f[l], w_conv[l], b_conv[l],
                 g_fox_out[l], g_ml_out[l], w_out[l])
        moe_w = (g_ffn[l], w_router_group[l], b_router_group[l], w_router_expert[l], b_router_expert[l],
                 w_exp_gate[l], w_exp_up[l], w_exp_down[l])
        zbuf = jnp.zeros((B, CONV_WIDTH - 1, 2 * MLSTM_WIDTH), xp.dtype)
        zC = jnp.zeros((B, N_MLSTM_HEADS, HEAD_DIM, HEAD_DIM), jnp.float32)
        zn = jnp.zeros((B, N_MLSTM_HEADS, HEAD_DIM), jnp.float32)
        zm = jnp.zeros((B, N_MLSTM_HEADS), jnp.float32)
        xp, kp, vp, lfp, cvp, Cp, n_p, mp = token_mixers(xp, zbuf, zC, zn, zm, fox_prompt, *mix_w)
        xp = hier_moe(xp, *moe_w)
        past_k = cache_k[l][page_table].reshape(DB, -1, N_FOX_HEADS, HEAD_DIM)
        past_v = cache_v[l][page_table].reshape(DB, -1, N_FOX_HEADS, HEAD_DIM)
        past_lf = cache_logf[l][page_table].reshape(DB, -1, N_FOX_HEADS)
        fox_fn = lambda q, k, v, lf, pk=past_k, pv=past_v, plf=past_lf: fox_sample(q, k, v, lf, pk, pv, plf)
        xs, ksn, vsn, lfs, cvs, Cs, n_s, ms = token_mixers(xs, state_conv[l], state_C[l], state_n[l],
                                                            state_m[l], fox_fn, *mix_w)
        xs = hier_moe(xs, *moe_w)
        kp_l.append(kp); vp_l.append(vp); lfp_l.append(lfp); cvp_l.append(cvp)
        Cp_l.append(Cp); np_l.append(n_p); mp_l.append(mp)
        ks_l.append(ksn); vs_l.append(vsn); lfs_l.append(lfs); cvs_l.append(cvs)
        Cs_l.append(Cs); ns_l.append(n_s); ms_l.append(ms)
    y_prompt = rms_norm(xp, g_final)
    y_sample = rms_norm(xs, g_final)
    return (y_prompt, y_sample,
            jnp.stack(kp_l), jnp.stack(vp_l), jnp.stack(lfp_l), jnp.stack(cvp_l),
            jnp.stack(Cp_l), jnp.stack(np_l), jnp.stack(mp_l),
            jnp.stack(ks_l), jnp.stack(vs_l), jnp.stack(lfs_l), jnp.stack(cvs_l),
            jnp.stack(Cs_l), jnp.stack(ns_l), jnp.stack(ms_l))
```

```python
import functools

import numpy as np
import jax
import jax.numpy as jnp
from jax import lax
from jax.experimental import pallas as pl
from jax.experimental.pallas import tpu as pltpu

HEAD_DIM = 64
N_HEADS = 8
WIDTH = N_HEADS * HEAD_DIM
CONV_WIDTH = 4
MLSTM_CHUNK = 128
N_GROUPS = 4
EXPERTS_PER_GROUP = 8
N_EXPERTS = N_GROUPS * EXPERTS_PER_GROUP
RMS_EPS = 1e-6
NEG_INF = -1e30
QK_SCALE = HEAD_DIM ** -0.5

LANES = 128
SUBLANES = 8
VMEM_LIMIT = 56 * 1024 * 1024

F32 = jnp.float32
BF16 = jnp.bfloat16


def _mm(a, b):
    return jnp.dot(a, b, preferred_element_type=F32)


def _mm_f32(a, b):
    return jnp.dot(a, b, preferred_element_type=F32, precision=lax.Precision.HIGHEST)


def _mm_nt(a, b):
    return lax.dot_general(a, b, (((1,), (1,)), ((), ())), preferred_element_type=F32)


def _mm_tn(a, b):
    return lax.dot_general(a, b, (((0,), (0,)), ((), ())), preferred_element_type=F32)


def _split3(a):
    hi = a.astype(BF16)
    r = a - hi.astype(F32)
    mid = r.astype(BF16)
    lo = (r - mid.astype(F32)).astype(BF16)
    return hi, mid, lo


def _mm3_right(a, b01):
    hi, mid, lo = _split3(a)
    return _mm(hi, b01) + _mm(mid, b01) + _mm(lo, b01)


def _mm3_left(a01, b):
    hi, mid, lo = _split3(b)
    return _mm(a01, hi) + _mm(a01, mid) + _mm(a01, lo)


def _log_sigmoid(x):
    return jnp.minimum(x, 0.0) - jnp.log1p(jnp.exp(-jnp.abs(x)))


def _rms(x, g):
    return x * lax.rsqrt(jnp.mean(x * x, axis=-1, keepdims=True) + RMS_EPS) * g


def _params(sem):
    return pltpu.CompilerParams(dimension_semantics=sem, vmem_limit_bytes=VMEM_LIMIT)


_C_Q, _C_K, _C_KRAW, _C_V, _C_MQK, _C_MV, _C_MO, _C_G, _C_END = (
    0, 1024, 2048, 2560, 3072, 4096, 4608, 5120, 5248)
_G_LOGF, _G_IG, _G_LF, _G_CUM = 0, 8, 16, 24
_AUG = HEAD_DIM


def _gate_tile(z, lane_axis):
    idx = lax.broadcasted_iota(jnp.int32, z.shape, lane_axis)
    is_ig = (idx >= _G_IG) & (idx < _G_LF)
    return jnp.where(is_ig, z, _log_sigmoid(z))


def _inproj_prompt_kernel(x_ref, gmix_ref, w_ref, wt_ref, brow_ref, bcol_ref, wconv_ref, bconv_ref,
                          p2q_ref, p2k_ref,
                          qa_ref, ka_ref, k_ref, v_ref, vt_ref, gcol_ref, grow_ref,
                          mq_ref, mk_ref, mv_ref, mo_ref, tail_ref,
                          conv_s, ccol_s, crow_s, *, tm):
    i = pl.program_id(0)

    @pl.when(i == 0)
    def _():
        conv_s[0:SUBLANES, :] = jnp.zeros((SUBLANES, conv_s.shape[1]), F32)
        ccol_s[...] = jnp.zeros(ccol_s.shape, F32)
        crow_s[...] = jnp.zeros(crow_s.shape, F32)

    xb = _rms(x_ref[...], gmix_ref[...]).astype(BF16)

    lane = lax.broadcasted_iota(jnp.int32, (tm, LANES), 1)
    g = _gate_tile(_mm(xb, w_ref[:, _C_G:_C_END]) + brow_ref[...], 1)
    zt = _mm_nt(wt_ref[...], xb)
    vt_ref[0] = zt[0:WIDTH].astype(BF16)
    gt = _gate_tile(zt[WIDTH:WIDTH + 32] + bcol_ref[...], 0)

    r = lax.broadcasted_iota(jnp.int32, (tm, tm), 0)
    c = lax.broadcasted_iota(jnp.int32, (tm, tm), 1)
    ltri = (c <= r).astype(BF16)
    utri = (r <= c).astype(BF16)
    cs = _mm3_left(ltri, g) + ccol_s[...]
    ccol_s[...] = cs[tm - 1:tm, :]
    cst = _mm3_right(gt[0:8], utri) + crow_s[...][:, 0:1]
    crow_s[...] = jnp.broadcast_to(cst[:, tm - 1:tm], crow_s.shape)
    in_cum = (lane >= _G_CUM) & (lane < _G_CUM + 8)
    gcol_ref[...] = jnp.where(in_cum, pltpu.roll(cs, _G_CUM, 1), g)
    grow_ref[...] = jnp.concatenate([gt[0:24], cst], axis=0)

    hi, mid, lo = _split3(cs)
    caug = jnp.where(lane < 8, hi.astype(F32),
                     jnp.where(lane < 16, pltpu.roll(mid.astype(F32), 8, 1),
                               jnp.where(lane < 24, pltpu.roll(lo.astype(F32), 16, 1),
                                         jnp.where(lane == 24, 1.0, 0.0)))).astype(BF16)
    zq = _mm(xb, w_ref[:, _C_Q:_C_K]) * QK_SCALE + _mm(caug, p2q_ref[...])
    zk = _mm(xb, w_ref[:, _C_K:_C_KRAW]) + _mm(caug, p2k_ref[...])
    for h in range(N_HEADS):
        qa_ref[h] = zq[:, h * LANES:(h + 1) * LANES].astype(BF16)
        ka_ref[h] = zk[:, h * LANES:(h + 1) * LANES].astype(BF16)

    k_ref[...] = _mm(xb, w_ref[:, _C_KRAW:_C_V])
    v_ref[...] = _mm(xb, w_ref[:, _C_V:_C_MQK])
    mv_ref[...] = _mm(xb, w_ref[:, _C_MV:_C_MO])
    mo_ref[...] = _mm(xb, w_ref[:, _C_MO:_C_G])

    u = _mm(xb, w_ref[:, _C_MQK:_C_MV])
    conv_s[SUBLANES:SUBLANES + tm, :] = u
    acc = bconv_ref[...] + u * wconv_ref[CONV_WIDTH - 1:CONV_WIDTH, :]
    for j in range(CONV_WIDTH - 1):
        back = CONV_WIDTH - 1 - j
        acc = acc + conv_s[SUBLANES - back:SUBLANES - back + tm, :] * wconv_ref[j:j + 1, :]
    qk = acc * jax.nn.sigmoid(acc)
    mq_ref[...] = qk[:, 0:WIDTH]
    mk_ref[...] = qk[:, WIDTH:2 * WIDTH] * QK_SCALE
    tail = conv_s[tm:tm + SUBLANES, :]
    tail_ref[...] = tail
    conv_s[0:SUBLANES, :] = tail


def _aug_placement():
    p2q = np.zeros((LANES, N_HEADS * LANES), np.float32)
    p2k = np.zeros((LANES, N_HEADS * LANES), np.float32)
    for h in range(N_HEADS):
        base = h * LANES + _AUG
        for part in range(3):
            p2q[part * 8 + h, base + part] = 1.0
            p2q[24, base + 3 + part] = 1.0
            p2k[24, base + part] = 1.0
            p2k[part * 8 + h, base + 3 + part] = -1.0
    return jnp.asarray(p2q, BF16), jnp.asarray(p2k, BF16)


def _split_w_in(w):
    d = w.shape[0]
    sizes = [WIDTH, WIDTH, WIDTH, N_HEADS, 2 * WIDTH, WIDTH, N_HEADS, N_HEADS, WIDTH]
    pts = np.cumsum([0] + sizes)
    parts = [w[:, pts[j]:pts[j + 1]] for j in range(len(sizes))]
    fq, fk, fv, ff, mqk, mv, mi, mf, mo = parts
    gates = jnp.concatenate([ff, mi, mf, jnp.zeros((d, LANES - 3 * N_HEADS), w.dtype)], axis=1)
    return fq, fk, fv, mqk, mv, mo, gates


def _gate_bias(b_fox_f, b_ml_i, b_ml_f):
    return jnp.concatenate([b_fox_f, b_ml_i, b_ml_f, jnp.zeros((LANES - 3 * N_HEADS,), F32)])


def _pad_heads(w):
    d = w.shape[0]
    w3 = w.reshape(d, N_HEADS, HEAD_DIM)
    return jnp.pad(w3, ((0, 0), (0, 0), (0, LANES - HEAD_DIM))).reshape(d, N_HEADS * LANES)


def _inproj_prompt(x, g_mix, w_in, gate_bias, w_conv, b_conv, tm=256):
    s, d = x.shape
    fq, fk, fv, mqk, mv, mo, gates = _split_w_in(w_in)
    w_all = jnp.concatenate([_pad_heads(fq), _pad_heads(fk), fk, fv, mqk, mv, mo, gates], axis=1).astype(BF16)
    w_t = jnp.concatenate([fv.T, gates[:, 0:32].T], axis=0).astype(BF16)
    p2q, p2k = _aug_placement()
    nblk = s // tm
    full = lambda shape: pl.BlockSpec(shape, lambda i: (0,) * len(shape))
    rows = lambda width: pl.BlockSpec((tm, width), lambda i: (i, 0))
    out_shape = (
        jax.ShapeDtypeStruct((N_HEADS, s, LANES), BF16),
        jax.ShapeDtypeStruct((N_HEADS, s, LANES), BF16),
        jax.ShapeDtypeStruct((s, WIDTH), F32),
        jax.ShapeDtypeStruct((s, WIDTH), F32),
        jax.ShapeDtypeStruct((nblk, WIDTH, tm), BF16),
        jax.ShapeDtypeStruct((s, LANES), F32),
        jax.ShapeDtypeStruct((32, s), F32),
        jax.ShapeDtypeStruct((s, WIDTH), F32),
        jax.ShapeDtypeStruct((s, WIDTH), F32),
        jax.ShapeDtypeStruct((s, WIDTH), F32),
        jax.ShapeDtypeStruct((s, WIDTH), F32),
        jax.ShapeDtypeStruct((SUBLANES, 2 * WIDTH), F32),
    )
    out_specs = (
        pl.BlockSpec((N_HEADS, tm, LANES), lambda i: (0, i, 0)),
        pl.BlockSpec((N_HEADS, tm, LANES), lambda i: (0, i, 0)),
        rows(WIDTH), rows(WIDTH),
        pl.BlockSpec((1, WIDTH, tm), lambda i: (i, 0, 0)),
        rows(LANES),
        pl.BlockSpec((32, tm), lambda i: (0, i)),
        rows(WIDTH), rows(WIDTH), rows(WIDTH), rows(WIDTH),
        full((SUBLANES, 2 * WIDTH)),
    )
    return pl.pallas_call(
        functools.partial(_inproj_prompt_kernel, tm=tm),
        grid=(nblk,),
        in_specs=[rows(d), full((1, d)), full(w_all.shape), full(w_t.shape), full((1, LANES)),
                  full((32, 1)), full((CONV_WIDTH, 2 * WIDTH)), full((1, 2 * WIDTH)),
                  full(p2q.shape), full(p2k.shape)],
        out_specs=out_specs,
        out_shape=out_shape,
        scratch_shapes=[pltpu.VMEM((tm + 2 * SUBLANES, 2 * WIDTH), F32),
                        pltpu.VMEM((1, LANES), F32),
                        pltpu.VMEM((SUBLANES, LANES), F32)],
        compiler_params=_params(("arbitrary",)),
        name="inproj_prompt",
    )(x, g_mix.reshape(1, d), w_all, w_t, gate_bias.reshape(1, LANES), gate_bias[0:32].reshape(32, 1),
      w_conv, b_conv.reshape(1, 2 * WIDTH), p2q, p2k)


_S_Q, _S_K, _S_V, _S_MQK, _S_MV, _S_MO, _S_G, _S_END = 0, 512, 1024, 1536, 2560, 3072, 3584, 3712


def _inproj_sample_kernel(x_ref, gmix_ref, w_ref, brow_ref, wconv_ref, bconv_ref, cstate_ref,
                          q_ref, k_ref, v_ref, g_ref, mq_ref, mk_ref, mv_ref, mo_ref, cnew_ref):
    xn = _rms(x_ref[...], gmix_ref[...])
    q_ref[...] = _mm_f32(xn, w_ref[:, _S_Q:_S_K])
    k_ref[...] = _mm_f32(xn, w_ref[:, _S_K:_S_V])
    v_ref[...] = _mm_f32(xn, w_ref[:, _S_V:_S_MQK])
    mv_ref[...] = _mm_f32(xn, w_ref[:, _S_MV:_S_MO])
    mo_ref[...] = _mm_f32(xn, w_ref[:, _S_MO:_S_G])
    g_ref[...] = _gate_tile(_mm_f32(xn, w_ref[:, _S_G:_S_END]) + brow_ref[...], 1)
    u = _mm_f32(xn, w_ref[:, _S_MQK:_S_MV])
    acc = bconv_ref[...] + u * wconv_ref[CONV_WIDTH - 1:CONV_WIDTH, :]
    for j in range(CONV_WIDTH - 1):
        acc = acc + cstate_ref[j] * wconv_ref[j:j + 1, :]
    qk = acc * jax.nn.sigmoid(acc)
    mq_ref[...] = qk[:, 0:WIDTH]
    mk_ref[...] = qk[:, WIDTH:2 * WIDTH] * QK_SCALE
    for j in range(CONV_WIDTH - 2):
        cnew_ref[j] = cstate_ref[j + 1]
    cnew_ref[CONV_WIDTH - 2] = u


def _inproj_sample(x, g_mix, w_in, gate_bias, w_conv, b_conv, conv_state_t):
    n, d = x.shape
    fq, fk, fv, mqk, mv, mo, gates = _split_w_in(w_in)
    w_all = jnp.concatenate([fq, fk, fv, mqk, mv, mo, gates], axis=1)
    wide = jax.ShapeDtypeStruct((n, WIDTH), F32)
    out_shape = (wide, wide, wide, jax.ShapeDtypeStruct((n, LANES), F32), wide, wide, wide, wide,
                 jax.ShapeDtypeStruct((CONV_WIDTH - 1, n, 2 * WIDTH), F32))
    return pl.pallas_call(
        _inproj_sample_kernel,
        out_shape=out_shape,
        compiler_params=pltpu.CompilerParams(vmem_limit_bytes=VMEM_LIMIT),
        name="inproj_sample",
    )(x, g_mix.reshape(1, d), w_all, gate_bias.reshape(1, LANES), w_conv, b_conv.reshape(1, 2 * WIDTH),
      conv_state_t)


def _fox_prompt_kernel(qa_ref, ka_ref, vt_ref, o_ref, *, t):
    qi = pl.program_id(1)
    q = qa_ref[0]

    def block(j, carry, masked):
        m, l, acc = carry
        kblk = ka_ref[0, pl.ds(pl.multiple_of(j * t, t), t), :]
        s = _mm_nt(kblk, q)
        if masked:
            kpos = lax.broadcasted_iota(jnp.int32, (t, t), 0)
            qpos = lax.broadcasted_iota(jnp.int32, (t, t), 1)
            s = jnp.where(kpos <= qpos, s, NEG_INF)
        m_new = jnp.maximum(m, jnp.max(s, axis=0, keepdims=True))
        alpha = jnp.exp(m - m_new)
        p = jnp.exp(s - m_new)
        l = alpha * l + jnp.sum(p, axis=0, keepdims=True)
        acc = alpha * acc + _mm(vt_ref[j], p.astype(BF16))
        return m_new, l, acc

    init = (jnp.full((1, t), NEG_INF, F32), jnp.zeros((1, t), F32), jnp.zeros((HEAD_DIM, t), F32))
    carry = lax.fori_loop(0, qi, lambda j, c: block(j, c, False), init)
    m, l, acc = block(qi, carry, True)
    o_ref[...] = acc / l


def _fox_prompt(qa, ka, vt, t):
    _, s, _ = qa.shape
    nblk = s // t
    return pl.pallas_call(
        functools.partial(_fox_prompt_kernel, t=t),
        grid=(N_HEADS, nblk),
        in_specs=[pl.BlockSpec((1, t, LANES), lambda h, i: (h, i, 0)),
                  pl.BlockSpec((1, s, LANES), lambda h, i: (h, 0, 0)),
                  pl.BlockSpec((nblk, HEAD_DIM, t), lambda h, i: (0, h, 0))],
        out_specs=pl.BlockSpec((HEAD_DIM, t), lambda h, i: (h, i)),
        out_shape=jax.ShapeDtypeStruct((WIDTH, s), F32),
        compiler_params=_params(("arbitrary", "arbitrary")),
        name="fox_prompt",
    )(qa, ka, vt)


_PAGES_PER_STEP = 8


def _fox_sample_kernel(pt_ref, q_ref, kn_ref, vn_ref, lfn_ref, eye_ref, tsuf_ref, *rest, page, npp):
    del pt_ref
    k_refs = rest[0:npp]
    v_refs = rest[npp:2 * npp]
    lf_refs = rest[2 * npp:3 * npp]
    o_ref = rest[3 * npp]
    m_s, l_s, acc_s, carry_s = rest[3 * npp + 1:]
    g = pl.program_id(1)
    q = q_ref[0] * QK_SCALE

    @pl.when(g == 0)
    def _():
        m_s[...] = jnp.broadcast_to(jnp.sum(q * kn_ref[0], axis=1, keepdims=True), m_s.shape)
        l_s[...] = jnp.ones(l_s.shape, F32)
        acc_s[...] = vn_ref[0]
        carry_s[...] = lfn_ref[0]

    sel = (lax.broadcasted_iota(jnp.int32, (page, N_HEADS, page), 0)
           == lax.broadcasted_iota(jnp.int32, (page, N_HEADS, page), 2))
    eye = eye_ref[...]
    tsuf = tsuf_ref[...]
    for r in range(npp):
        qk_key = jnp.sum(k_refs[r][...] * q[None], axis=-1, keepdims=True)
        qk = jnp.sum(jnp.where(sel, qk_key, 0.0), axis=0)
        lft = [_mm_nt(eye, part) for part in _split3(lf_refs[r][...])]
        carry = carry_s[...][:, 0:1]
        bias = carry + _mm(lft[0].astype(BF16), tsuf) + _mm(lft[1].astype(BF16), tsuf) \
            + _mm(lft[2].astype(BF16), tsuf)
        carry_s[...] = jnp.broadcast_to(
            carry + jnp.sum(lft[0] + lft[1] + lft[2], axis=1, keepdims=True), carry_s.shape)
        s = qk + bias
        m = m_s[...][:, 0:1]
        m_new = jnp.maximum(m, jnp.max(s, axis=1, keepdims=True))
        alpha = jnp.exp(m - m_new)
        p = jnp.exp(s - m_new)
        l_s[...] = jnp.broadcast_to(alpha * l_s[...][:, 0:1] + jnp.sum(p, axis=1, keepdims=True), l_s.shape)
        p_key = jnp.sum(jnp.where(sel, p[None], 0.0), axis=-1, keepdims=True)
        acc_s[...] = alpha * acc_s[...] + jnp.sum(p_key * v_refs[r][...], axis=0)
        m_s[...] = jnp.broadcast_to(m_new, m_s.shape)

    @pl.when(g == pl.num_programs(1) - 1)
    def _():
        o_ref[0] = acc_s[...] / l_s[...][:, 0:1]


def _fox_sample(q, k_new, v_new, logf_new, cache_k, cache_v, cache_logf, page_table):
    nb, n_pages = page_table.shape
    page = cache_k.shape[2]
    npp = _PAGES_PER_STEP
    q3, kn3, vn3 = (a.reshape(nb, N_HEADS, HEAD_DIM) for a in (q, k_new, v_new))
    lfn = jnp.broadcast_to(logf_new[:, :, None], (nb, N_HEADS, LANES))
    eye = jnp.eye(N_HEADS, dtype=BF16)
    tsuf = jnp.asarray(np.tril(np.ones((page, page), np.float32), -1), BF16)

    def page_map(r):
        return lambda b, g, pt: (0, pt[b, n_pages - 1 - (g * npp + r)], 0, 0, 0)

    def lf_map(r):
        return lambda b, g, pt: (0, pt[b, n_pages - 1 - (g * npp + r)], 0, 0)

    per_b = pl.BlockSpec((1, N_HEADS, HEAD_DIM), lambda b, g, pt: (b, 0, 0))
    kv_specs = [pl.BlockSpec((None, None, page, N_HEADS, HEAD_DIM), page_map(r)) for r in range(npp)]
    lf_specs = [pl.BlockSpec((None, None, page, N_HEADS), lf_map(r)) for r in range(npp)]
    grid_spec = pltpu.PrefetchScalarGridSpec(
        num_scalar_prefetch=1,
        grid=(nb, n_pages // npp),
        in_specs=[per_b, per_b, per_b,
                  pl.BlockSpec((1, N_HEADS, LANES), lambda b, g, pt: (b, 0, 0)),
                  pl.BlockSpec((N_HEADS, N_HEADS), lambda b, g, pt: (0, 0)),
                  pl.BlockSpec((page, page), lambda b, g, pt: (0, 0))] + kv_specs + kv_specs + lf_specs,
        out_specs=per_b,
        scratch_shapes=[pltpu.VMEM((N_HEADS, LANES), F32), pltpu.VMEM((N_HEADS, LANES), F32),
                        pltpu.VMEM((N_HEADS, HEAD_DIM), F32), pltpu.VMEM((N_HEADS, LANES), F32)],
    )
    out = pl.pallas_call(
        functools.partial(_fox_sample_kernel, page=page, npp=npp),
        grid_spec=grid_spec,
        out_shape=jax.ShapeDtypeStruct((nb, N_HEADS, HEAD_DIM), F32),
        compiler_params=_params(("arbitrary", "arbitrary")),
        name="fox_sample",
    )(page_table, q3, kn3, vn3, lfn, eye, tsuf, *([cache_k] * npp), *([cache_v] * npp), *([cache_logf] * npp))
    return out.reshape(nb, WIDTH)


def _gated_head_norm(h, o_pre, g):
    hg = h * jax.nn.sigmoid(o_pre)
    return hg * lax.rsqrt(jnp.mean(hg * hg, axis=-1, keepdims=True) + RMS_EPS) * g


def _mlstm_prompt_kernel(q_ref, k_ref, v_ref, o_ref, gcol_ref, grow_ref, gml_ref,
                         y_ref, c_ref, n_ref, m_ref, *, chunk):
    @pl.when(pl.program_id(0) == 0)
    def _():
        c_ref[...] = jnp.zeros(c_ref.shape, F32)
        n_ref[...] = jnp.zeros(n_ref.shape, F32)
        m_ref[...] = jnp.zeros(m_ref.shape, F32)

    r = lax.broadcasted_iota(jnp.int32, (chunk, chunk), 0)
    c = lax.broadcasted_iota(jnp.int32, (chunk, chunk), 1)
    causal = c <= r
    gcol = gcol_ref[...]
    grow = grow_ref[...]
    a_col_all = _mm3_left(causal.astype(BF16), gcol)
    a_row_all = _mm3_right(grow[_G_LF:_G_LF + 8], (r <= c).astype(BF16))
    for h in range(N_HEADS):
        sl = slice(h * HEAD_DIM, (h + 1) * HEAD_DIM)
        q = q_ref[:, sl]
        k = k_ref[:, sl]
        v = v_ref[:, sl]
        a_c = a_col_all[:, _G_LF + h:_G_LF + h + 1]
        ig_c = gcol[:, _G_IG + h:_G_IG + h + 1]
        a_r = a_row_all[h:h + 1, :]
        ig_r = grow[_G_IG + h:_G_IG + h + 1, :]
        m_prev = m_ref[h][:, 0:1]
        c_prev = c_ref[h]
        n_prev = n_ref[h]
        d = jnp.where(causal, a_c - a_r + ig_r, NEG_INF)
        b = a_c + m_prev
        m_t = jnp.maximum(b, jnp.max(d, axis=1, keepdims=True))
        w_intra = jnp.exp(d - m_t)
        w_inter = jnp.exp(b - m_t)
        qb = q.astype(BF16)
        scores = _mm_nt(qb, k.astype(BF16)) * w_intra
        num = w_inter * _mm(qb, c_prev.astype(BF16)) + _mm(scores.astype(BF16), v.astype(BF16))
        den = w_inter * jnp.sum(q * n_prev, axis=1, keepdims=True) + jnp.sum(scores, axis=1, keepdims=True)
        hh = num / jnp.maximum(jnp.abs(den), jnp.exp(-m_t))
        m_new = m_t[chunk - 1:chunk, :]
        a_last = a_c[chunk - 1:chunk, :]
        decay = jnp.exp(a_last + m_prev - m_new)
        w_write = jnp.exp(a_last - a_c + ig_c - m_new)
        kw = k * w_write
        c_ref[h] = decay * c_prev + _mm_tn(kw.astype(BF16), v.astype(BF16))
        n_ref[h] = decay * n_prev + jnp.sum(kw, axis=0, keepdims=True)
        m_ref[h] = jnp.broadcast_to(m_new, (1, LANES))
        y_ref[:, sl] = _gated_head_norm(hh, o_ref[:, sl], gml_ref[:, sl])


def _mlstm_prompt(mq, mk, mv, mo, gcol, grow, g_ml):
    s = mq.shape[0]
    chunk = int(np.gcd(s, MLSTM_CHUNK))
    rows = pl.BlockSpec((chunk, WIDTH), lambda i: (i, 0))
    state = lambda shape: pl.BlockSpec(shape, lambda i: (0,) * len(shape))
    return pl.pallas_call(
        functools.partial(_mlstm_prompt_kernel, chunk=chunk),
        grid=(s // chunk,),
        in_specs=[rows, rows, rows, rows, pl.BlockSpec((chunk, LANES), lambda i: (i, 0)),
                  pl.BlockSpec((32, chunk), lambda i: (0, i)), state((1, WIDTH))],
        out_specs=(rows, state((N_HEADS, HEAD_DIM, HEAD_DIM)), state((N_HEADS, 1, HEAD_DIM)),
                   state((N_HEADS, 1, LANES))),
        out_shape=(jax.ShapeDtypeStruct((s, WIDTH), F32),
                   jax.ShapeDtypeStruct((N_HEADS, HEAD_DIM, HEAD_DIM), F32),
                   jax.ShapeDtypeStruct((N_HEADS, 1, HEAD_DIM), F32),
                   jax.ShapeDtypeStruct((N_HEADS, 1, LANES), F32)),
        compiler_params=_params(("arbitrary",)),
        name="mlstm_prompt",
    )(mq, mk, mv, mo, gcol, grow, g_ml.reshape(1, WIDTH))


def _mlstm_sample_kernel(q_ref, k_ref, v_ref, o_ref, ig_ref, lf_ref, m_ref, c_ref, n_ref, gml_ref,
                         y_ref, cn_ref, nn_ref, mn_ref):
    r = lax.broadcasted_iota(jnp.int32, (HEAD_DIM, HEAD_DIM), 0)
    c = lax.broadcasted_iota(jnp.int32, (HEAD_DIM, HEAD_DIM), 1)
    eye = r == c

    def column(row):
        return jnp.sum(jnp.where(eye, jnp.broadcast_to(row, (HEAD_DIM, HEAD_DIM)), 0.0), axis=1, keepdims=True)

    for h in range(N_HEADS):
        q = q_ref[0, h:h + 1, :]
        k = k_ref[0, h:h + 1, :]
        v = v_ref[0, h:h + 1, :]
        ig = ig_ref[0, h:h + 1, 0:1]
        lf = lf_ref[0, h:h + 1, 0:1]
        m_prev = m_ref[0, h:h + 1, 0:1]
        c_prev = c_ref[0, h]
        n_prev = n_ref[0, h:h + 1, :]
        b = lf + m_prev
        m_t = jnp.maximum(b, ig)
        w_intra = jnp.exp(ig - m_t)
        w_inter = jnp.exp(b - m_t)
        scores = jnp.sum(q * k, axis=1, keepdims=True) * w_intra
        qc = jnp.sum(column(q) * c_prev, axis=0, keepdims=True)
        num = w_inter * qc + scores * v
        den = w_inter * jnp.sum(q * n_prev, axis=1, keepdims=True) + scores
        hh = num / jnp.maximum(jnp.abs(den), jnp.exp(-m_t))
        cn_ref[0, h] = w_inter * c_prev + w_intra * (column(k) * v)
        nn_ref[0, h:h + 1, :] = w_inter * n_prev + w_intra * k
        mn_ref[0, h:h + 1, :] = jnp.broadcast_to(m_t, (1, LANES))
        y_ref[0, h:h + 1, :] = _gated_head_norm(hh, o_ref[0, h:h + 1, :], gml_ref[h:h + 1, :])


def _mlstm_sample(mq, mk, mv, mo, ig, lf, state_c, state_n, state_m, g_ml):
    nb = mq.shape[0]
    heads = lambda a: a.reshape(nb, N_HEADS, HEAD_DIM)
    lanes = lambda a: jnp.broadcast_to(a[:, :, None], (nb, N_HEADS, LANES))
    vec = pl.BlockSpec((1, N_HEADS, HEAD_DIM), lambda b: (b, 0, 0))
    sca = pl.BlockSpec((1, N_HEADS, LANES), lambda b: (b, 0, 0))
    mat = pl.BlockSpec((1, N_HEADS, HEAD_DIM, HEAD_DIM), lambda b: (b, 0, 0, 0))
    y, cn, nn, mn = pl.pallas_call(
        _mlstm_sample_kernel,
        grid=(nb,),
        in_specs=[vec, vec, vec, vec, sca, sca, sca, mat, vec,
                  pl.BlockSpec((N_HEADS, HEAD_DIM), lambda b: (0, 0))],
        out_specs=(vec, mat, vec, sca),
        out_shape=(jax.ShapeDtypeStruct((nb, N_HEADS, HEAD_DIM), F32),
                   jax.ShapeDtypeStruct((nb, N_HEADS, HEAD_DIM, HEAD_DIM), F32),
                   jax.ShapeDtypeStruct((nb, N_HEADS, HEAD_DIM), F32),
                   jax.ShapeDtypeStruct((nb, N_HEADS, LANES), F32)),
        compiler_params=_params(("arbitrary",)),
        name="mlstm_sample",
    )(heads(mq), heads(mk), heads(mv), heads(mo), lanes(ig), lanes(lf), lanes(state_m), state_c, state_n,
      g_ml.reshape(N_HEADS, HEAD_DIM))
    return y.reshape(nb, WIDTH), cn, nn, mn[:, :, 0]


_R_EXPERT, _R_GROUP = 0, N_EXPERTS


def _outproj_router_kernel(yf_ref, yml_ref, x_ref, wf_ref, wm_ref, gfox_ref, gffn_ref, wr_ref, br_ref,
                           x1_ref, xn_ref, comb_ref, *, prompt):
    yf = yf_ref[...]
    if prompt:
        ms = jnp.mean(yf * yf, axis=0, keepdims=True)
        yfn = (yf * lax.rsqrt(ms + RMS_EPS) * gfox_ref[...]).astype(BF16)
        y = _mm_tn(yfn, wf_ref[...]) + _mm(yml_ref[...].astype(BF16), wm_ref[...])
    else:
        y = _mm_f32(_rms(yf, gfox_ref[...]), wf_ref[...]) + _mm_f32(yml_ref[...], wm_ref[...])
    x1 = x_ref[...] + y
    x1_ref[...] = x1
    xn = _rms(x1, gffn_ref[...])
    xb = xn.astype(BF16)
    xn_ref[...] = xb

    router = _mm(xb, wr_ref[...]) if prompt else _mm_f32(xn, wr_ref[...])
    logits = router + br_ref[...]
    lane = lax.broadcasted_iota(jnp.int32, logits.shape, 1)
    big = jnp.int32(2 * LANES)

    def first_argmax(vals):
        top = jnp.max(vals, axis=1, keepdims=True)
        idx = jnp.min(jnp.where(vals == top, lane, big), axis=1, keepdims=True)
        return top, idx

    is_group = (lane >= _R_GROUP) & (lane < _R_GROUP + N_GROUPS)
    lg = jnp.where(is_group, logits, NEG_INF)
    lg_top, lg_idx = first_argmax(lg)
    gate_g = 1.0 / jnp.sum(jnp.exp(lg - lg_top), axis=1, keepdims=True)
    grp = lg_idx - _R_GROUP
    in_grp = (lane >= grp * EXPERTS_PER_GROUP) & (lane < (grp + 1) * EXPERTS_PER_GROUP)
    le = jnp.where(in_grp, logits, NEG_INF)
    top1, idx1 = first_argmax(le)
    top2, idx2 = first_argmax(jnp.where(lane == idx1, NEG_INF, le))
    e2 = jnp.exp(top2 - top1)
    w1 = gate_g / (1.0 + e2)
    w2 = gate_g * e2 / (1.0 + e2)
    comb_ref[...] = jnp.where(lane == idx1, w1, 0.0) + jnp.where(lane == idx2, w2, 0.0)


def _outproj_router(yf, yml, x, w_out, g_fox, g_ffn, w_rg, b_rg, w_re, b_re, tm, prompt):
    n, d = x.shape
    wdt = BF16 if prompt else F32
    wf = w_out[0:WIDTH].astype(wdt)
    wm = w_out[WIDTH:2 * WIDTH].astype(wdt)
    pad = LANES - N_EXPERTS - N_GROUPS
    wr = jnp.concatenate([w_re, w_rg, jnp.zeros((d, pad), F32)], axis=1).astype(wdt)
    br = jnp.concatenate([b_re, b_rg, jnp.zeros((pad,), F32)]).reshape(1, LANES)
    full = lambda shape: pl.BlockSpec(shape, lambda i: (0,) * len(shape))
    rows = lambda width: pl.BlockSpec((tm, width), lambda i: (i, 0))
    if prompt:
        yf_spec = pl.BlockSpec((WIDTH, tm), lambda i: (0, i))
        gfox = g_fox.reshape(WIDTH, 1)
    else:
        yf_spec = rows(WIDTH)
        gfox = g_fox.reshape(1, WIDTH)
    return pl.pallas_call(
        functools.partial(_outproj_router_kernel, prompt=prompt),
        grid=(n // tm,),
        in_specs=[yf_spec, rows(WIDTH), rows(d), full((WIDTH, d)), full((WIDTH, d)), full(gfox.shape),
                  full((1, d)), full((d, LANES)), full((1, LANES))],
        out_specs=(rows(d), rows(d), rows(LANES)),
        out_shape=(jax.ShapeDtypeStruct((n, d), F32), jax.ShapeDtypeStruct((n, d), BF16),
                   jax.ShapeDtypeStruct((n, LANES), F32)),
        compiler_params=_params(("arbitrary",)),
        name="outproj_router_prompt" if prompt else "outproj_router_sample",
    )(yf, yml, x, wf, wm, gfox, g_ffn.reshape(1, d), wr, br)


def _moe_kernel(xn_ref, comb_ref, x1_ref, wg_ref, wu_ref, wd_ref, gfin_ref, y_ref, acc_s):
    e = pl.program_id(1)

    @pl.when(e == 0)
    def _():
        acc_s[...] = jnp.zeros(acc_s.shape, F32)

    xb = xn_ref[...]
    gate = _mm(xb, wg_ref[0])
    he = gate * jax.nn.sigmoid(gate) * _mm(xb, wu_ref[0])
    out = _mm(he.astype(BF16), wd_ref[0])
    comb = comb_ref[...]
    lane = lax.broadcasted_iota(jnp.int32, comb.shape, 1)
    w_e = jnp.sum(jnp.where(lane == e, comb, 0.0), axis=1, keepdims=True)
    acc_s[...] += w_e * out

    @pl.when(e == pl.num_programs(1) - 1)
    def _():
        y_ref[...] = _rms(x1_ref[...] + acc_s[...], gfin_ref[...])


def _moe(xn, comb, x1, wg, wu, wd, g_final, tm):
    n, d = x1.shape
    de = wg.shape[2]
    rows = lambda width: pl.BlockSpec((tm, width), lambda i, e: (i, 0))
    return pl.pallas_call(
        _moe_kernel,
        grid=(n // tm, N_EXPERTS),
        in_specs=[rows(d), rows(LANES), rows(d),
                  pl.BlockSpec((1, d, de), lambda i, e: (e, 0, 0)),
                  pl.BlockSpec((1, d, de), lambda i, e: (e, 0, 0)),
                  pl.BlockSpec((1, de, d), lambda i, e: (e, 0, 0)),
                  pl.BlockSpec((1, d), lambda i, e: (0, 0))],
        out_specs=rows(d),
        out_shape=jax.ShapeDtypeStruct((n, d), F32),
        scratch_shapes=[pltpu.VMEM((tm, d), F32)],
        compiler_params=_params(("arbitrary", "arbitrary")),
        name="moe",
    )(xn, comb, x1, wg, wu, wd, g_final.reshape(1, d))


def kernel(x_prompt, x_sample, cache_k, cache_v, cache_logf, state_conv, state_C, state_n, state_m,
           page_table, g_mix, w_in, b_fox_f, b_ml_i, b_ml_f, w_conv, b_conv, g_fox_out, g_ml_out,
           w_out, g_ffn, w_router_group, b_router_group, w_router_expert, b_router_expert,
           w_exp_gate, w_exp_up, w_exp_down, g_final):
    depth = w_in.shape[0]
    batch, seq, d = x_prompt.shape
    nb, dec_seq, _ = x_sample.shape
    assert depth == 1 and batch == 1 and dec_seq == 1
    l = 0
    gate_bias = _gate_bias(b_fox_f[l], b_ml_i[l], b_ml_f[l])
    wg, wu, wd = (w[l].astype(BF16) for w in (w_exp_gate, w_exp_up, w_exp_down))
    router = (w_router_group[l], b_router_group[l], w_router_expert[l], b_router_expert[l])

    t_attn = min(512, seq)
    (qa, ka, k_p, v_p, vt, gcol, grow, mq, mk, mv, mo, tail) = _inproj_prompt(
        x_prompt[0], g_mix[l], w_in[l], gate_bias, w_conv[l], b_conv[l], tm=t_attn)
    y_fox_t = _fox_prompt(qa, ka, vt, t_attn)
    y_ml, c_p, n_p, m_p = _mlstm_prompt(mq, mk, mv, mo, gcol, grow, g_ml_out[l])
    x1, xn, comb = _outproj_router(y_fox_t, y_ml, x_prompt[0], w_out[l], g_fox_out[l], g_ffn[l], *router,
                                   tm=min(512, seq), prompt=True)
    y_prompt = _moe(xn, comb, x1, wg, wu, wd, g_final, tm=min(1024, seq))

    xs = x_sample[:, 0, :]
    (q_s, k_s, v_s, g_s, mq_s, mk_s, mv_s, mo_s, conv_new) = _inproj_sample(
        xs, g_mix[l], w_in[l], gate_bias, w_conv[l], b_conv[l], jnp.transpose(state_conv[l], (1, 0, 2)))
    logf_s = g_s[:, _G_LOGF:_G_LOGF + N_HEADS]
    y_fox_s = _fox_sample(q_s, k_s, v_s, logf_s, cache_k[l:l + 1], cache_v[l:l + 1], cache_logf[l:l + 1],
                          page_table)
    y_ml_s, c_s, n_s, m_s = _mlstm_sample(mq_s, mk_s, mv_s, mo_s, g_s[:, _G_IG:_G_IG + N_HEADS],
                                          g_s[:, _G_LF:_G_LF + N_HEADS], state_C[l], state_n[l], state_m[l],
                                          g_ml_out[l])
    x1_s, xn_s, comb_s = _outproj_router(y_fox_s, y_ml_s, xs, w_out[l], g_fox_out[l], g_ffn[l], *router,
                                         tm=nb, prompt=False)
    y_sample = _moe(xn_s, comb_s, x1_s, wg, wu, wd, g_final, tm=nb)

    heads = lambda a, n: a.reshape(1, n, -1, N_HEADS, HEAD_DIM)
    return (
        y_prompt[None], y_sample[:, None, :],
        heads(k_p, 1), heads(v_p, 1), gcol[:, _G_LOGF:_G_LOGF + N_HEADS].reshape(1, 1, seq, N_HEADS),
        tail[SUBLANES - (CONV_WIDTH - 1):][None, None],
        c_p[None, None], n_p[:, 0, :][None, None], m_p[:, 0, 0][None, None],
        heads(k_s, nb), heads(v_s, nb), logf_s.reshape(1, nb, 1, N_HEADS),
        jnp.transpose(conv_new, (1, 0, 2))[None],
        c_s[None], n_s[None], m_s[None],
    )
```

```python
import functools

import numpy as np
import jax
import jax.numpy as jnp
from jax import lax
from jax.experimental import pallas as pl
from jax.experimental.pallas import tpu as pltpu

HEAD_DIM = 64
N_HEADS = 8
WIDTH = N_HEADS * HEAD_DIM
CONV_WIDTH = 4
MLSTM_CHUNK = 128
N_GROUPS = 4
EXPERTS_PER_GROUP = 8
N_EXPERTS = N_GROUPS * EXPERTS_PER_GROUP
RMS_EPS = 1e-6
NEG_INF = -1e30
QK_SCALE = HEAD_DIM ** -0.5

LANES = 128
SUBLANES = 8
VMEM_LIMIT = 56 * 1024 * 1024

F32 = jnp.float32
BF16 = jnp.bfloat16


def _mm(a, b):
    return jnp.dot(a, b, preferred_element_type=F32)


def _mm_f32(a, b):
    return jnp.dot(a, b, preferred_element_type=F32, precision=lax.Precision.HIGHEST)


def _mm_nt(a, b):
    return lax.dot_general(a, b, (((1,), (1,)), ((), ())), preferred_element_type=F32)


def _mm_tn(a, b):
    return lax.dot_general(a, b, (((0,), (0,)), ((), ())), preferred_element_type=F32)


def _split3(a):
    hi = a.astype(BF16)
    r = a - hi.astype(F32)
    mid = r.astype(BF16)
    lo = (r - mid.astype(F32)).astype(BF16)
    return hi, mid, lo


def _mm3_right(a, b01):
    hi, mid, lo = _split3(a)
    return _mm(hi, b01) + _mm(mid, b01) + _mm(lo, b01)


def _mm3_left(a01, b):
    hi, mid, lo = _split3(b)
    return _mm(a01, hi) + _mm(a01, mid) + _mm(a01, lo)


def _log_sigmoid(x):
    return jnp.minimum(x, 0.0) - jnp.log1p(jnp.exp(-jnp.abs(x)))


def _rms(x, g):
    return x * lax.rsqrt(jnp.mean(x * x, axis=-1, keepdims=True) + RMS_EPS) * g


def _params(sem):
    return pltpu.CompilerParams(dimension_semantics=sem, vmem_limit_bytes=VMEM_LIMIT)


_C_Q, _C_K, _C_KRAW, _C_V, _C_MQK, _C_MV, _C_MO, _C_G, _C_END = (
    0, 1024, 2048, 2560, 3072, 4096, 4608, 5120, 5248)
_G_LOGF, _G_IG, _G_LF, _G_CUM = 0, 8, 16, 24
_AUG = HEAD_DIM


def _gate_tile(z, lane_axis):
    idx = lax.broadcasted_iota(jnp.int32, z.shape, lane_axis)
    is_ig = (idx >= _G_IG) & (idx < _G_LF)
    return jnp.where(is_ig, z, _log_sigmoid(z))


def _inproj_prompt_kernel(x_ref, gmix_ref, w_ref, wt_ref, brow_ref, bcol_ref, wconv_ref, bconv_ref,
                          p2q_ref, p2k_ref,
                          qa_ref, ka_ref, k_ref, v_ref, vt_ref, gcol_ref, grow_ref,
                          mq_ref, mk_ref, mv_ref, mo_ref, tail_ref,
                          conv_s, ccol_s, crow_s, *, tm):
    i = pl.program_id(0)

    @pl.when(i == 0)
    def _():
        conv_s[0:SUBLANES, :] = jnp.zeros((SUBLANES, conv_s.shape[1]), F32)
        ccol_s[...] = jnp.zeros(ccol_s.shape, F32)
        crow_s[...] = jnp.zeros(crow_s.shape, F32)

    xb = _rms(x_ref[...], gmix_ref[...]).astype(BF16)

    lane = lax.broadcasted_iota(jnp.int32, (tm, LANES), 1)
    g = _gate_tile(_mm(xb, w_ref[:, _C_G:_C_END]) + brow_ref[...], 1)
    zt = _mm_nt(wt_ref[...], xb)
    vt_ref[0] = zt[0:WIDTH].astype(BF16)
    gt = _gate_tile(zt[WIDTH:WIDTH + 32] + bcol_ref[...], 0)

    r = lax.broadcasted_iota(jnp.int32, (tm, tm), 0)
    c = lax.broadcasted_iota(jnp.int32, (tm, tm), 1)
    ltri = (c <= r).astype(BF16)
    utri = (r <= c).astype(BF16)
    cs = _mm3_left(ltri, g) + ccol_s[...]
    ccol_s[...] = cs[tm - 1:tm, :]
    cst = _mm3_right(gt[0:8], utri) + crow_s[...][:, 0:1]
    crow_s[...] = jnp.broadcast_to(cst[:, tm - 1:tm], crow_s.shape)
    in_cum = (lane >= _G_CUM) & (lane < _G_CUM + 8)
    gcol_ref[...] = jnp.where(in_cum, pltpu.roll(cs, _G_CUM, 1), g)
    grow_ref[...] = jnp.concatenate([gt[0:24], cst], axis=0)

    hi, mid, lo = _split3(cs)
    caug = jnp.where(lane < 8, hi.astype(F32),
                     jnp.where(lane < 16, pltpu.roll(mid.astype(F32), 8, 1),
                               jnp.where(lane < 24, pltpu.roll(lo.astype(F32), 16, 1),
                                         jnp.where(lane == 24, 1.0, 0.0)))).astype(BF16)
    zq = _mm(xb, w_ref[:, _C_Q:_C_K]) * QK_SCALE + _mm(caug, p2q_ref[...])
    zk = _mm(xb, w_ref[:, _C_K:_C_KRAW]) + _mm(caug, p2k_ref[...])
    for h in range(N_HEADS):
        qa_ref[h] = zq[:, h * LANES:(h + 1) * LANES].astype(BF16)
        ka_ref[h] = zk[:, h * LANES:(h + 1) * LANES].astype(BF16)

    k_ref[...] = _mm(xb, w_ref[:, _C_KRAW:_C_V])
    v_ref[...] = _mm(xb, w_ref[:, _C_V:_C_MQK])
    mv_ref[...] = _mm(xb, w_ref[:, _C_MV:_C_MO])
    mo_ref[...] = _mm(xb, w_ref[:, _C_MO:_C_G])

    u = _mm(xb, w_ref[:, _C_MQK:_C_MV])
    conv_s[SUBLANES:SUBLANES + tm, :] = u
    acc = bconv_ref[...] + u * wconv_ref[CONV_WIDTH - 1:CONV_WIDTH, :]
    for j in range(CONV_WIDTH - 1):
        back = CONV_WIDTH - 1 - j
        acc = acc + conv_s[SUBLANES - back:SUBLANES - back + tm, :] * wconv_ref[j:j + 1, :]
    qk = acc * jax.nn.sigmoid(acc)
    mq_ref[...] = qk[:, 0:WIDTH]
    mk_ref[...] = qk[:, WIDTH:2 * WIDTH] * QK_SCALE
    tail = conv_s[tm:tm + SUBLANES, :]
    tail_ref[...] = tail
    conv_s[0:SUBLANES, :] = tail


def _aug_placement():
    p2q = np.zeros((LANES, N_HEADS * LANES), np.float32)
    p2k = np.zeros((LANES, N_HEADS * LANES), np.float32)
    for h in range(N_HEADS):
        base = h * LANES + _AUG
        for part in range(3):
            p2q[part * 8 + h, base + part] = 1.0
            p2q[24, base + 3 + part] = 1.0
            p2k[24, base + part] = 1.0
            p2k[part * 8 + h, base + 3 + part] = -1.0
    return jnp.asarray(p2q, BF16), jnp.asarray(p2k, BF16)


def _split_w_in(w):
    d = w.shape[0]
    sizes = [WIDTH, WIDTH, WIDTH, N_HEADS, 2 * WIDTH, WIDTH, N_HEADS, N_HEADS, WIDTH]
    pts = np.cumsum([0] + sizes)
    parts = [w[:, pts[j]:pts[j + 1]] for j in range(len(sizes))]
    fq, fk, fv, ff, mqk, mv, mi, mf, mo = parts
    gates = jnp.concatenate([ff, mi, mf, jnp.zeros((d, LANES - 3 * N_HEADS), w.dtype)], axis=1)
    return fq, fk, fv, mqk, mv, mo, gates


def _gate_bias(b_fox_f, b_ml_i, b_ml_f):
    return jnp.concatenate([b_fox_f, b_ml_i, b_ml_f, jnp.zeros((LANES - 3 * N_HEADS,), F32)])


def _pad_heads(w):
    d = w.shape[0]
    w3 = w.reshape(d, N_HEADS, HEAD_DIM)
    return jnp.pad(w3, ((0, 0), (0, 0), (0, LANES - HEAD_DIM))).reshape(d, N_HEADS * LANES)


def _inproj_prompt(x, g_mix, w_in, gate_bias, w_conv, b_conv, tm=256):
    s, d = x.shape
    fq, fk, fv, mqk, mv, mo, gates = _split_w_in(w_in)
    w_all = jnp.concatenate([_pad_heads(fq), _pad_heads(fk), fk, fv, mqk, mv, mo, gates], axis=1).astype(BF16)
    w_t = jnp.concatenate([fv.T, gates[:, 0:32].T], axis=0).astype(BF16)
    p2q, p2k = _aug_placement()
    nblk = s // tm
    full = lambda shape: pl.BlockSpec(shape, lambda i: (0,) * len(shape))
    rows = lambda width: pl.BlockSpec((tm, width), lambda i: (i, 0))
    out_shape = (
        jax.ShapeDtypeStruct((N_HEADS, s, LANES), BF16),
        jax.ShapeDtypeStruct((N_HEADS, s, LANES), BF16),
        jax.ShapeDtypeStruct((s, WIDTH), F32),
        jax.ShapeDtypeStruct((s, WIDTH), F32),
        jax.ShapeDtypeStruct((nblk, WIDTH, tm), BF16),
        jax.ShapeDtypeStruct((s, LANES), F32),
        jax.ShapeDtypeStruct((32, s), F32),
        jax.ShapeDtypeStruct((s, WIDTH), F32),
        jax.ShapeDtypeStruct((s, WIDTH), F32),
        jax.ShapeDtypeStruct((s, WIDTH), F32),
        jax.ShapeDtypeStruct((s, WIDTH), F32),
        jax.ShapeDtypeStruct((SUBLANES, 2 * WIDTH), F32),
    )
    out_specs = (
        pl.BlockSpec((N_HEADS, tm, LANES), lambda i: (0, i, 0)),
        pl.BlockSpec((N_HEADS, tm, LANES), lambda i: (0, i, 0)),
        rows(WIDTH), rows(WIDTH),
        pl.BlockSpec((1, WIDTH, tm), lambda i: (i, 0, 0)),
        rows(LANES),
        pl.BlockSpec((32, tm), lambda i: (0, i)),
        rows(WIDTH), rows(WIDTH), rows(WIDTH), rows(WIDTH),
        full((SUBLANES, 2 * WIDTH)),
    )
    return pl.pallas_call(
        functools.partial(_inproj_prompt_kernel, tm=tm),
        grid=(nblk,),
        in_specs=[rows(d), full((1, d)), full(w_all.shape), full(w_t.shape), full((1, LANES)),
                  full((32, 1)), full((CONV_WIDTH, 2 * WIDTH)), full((1, 2 * WIDTH)),
                  full(p2q.shape), full(p2k.shape)],
        out_specs=out_specs,
        out_shape=out_shape,
        scratch_shapes=[pltpu.VMEM((tm + 2 * SUBLANES, 2 * WIDTH), F32),
                        pltpu.VMEM((1, LANES), F32),
                        pltpu.VMEM((SUBLANES, LANES), F32)],
        compiler_params=_params(("arbitrary",)),
        name="inproj_prompt",
    )(x, g_mix.reshape(1, d), w_all, w_t, gate_bias.reshape(1, LANES), gate_bias[0:32].reshape(32, 1),
      w_conv, b_conv.reshape(1, 2 * WIDTH), p2q, p2k)


_S_Q, _S_K, _S_V, _S_MQK, _S_MV, _S_MO, _S_G, _S_END = 0, 512, 1024, 1536, 2560, 3072, 3584, 3712


def _inproj_sample_kernel(x_ref, gmix_ref, w_ref, brow_ref, wconv_ref, bconv_ref, cstate_ref,
                          q_ref, k_ref, v_ref, g_ref, mq_ref, mk_ref, mv_ref, mo_ref, cnew_ref):
    xn = _rms(x_ref[...], gmix_ref[...])
    q_ref[...] = _mm_f32(xn, w_ref[:, _S_Q:_S_K])
    k_ref[...] = _mm_f32(xn, w_ref[:, _S_K:_S_V])
    v_ref[...] = _mm_f32(xn, w_ref[:, _S_V:_S_MQK])
    mv_ref[...] = _mm_f32(xn, w_ref[:, _S_MV:_S_MO])
    mo_ref[...] = _mm_f32(xn, w_ref[:, _S_MO:_S_G])
    g_ref[...] = _gate_tile(_mm_f32(xn, w_ref[:, _S_G:_S_END]) + brow_ref[...], 1)
    u = _mm_f32(xn, w_ref[:, _S_MQK:_S_MV])
    acc = bconv_ref[...] + u * wconv_ref[CONV_WIDTH - 1:CONV_WIDTH, :]
    for j in range(CONV_WIDTH - 1):
        acc = acc + cstate_ref[j] * wconv_ref[j:j + 1, :]
    qk = acc * jax.nn.sigmoid(acc)
    mq_ref[...] = qk[:, 0:WIDTH]
    mk_ref[...] = qk[:, WIDTH:2 * WIDTH] * QK_SCALE
    for j in range(CONV_WIDTH - 2):
        cnew_ref[j] = cstate_ref[j + 1]
    cnew_ref[CONV_WIDTH - 2] = u


def _inproj_sample(x, g_mix, w_in, gate_bias, w_conv, b_conv, conv_state_t):
    n, d = x.shape
    fq, fk, fv, mqk, mv, mo, gates = _split_w_in(w_in)
    w_all = jnp.concatenate([fq, fk, fv, mqk, mv, mo, gates], axis=1)
    wide = jax.ShapeDtypeStruct((n, WIDTH), F32)
    out_shape = (wide, wide, wide, jax.ShapeDtypeStruct((n, LANES), F32), wide, wide, wide, wide,
                 jax.ShapeDtypeStruct((CONV_WIDTH - 1, n, 2 * WIDTH), F32))
    return pl.pallas_call(
        _inproj_sample_kernel,
        out_shape=out_shape,
        compiler_params=pltpu.CompilerParams(vmem_limit_bytes=VMEM_LIMIT),
        name="inproj_sample",
    )(x, g_mix.reshape(1, d), w_all, gate_bias.reshape(1, LANES), w_conv, b_conv.reshape(1, 2 * WIDTH),
      conv_state_t)


def _fox_prompt_kernel(qa_ref, ka_ref, vt_ref, o_ref, *, t):
    qi = pl.program_id(1)
    q = qa_ref[0]

    def block(j, carry, masked):
        m, l, acc = carry
        kblk = ka_ref[0, pl.ds(pl.multiple_of(j * t, t), t), :]
        s = _mm_nt(kblk, q)
        if masked:
            kpos = lax.broadcasted_iota(jnp.int32, (t, t), 0)
            qpos = lax.broadcasted_iota(jnp.int32, (t, t), 1)
            s = jnp.where(kpos <= qpos, s, NEG_INF)
        m_new = jnp.maximum(m, jnp.max(s, axis=0, keepdims=True))
        alpha = jnp.exp(m - m_new)
        p = jnp.exp(s - m_new)
        l = alpha * l + jnp.sum(p, axis=0, keepdims=True)
        acc = alpha * acc + _mm(vt_ref[j], p.astype(BF16))
        return m_new, l, acc

    init = (jnp.full((1, t), NEG_INF, F32), jnp.zeros((1, t), F32), jnp.zeros((HEAD_DIM, t), F32))
    carry = lax.fori_loop(0, qi, lambda j, c: block(j, c, False), init)
    m, l, acc = block(qi, carry, True)
    o_ref[...] = acc / l


def _fox_prompt(qa, ka, vt, t):
    _, s, _ = qa.shape
    nblk = s // t
    return pl.pallas_call(
        functools.partial(_fox_prompt_kernel, t=t),
        grid=(N_HEADS, nblk),
        in_specs=[pl.BlockSpec((1, t, LANES), lambda h, i: (h, i, 0)),
                  pl.BlockSpec((1, s, LANES), lambda h, i: (h, 0, 0)),
                  pl.BlockSpec((nblk, HEAD_DIM, t), lambda h, i: (0, h, 0))],
        out_specs=pl.BlockSpec((HEAD_DIM, t), lambda h, i: (h, i)),
        out_shape=jax.ShapeDtypeStruct((WIDTH, s), F32),
        compiler_params=_params(("arbitrary", "arbitrary")),
        name="fox_prompt",
    )(qa, ka, vt)


_PAGES_PER_STEP = 8


def _fox_sample_kernel(pt_ref, qb_ref, vnb_ref, q_ref, kn_ref, lfn_ref, tsuf_ref, *rest, page, npp):
    del pt_ref
    k_refs = rest[0:npp]
    v_refs = rest[npp:2 * npp]
    lf_refs = rest[2 * npp:3 * npp]
    o_ref = rest[3 * npp]
    qs_s, m_s, l_s, acc_s, carry_s = rest[3 * npp + 1:]
    g = pl.program_id(1)

    @pl.when(g == 0)
    def _():
        qs_s[...] = qb_ref[0] * QK_SCALE
        s_self = jnp.sum(q_ref[0] * QK_SCALE * kn_ref[0], axis=1, keepdims=True)
        m_s[...] = jnp.broadcast_to(s_self, m_s.shape)
        l_s[...] = jnp.ones(l_s.shape, F32)
        lane = lax.broadcasted_iota(jnp.int32, acc_s.shape, 2)
        acc_s[...] = jnp.where(lane == 0, vnb_ref[0], 0.0)
        carry_s[...] = lfn_ref[0]

    tsuf = tsuf_ref[...]
    carry = carry_s[...][:, 0:1]
    scores = []
    for r in range(npp):
        lf = lf_refs[r][...]
        bias = carry + _mm3_right(lf, tsuf)
        carry = carry + jnp.sum(lf, axis=1, keepdims=True)
        qk = jnp.concatenate(
            [jnp.sum(k_refs[r][h] * qs_s[h], axis=0, keepdims=True) for h in range(N_HEADS)], axis=0)
        scores.append(qk + bias)
    carry_s[...] = jnp.broadcast_to(carry, carry_s.shape)

    m = m_s[...][:, 0:1]
    m_new = m
    for s in scores:
        m_new = jnp.maximum(m_new, jnp.max(s, axis=1, keepdims=True))
    alpha = jnp.exp(m - m_new)
    probs = [jnp.exp(s - m_new) for s in scores]
    l_new = alpha * l_s[...][:, 0:1]
    for p in probs:
        l_new = l_new + jnp.sum(p, axis=1, keepdims=True)
    l_s[...] = jnp.broadcast_to(l_new, l_s.shape)
    m_s[...] = jnp.broadcast_to(m_new, m_s.shape)
    for h in range(N_HEADS):
        upd = alpha[h:h + 1, :] * acc_s[h]
        for r in range(npp):
            upd = upd + probs[r][h:h + 1, :] * v_refs[r][h]
        acc_s[h] = upd

    @pl.when(g == pl.num_programs(1) - 1)
    def _():
        rr = lax.broadcasted_iota(jnp.int32, (HEAD_DIM, HEAD_DIM), 0)
        cc = lax.broadcasted_iota(jnp.int32, (HEAD_DIM, HEAD_DIM), 1)
        for h in range(N_HEADS):
            col = jnp.sum(acc_s[h], axis=1, keepdims=True) / l_s[...][h:h + 1, 0:1]
            row = jnp.sum(jnp.where(rr == cc, jnp.broadcast_to(col, (HEAD_DIM, HEAD_DIM)), 0.0),
                          axis=0, keepdims=True)
            o_ref[0, h:h + 1, :] = row


def _fox_sample(q, k_new, v_new, logf_new, cache_k, cache_v, cache_logf, page_table):
    nb, n_pages = page_table.shape
    page = cache_k.shape[2]
    npp = _PAGES_PER_STEP
    assert page == LANES and n_pages % npp == 0
    kt = jnp.transpose(cache_k, (0, 1, 3, 4, 2))
    vt = jnp.transpose(cache_v, (0, 1, 3, 4, 2))
    lft = jnp.transpose(cache_logf, (0, 1, 3, 2))
    q3, kn3, vn3 = (a.reshape(nb, N_HEADS, HEAD_DIM) for a in (q, k_new, v_new))
    lanes = lambda a: jnp.broadcast_to(a[..., None], a.shape + (LANES,))
    tsuf = jnp.asarray(np.tril(np.ones((page, page), np.float32), -1), BF16)

    def page_map(r):
        return lambda b, g, pt: (0, pt[b, n_pages - 1 - (g * npp + r)], 0, 0, 0)

    def lf_map(r):
        return lambda b, g, pt: (0, pt[b, n_pages - 1 - (g * npp + r)], 0, 0)

    per_b = pl.BlockSpec((1, N_HEADS, HEAD_DIM), lambda b, g, pt: (b, 0, 0))
    per_b_lanes = pl.BlockSpec((1, N_HEADS, HEAD_DIM, LANES), lambda b, g, pt: (b, 0, 0, 0))
    kv_specs = [pl.BlockSpec((None, None, N_HEADS, HEAD_DIM, page), page_map(r)) for r in range(npp)]
    lf_specs = [pl.BlockSpec((None, None, N_HEADS, page), lf_map(r)) for r in range(npp)]
    stat = pltpu.VMEM((N_HEADS, LANES), F32)
    grid_spec = pltpu.PrefetchScalarGridSpec(
        num_scalar_prefetch=1,
        grid=(nb, n_pages // npp),
        in_specs=[per_b_lanes, per_b_lanes, per_b, per_b,
                  pl.BlockSpec((1, N_HEADS, LANES), lambda b, g, pt: (b, 0, 0)),
                  pl.BlockSpec((page, page), lambda b, g, pt: (0, 0))] + kv_specs + kv_specs + lf_specs,
        out_specs=per_b,
        scratch_shapes=[pltpu.VMEM((N_HEADS, HEAD_DIM, LANES), F32), stat, stat,
                        pltpu.VMEM((N_HEADS, HEAD_DIM, page), F32), stat],
    )
    out = pl.pallas_call(
        functools.partial(_fox_sample_kernel, page=page, npp=npp),
        grid_spec=grid_spec,
        out_shape=jax.ShapeDtypeStruct((nb, N_HEADS, HEAD_DIM), F32),
        compiler_params=_params(("arbitrary", "arbitrary")),
        name="fox_sample",
    )(page_table, lanes(q3), lanes(vn3), q3, kn3, lanes(logf_new), tsuf,
      *([kt] * npp), *([vt] * npp), *([lft] * npp))
    return out.reshape(nb, WIDTH)


def _gated_head_norm(h, o_pre, g):
    hg = h * jax.nn.sigmoid(o_pre)
    return hg * lax.rsqrt(jnp.mean(hg * hg, axis=-1, keepdims=True) + RMS_EPS) * g


def _mlstm_prompt_kernel(q_ref, k_ref, v_ref, o_ref, gcol_ref, grow_ref, gml_ref,
                         y_ref, c_ref, n_ref, m_ref, *, chunk):
    @pl.when(pl.program_id(0) == 0)
    def _():
        c_ref[...] = jnp.zeros(c_ref.shape, F32)
        n_ref[...] = jnp.zeros(n_ref.shape, F32)
        m_ref[...] = jnp.zeros(m_ref.shape, F32)

    r = lax.broadcasted_iota(jnp.int32, (chunk, chunk), 0)
    c = lax.broadcasted_iota(jnp.int32, (chunk, chunk), 1)
    causal = c <= r
    gcol = gcol_ref[...]
    grow = grow_ref[...]
    a_col_all = _mm3_left(causal.astype(BF16), gcol)
    a_row_all = _mm3_right(grow[_G_LF:_G_LF + 8], (r <= c).astype(BF16))
    for h in range(N_HEADS):
        sl = slice(h * HEAD_DIM, (h + 1) * HEAD_DIM)
        q = q_ref[:, sl]
        k = k_ref[:, sl]
        v = v_ref[:, sl]
        a_c = a_col_all[:, _G_LF + h:_G_LF + h + 1]
        ig_c = gcol[:, _G_IG + h:_G_IG + h + 1]
        a_r = a_row_all[h:h + 1, :]
        ig_r = grow[_G_IG + h:_G_IG + h + 1, :]
        m_prev = m_ref[h][:, 0:1]
        c_prev = c_ref[h]
        n_prev = n_ref[h]
        d = jnp.where(causal, a_c - a_r + ig_r, NEG_INF)
        b = a_c + m_prev
        m_t = jnp.maximum(b, jnp.max(d, axis=1, keepdims=True))
        w_intra = jnp.exp(d - m_t)
        w_inter = jnp.exp(b - m_t)
        qb = q.astype(BF16)
        scores = _mm_nt(qb, k.astype(BF16)) * w_intra
        num = w_inter * _mm(qb, c_prev.astype(BF16)) + _mm(scores.astype(BF16), v.astype(BF16))
        den = w_inter * jnp.sum(q * n_prev, axis=1, keepdims=True) + jnp.sum(scores, axis=1, keepdims=True)
        hh = num / jnp.maximum(jnp.abs(den), jnp.exp(-m_t))
        m_new = m_t[chunk - 1:chunk, :]
        a_last = a_c[chunk - 1:chunk, :]
        decay = jnp.exp(a_last + m_prev - m_new)
        w_write = jnp.exp(a_last - a_c + ig_c - m_new)
        kw = k * w_write
        c_ref[h] = decay * c_prev + _mm_tn(kw.astype(BF16), v.astype(BF16))
        n_ref[h] = decay * n_prev + jnp.sum(kw, axis=0, keepdims=True)
        m_ref[h] = jnp.broadcast_to(m_new, (1, LANES))
        y_ref[:, sl] = _gated_head_norm(hh, o_ref[:, sl], gml_ref[:, sl])


def _mlstm_prompt(mq, mk, mv, mo, gcol, grow, g_ml):
    s = mq.shape[0]
    chunk = int(np.gcd(s, MLSTM_CHUNK))
    rows = pl.BlockSpec((chunk, WIDTH), lambda i: (i, 0))
    state = lambda shape: pl.BlockSpec(shape, lambda i: (0,) * len(shape))
    return pl.pallas_call(
        functools.partial(_mlstm_prompt_kernel, chunk=chunk),
        grid=(s // chunk,),
        in_specs=[rows, rows, rows, rows, pl.BlockSpec((chunk, LANES), lambda i: (i, 0)),
                  pl.BlockSpec((32, chunk), lambda i: (0, i)), state((1, WIDTH))],
        out_specs=(rows, state((N_HEADS, HEAD_DIM, HEAD_DIM)), state((N_HEADS, 1, HEAD_DIM)),
                   state((N_HEADS, 1, LANES))),
        out_shape=(jax.ShapeDtypeStruct((s, WIDTH), F32),
                   jax.ShapeDtypeStruct((N_HEADS, HEAD_DIM, HEAD_DIM), F32),
                   jax.ShapeDtypeStruct((N_HEADS, 1, HEAD_DIM), F32),
                   jax.ShapeDtypeStruct((N_HEADS, 1, LANES), F32)),
        compiler_params=_params(("arbitrary",)),
        name="mlstm_prompt",
    )(mq, mk, mv, mo, gcol, grow, g_ml.reshape(1, WIDTH))


def _mlstm_sample_kernel(q_ref, k_ref, v_ref, o_ref, ig_ref, lf_ref, m_ref, c_ref, n_ref, gml_ref,
                         y_ref, cn_ref, nn_ref, mn_ref):
    r = lax.broadcasted_iota(jnp.int32, (HEAD_DIM, HEAD_DIM), 0)
    c = lax.broadcasted_iota(jnp.int32, (HEAD_DIM, HEAD_DIM), 1)
    eye = r == c

    def column(row):
        return jnp.sum(jnp.where(eye, jnp.broadcast_to(row, (HEAD_DIM, HEAD_DIM)), 0.0), axis=1, keepdims=True)

    for h in range(N_HEADS):
        q = q_ref[0, h:h + 1, :]
        k = k_ref[0, h:h + 1, :]
        v = v_ref[0, h:h + 1, :]
        ig = ig_ref[0, h:h + 1, 0:1]
        lf = lf_ref[0, h:h + 1, 0:1]
        m_prev = m_ref[0, h:h + 1, 0:1]
        c_prev = c_ref[0, h]
        n_prev = n_ref[0, h:h + 1, :]
        b = lf + m_prev
        m_t = jnp.maximum(b, ig)
        w_intra = jnp.exp(ig - m_t)
        w_inter = jnp.exp(b - m_t)
        scores = jnp.sum(q * k, axis=1, keepdims=True) * w_intra
        qc = jnp.sum(column(q) * c_prev, axis=0, keepdims=True)
        num = w_inter * qc + scores * v
        den = w_inter * jnp.sum(q * n_prev, axis=1, keepdims=True) + scores
        hh = num / jnp.maximum(jnp.abs(den), jnp.exp(-m_t))
        cn_ref[0, h] = w_inter * c_prev + w_intra * (column(k) * v)
        nn_ref[0, h:h + 1, :] = w_inter * n_prev + w_intra * k
        mn_ref[0, h:h + 1, :] = jnp.broadcast_to(m_t, (1, LANES))
        y_ref[0, h:h + 1, :] = _gated_head_norm(hh, o_ref[0, h:h + 1, :], gml_ref[h:h + 1, :])


def _mlstm_sample(mq, mk, mv, mo, ig, lf, state_c, state_n, state_m, g_ml):
    nb = mq.shape[0]
    heads = lambda a: a.reshape(nb, N_HEADS, HEAD_DIM)
    lanes = lambda a: jnp.broadcast_to(a[:, :, None], (nb, N_HEADS, LANES))
    vec = pl.BlockSpec((1, N_HEADS, HEAD_DIM), lambda b: (b, 0, 0))
    sca = pl.BlockSpec((1, N_HEADS, LANES), lambda b: (b, 0, 0))
    mat = pl.BlockSpec((1, N_HEADS, HEAD_DIM, HEAD_DIM), lambda b: (b, 0, 0, 0))
    y, cn, nn, mn = pl.pallas_call(
        _mlstm_sample_kernel,
        grid=(nb,),
        in_specs=[vec, vec, vec, vec, sca, sca, sca, mat, vec,
                  pl.BlockSpec((N_HEADS, HEAD_DIM), lambda b: (0, 0))],
        out_specs=(vec, mat, vec, sca),
        out_shape=(jax.ShapeDtypeStruct((nb, N_HEADS, HEAD_DIM), F32),
                   jax.ShapeDtypeStruct((nb, N_HEADS, HEAD_DIM, HEAD_DIM), F32),
                   jax.ShapeDtypeStruct((nb, N_HEADS, HEAD_DIM), F32),
                   jax.ShapeDtypeStruct((nb, N_HEADS, LANES), F32)),
        compiler_params=_params(("arbitrary",)),
        name="mlstm_sample",
    )(heads(mq), heads(mk), heads(mv), heads(mo), lanes(ig), lanes(lf), lanes(state_m), state_c, state_n,
      g_ml.reshape(N_HEADS, HEAD_DIM))
    return y.reshape(nb, WIDTH), cn, nn, mn[:, :, 0]


_R_EXPERT, _R_GROUP = 0, N_EXPERTS


def _outproj_router_kernel(yf_ref, yml_ref, x_ref, wf_ref, wm_ref, gfox_ref, gffn_ref, wr_ref, br_ref,
                           x1_ref, xn_ref, comb_ref, *, prompt):
    yf = yf_ref[...]
    if prompt:
        ms = jnp.mean(yf * yf, axis=0, keepdims=True)
        yfn = (yf * lax.rsqrt(ms + RMS_EPS) * gfox_ref[...]).astype(BF16)
        y = _mm_tn(yfn, wf_ref[...]) + _mm(yml_ref[...].astype(BF16), wm_ref[...])
    else:
        y = _mm_f32(_rms(yf, gfox_ref[...]), wf_ref[...]) + _mm_f32(yml_ref[...], wm_ref[...])
    x1 = x_ref[...] + y
    x1_ref[...] = x1
    xn = _rms(x1, gffn_ref[...])
    xb = xn.astype(BF16)
    xn_ref[...] = xb

    router = _mm(xb, wr_ref[...]) if prompt else _mm_f32(xn, wr_ref[...])
    logits = router + br_ref[...]
    lane = lax.broadcasted_iota(jnp.int32, logits.shape, 1)
    big = jnp.int32(2 * LANES)

    def first_argmax(vals):
        top = jnp.max(vals, axis=1, keepdims=True)
        idx = jnp.min(jnp.where(vals == top, lane, big), axis=1, keepdims=True)
        return top, idx

    is_group = (lane >= _R_GROUP) & (lane < _R_GROUP + N_GROUPS)
    lg = jnp.where(is_group, logits, NEG_INF)
    lg_top, lg_idx = first_argmax(lg)
    gate_g = 1.0 / jnp.sum(jnp.exp(lg - lg_top), axis=1, keepdims=True)
    grp = lg_idx - _R_GROUP
    in_grp = (lane >= grp * EXPERTS_PER_GROUP) & (lane < (grp + 1) * EXPERTS_PER_GROUP)
    le = jnp.where(in_grp, logits, NEG_INF)
    top1, idx1 = first_argmax(le)
    top2, idx2 = first_argmax(jnp.where(lane == idx1, NEG_INF, le))
    e2 = jnp.exp(top2 - top1)
    w1 = gate_g / (1.0 + e2)
    w2 = gate_g * e2 / (1.0 + e2)
    comb_ref[...] = jnp.where(lane == idx1, w1, 0.0) + jnp.where(lane == idx2, w2, 0.0)


def _outproj_router(yf, yml, x, w_out, g_fox, g_ffn, w_rg, b_rg, w_re, b_re, tm, prompt):
    n, d = x.shape
    wdt = BF16 if prompt else F32
    wf = w_out[0:WIDTH].astype(wdt)
    wm = w_out[WIDTH:2 * WIDTH].astype(wdt)
    pad = LANES - N_EXPERTS - N_GROUPS
    wr = jnp.concatenate([w_re, w_rg, jnp.zeros((d, pad), F32)], axis=1).astype(wdt)
    br = jnp.concatenate([b_re, b_rg, jnp.zeros((pad,), F32)]).reshape(1, LANES)
    full = lambda shape: pl.BlockSpec(shape, lambda i: (0,) * len(shape))
    rows = lambda width: pl.BlockSpec((tm, width), lambda i: (i, 0))
    if prompt:
        yf_spec = pl.BlockSpec((WIDTH, tm), lambda i: (0, i))
        gfox = g_fox.reshape(WIDTH, 1)
    else:
        yf_spec = rows(WIDTH)
        gfox = g_fox.reshape(1, WIDTH)
    return pl.pallas_call(
        functools.partial(_outproj_router_kernel, prompt=prompt),
        grid=(n // tm,),
        in_specs=[yf_spec, rows(WIDTH), rows(d), full((WIDTH, d)), full((WIDTH, d)), full(gfox.shape),
                  full((1, d)), full((d, LANES)), full((1, LANES))],
        out_specs=(rows(d), rows(d), rows(LANES)),
        out_shape=(jax.ShapeDtypeStruct((n, d), F32), jax.ShapeDtypeStruct((n, d), BF16),
                   jax.ShapeDtypeStruct((n, LANES), F32)),
        compiler_params=_params(("arbitrary",)),
        name="outproj_router_prompt" if prompt else "outproj_router_sample",
    )(yf, yml, x, wf, wm, gfox, g_ffn.reshape(1, d), wr, br)


def _moe_kernel(xn_ref, comb_ref, x1_ref, wg_ref, wu_ref, wd_ref, gfin_ref, y_ref, acc_s):
    e = pl.program_id(1)

    @pl.when(e == 0)
    def _():
        acc_s[...] = jnp.zeros(acc_s.shape, F32)

    xb = xn_ref[...]
    gate = _mm(xb, wg_ref[0])
    he = gate * jax.nn.sigmoid(gate) * _mm(xb, wu_ref[0])
    out = _mm(he.astype(BF16), wd_ref[0])
    comb = comb_ref[...]
    lane = lax.broadcasted_iota(jnp.int32, comb.shape, 1)
    w_e = jnp.sum(jnp.where(lane == e, comb, 0.0), axis=1, keepdims=True)
    acc_s[...] += w_e * out

    @pl.when(e == pl.num_programs(1) - 1)
    def _():
        y_ref[...] = _rms(x1_ref[...] + acc_s[...], gfin_ref[...])


def _moe(xn, comb, x1, wg, wu, wd, g_final, tm):
    n, d = x1.shape
    de = wg.shape[2]
    rows = lambda width: pl.BlockSpec((tm, width), lambda i, e: (i, 0))
    return pl.pallas_call(
        _moe_kernel,
        grid=(n // tm, N_EXPERTS),
        in_specs=[rows(d), rows(LANES), rows(d),
                  pl.BlockSpec((1, d, de), lambda i, e: (e, 0, 0)),
                  pl.BlockSpec((1, d, de), lambda i, e: (e, 0, 0)),
                  pl.BlockSpec((1, de, d), lambda i, e: (e, 0, 0)),
                  pl.BlockSpec((1, d), lambda i, e: (0, 0))],
        out_specs=rows(d),
        out_shape=jax.ShapeDtypeStruct((n, d), F32),
        scratch_shapes=[pltpu.VMEM((tm, d), F32)],
        compiler_params=_params(("arbitrary", "arbitrary")),
        name="moe",
    )(xn, comb, x1, wg, wu, wd, g_final.reshape(1, d))


def kernel(x_prompt, x_sample, cache_k, cache_v, cache_logf, state_conv, state_C, state_n, state_m,
           page_table, g_mix, w_in, b_fox_f, b_ml_i, b_ml_f, w_conv, b_conv, g_fox_out, g_ml_out,
           w_out, g_ffn, w_router_group, b_router_group, w_router_expert, b_router_expert,
           w_exp_gate, w_exp_up, w_exp_down, g_final):
    depth = w_in.shape[0]
    batch, seq, d = x_prompt.shape
    nb, dec_seq, _ = x_sample.shape
    assert depth == 1 and batch == 1 and dec_seq == 1
    l = 0
    gate_bias = _gate_bias(b_fox_f[l], b_ml_i[l], b_ml_f[l])
    wg, wu, wd = (w[l].astype(BF16) for w in (w_exp_gate, w_exp_up, w_exp_down))
    router = (w_router_group[l], b_router_group[l], w_router_expert[l], b_router_expert[l])

    t_attn = min(512, seq)
    (qa, ka, k_p, v_p, vt, gcol, grow, mq, mk, mv, mo, tail) = _inproj_prompt(
        x_prompt[0], g_mix[l], w_in[l], gate_bias, w_conv[l], b_conv[l], tm=t_attn)
    y_fox_t = _fox_prompt(qa, ka, vt, t_attn)
    y_ml, c_p, n_p, m_p = _mlstm_prompt(mq, mk, mv, mo, gcol, grow, g_ml_out[l])
    x1, xn, comb = _outproj_router(y_fox_t, y_ml, x_prompt[0], w_out[l], g_fox_out[l], g_ffn[l], *router,
                                   tm=min(512, seq), prompt=True)
    y_prompt = _moe(xn, comb, x1, wg, wu, wd, g_final, tm=min(1024, seq))

    xs = x_sample[:, 0, :]
    (q_s, k_s, v_s, g_s, mq_s, mk_s, mv_s, mo_s, conv_new) = _inproj_sample(
        xs, g_mix[l], w_in[l], gate_bias, w_conv[l], b_conv[l], jnp.transpose(state_conv[l], (1, 0, 2)))
    logf_s = g_s[:, _G_LOGF:_G_LOGF + N_HEADS]
    y_fox_s = _fox_sample(q_s, k_s, v_s, logf_s, cache_k[l:l + 1], cache_v[l:l + 1], cache_logf[l:l + 1],
                          page_table)
    y_ml_s, c_s, n_s, m_s = _mlstm_sample(mq_s, mk_s, mv_s, mo_s, g_s[:, _G_IG:_G_IG + N_HEADS],
                                          g_s[:, _G_LF:_G_LF + N_HEADS], state_C[l], state_n[l], state_m[l],
                                          g_ml_out[l])
    x1_s, xn_s, comb_s = _outproj_router(y_fox_s, y_ml_s, xs, w_out[l], g_fox_out[l], g_ffn[l], *router,
                                         tm=nb, prompt=False)
    y_sample = _moe(xn_s, comb_s, x1_s, wg, wu, wd, g_final, tm=nb)

    heads = lambda a, n: a.reshape(1, n, -1, N_HEADS, HEAD_DIM)
    return (
        y_prompt[None], y_sample[:, None, :],
        heads(k_p, 1), heads(v_p, 1), gcol[:, _G_LOGF:_G_LOGF + N_HEADS].reshape(1, 1, seq, N_HEADS),
        tail[SUBLANES - (CONV_WIDTH - 1):][None, None],
        c_p[None, None], n_p[:, 0, :][None, None], m_p[:, 0, 0][None, None],
        heads(k_s, nb), heads(v_s, nb), logf_s.reshape(1, nb, 1, N_HEADS),
        jnp.transpose(conv_new, (1, 0, 2))[None],
        c_s[None], n_s[None], m_s[None],
    )
```

```python
import functools

import numpy as np
import jax
import jax.numpy as jnp
from jax import lax
from jax.experimental import pallas as pl
from jax.experimental.pallas import tpu as pltpu

HEAD_DIM = 64
N_HEADS = 8
WIDTH = N_HEADS * HEAD_DIM
CONV_WIDTH = 4
MLSTM_CHUNK = 128
N_GROUPS = 4
EXPERTS_PER_GROUP = 8
N_EXPERTS = N_GROUPS * EXPERTS_PER_GROUP
RMS_EPS = 1e-6
NEG_INF = -1e30
QK_SCALE = HEAD_DIM ** -0.5
LOG2E = 1.4426950408889634

LANES = 128
SUBLANES = 8
VMEM_LIMIT = 56 * 1024 * 1024

F32 = jnp.float32
BF16 = jnp.bfloat16


def _mm(a, b):
    return jnp.dot(a, b, preferred_element_type=F32)


def _mm_f32(a, b):
    return jnp.dot(a, b, preferred_element_type=F32, precision=lax.Precision.HIGHEST)


def _mm_nt(a, b):
    return lax.dot_general(a, b, (((1,), (1,)), ((), ())), preferred_element_type=F32)


def _mm_tn(a, b):
    return lax.dot_general(a, b, (((0,), (0,)), ((), ())), preferred_element_type=F32)


def _split3(a):
    hi = a.astype(BF16)
    r = a - hi.astype(F32)
    mid = r.astype(BF16)
    lo = (r - mid.astype(F32)).astype(BF16)
    return hi, mid, lo


def _mm3_right(a, b01):
    hi, mid, lo = _split3(a)
    return _mm(hi, b01) + _mm(mid, b01) + _mm(lo, b01)


def _mm3_left(a01, b):
    hi, mid, lo = _split3(b)
    return _mm(a01, hi) + _mm(a01, mid) + _mm(a01, lo)


def _log_sigmoid(x):
    return jnp.minimum(x, 0.0) - jnp.log1p(jnp.exp(-jnp.abs(x)))


def _rms(x, g):
    return x * lax.rsqrt(jnp.mean(x * x, axis=-1, keepdims=True) + RMS_EPS) * g


def _params(sem):
    return pltpu.CompilerParams(dimension_semantics=sem, vmem_limit_bytes=VMEM_LIMIT)


_C_Q, _C_K, _C_KRAW, _C_V, _C_MQK, _C_MV, _C_MO, _C_G, _C_END = (
    0, 1024, 2048, 2560, 3072, 4096, 4608, 5120, 5248)
_G_LOGF, _G_IG, _G_LF, _G_CUM = 0, 8, 16, 24
_AUG = HEAD_DIM


def _gate_tile(z, lane_axis):
    idx = lax.broadcasted_iota(jnp.int32, z.shape, lane_axis)
    is_ig = (idx >= _G_IG) & (idx < _G_LF)
    return jnp.where(is_ig, z, _log_sigmoid(z))


def _inproj_prompt_kernel(x_ref, gmix_ref, w_ref, wt_ref, brow_ref, bcol_ref, wconv_ref, bconv_ref,
                          p2q_ref, p2k_ref,
                          qa_ref, ka_ref, k_ref, v_ref, vt_ref, gcol_ref, grow_ref,
                          mq_ref, mk_ref, mv_ref, mo_ref, tail_ref,
                          conv_s, ccol_s, crow_s, *, tm):
    i = pl.program_id(0)

    @pl.when(i == 0)
    def _():
        conv_s[0:SUBLANES, :] = jnp.zeros((SUBLANES, conv_s.shape[1]), F32)
        ccol_s[...] = jnp.zeros(ccol_s.shape, F32)
        crow_s[...] = jnp.zeros(crow_s.shape, F32)

    xb = _rms(x_ref[...], gmix_ref[...]).astype(BF16)

    lane = lax.broadcasted_iota(jnp.int32, (tm, LANES), 1)
    g = _gate_tile(_mm(xb, w_ref[:, _C_G:_C_END]) + brow_ref[...], 1)
    zt = _mm_nt(wt_ref[...], xb)
    for h in range(N_HEADS):
        vt_ref[h, 0, 0:HEAD_DIM, :] = zt[h * HEAD_DIM:(h + 1) * HEAD_DIM].astype(BF16)
        vt_ref[h, 0, HEAD_DIM:_VT_ROWS, :] = jnp.ones((_VT_ROWS - HEAD_DIM, tm), BF16)
    gt = _gate_tile(zt[WIDTH:WIDTH + 32] + bcol_ref[...], 0)

    r = lax.broadcasted_iota(jnp.int32, (tm, tm), 0)
    c = lax.broadcasted_iota(jnp.int32, (tm, tm), 1)
    ltri = (c <= r).astype(BF16)
    utri = (r <= c).astype(BF16)
    cs = _mm3_left(ltri, g) + ccol_s[...]
    ccol_s[...] = cs[tm - 1:tm, :]
    cst = _mm3_right(gt[0:8], utri) + crow_s[...][:, 0:1]
    crow_s[...] = jnp.broadcast_to(cst[:, tm - 1:tm], crow_s.shape)
    in_cum = (lane >= _G_CUM) & (lane < _G_CUM + 8)
    gcol_ref[...] = jnp.where(in_cum, pltpu.roll(cs, _G_CUM, 1), g)
    grow_ref[...] = jnp.concatenate([gt[0:24], cst], axis=0)

    hi, mid, lo = _split3(cs * LOG2E)
    caug = jnp.where(lane < 8, hi.astype(F32),
                     jnp.where(lane < 16, pltpu.roll(mid.astype(F32), 8, 1),
                               jnp.where(lane < 24, pltpu.roll(lo.astype(F32), 16, 1),
                                         jnp.where(lane == 24, 1.0, 0.0)))).astype(BF16)
    zq = _mm(xb, w_ref[:, _C_Q:_C_K]) * (QK_SCALE * LOG2E) + _mm(caug, p2q_ref[...])
    zk = _mm(xb, w_ref[:, _C_K:_C_KRAW]) + _mm(caug, p2k_ref[...])
    for h in range(N_HEADS):
        qa_ref[h] = zq[:, h * LANES:(h + 1) * LANES].astype(BF16)
        ka_ref[h] = zk[:, h * LANES:(h + 1) * LANES].astype(BF16)

    k_ref[...] = _mm(xb, w_ref[:, _C_KRAW:_C_V])
    v_ref[...] = _mm(xb, w_ref[:, _C_V:_C_MQK])
    mv_ref[...] = _mm(xb, w_ref[:, _C_MV:_C_MO])
    mo_ref[...] = _mm(xb, w_ref[:, _C_MO:_C_G])

    u = _mm(xb, w_ref[:, _C_MQK:_C_MV])
    conv_s[SUBLANES:SUBLANES + tm, :] = u
    acc = bconv_ref[...] + u * wconv_ref[CONV_WIDTH - 1:CONV_WIDTH, :]
    for j in range(CONV_WIDTH - 1):
        back = CONV_WIDTH - 1 - j
        acc = acc + conv_s[SUBLANES - back:SUBLANES - back + tm, :] * wconv_ref[j:j + 1, :]
    qk = acc * jax.nn.sigmoid(acc)
    mq_ref[...] = qk[:, 0:WIDTH]
    mk_ref[...] = qk[:, WIDTH:2 * WIDTH] * QK_SCALE
    tail = conv_s[tm:tm + SUBLANES, :]
    tail_ref[...] = tail
    conv_s[0:SUBLANES, :] = tail


def _aug_placement():
    p2q = np.zeros((LANES, N_HEADS * LANES), np.float32)
    p2k = np.zeros((LANES, N_HEADS * LANES), np.float32)
    for h in range(N_HEADS):
        base = h * LANES + _AUG
        for part in range(3):
            p2q[part * 8 + h, base + part] = 1.0
            p2q[24, base + 3 + part] = 1.0
            p2k[24, base + part] = 1.0
            p2k[part * 8 + h, base + 3 + part] = -1.0
    return jnp.asarray(p2q, BF16), jnp.asarray(p2k, BF16)


def _split_w_in(w):
    d = w.shape[0]
    sizes = [WIDTH, WIDTH, WIDTH, N_HEADS, 2 * WIDTH, WIDTH, N_HEADS, N_HEADS, WIDTH]
    pts = np.cumsum([0] + sizes)
    parts = [w[:, pts[j]:pts[j + 1]] for j in range(len(sizes))]
    fq, fk, fv, ff, mqk, mv, mi, mf, mo = parts
    gates = jnp.concatenate([ff, mi, mf, jnp.zeros((d, LANES - 3 * N_HEADS), w.dtype)], axis=1)
    return fq, fk, fv, mqk, mv, mo, gates


def _gate_bias(b_fox_f, b_ml_i, b_ml_f):
    return jnp.concatenate([b_fox_f, b_ml_i, b_ml_f, jnp.zeros((LANES - 3 * N_HEADS,), F32)])


def _pad_heads(w):
    d = w.shape[0]
    w3 = w.reshape(d, N_HEADS, HEAD_DIM)
    return jnp.pad(w3, ((0, 0), (0, 0), (0, LANES - HEAD_DIM))).reshape(d, N_HEADS * LANES)


def _inproj_prompt(x, g_mix, w_in, gate_bias, w_conv, b_conv, tm=256):
    s, d = x.shape
    fq, fk, fv, mqk, mv, mo, gates = _split_w_in(w_in)
    w_all = jnp.concatenate([_pad_heads(fq), _pad_heads(fk), fk, fv, mqk, mv, mo, gates], axis=1).astype(BF16)
    w_t = jnp.concatenate([fv.T, gates[:, 0:32].T], axis=0).astype(BF16)
    p2q, p2k = _aug_placement()
    nblk = s // tm
    full = lambda shape: pl.BlockSpec(shape, lambda i: (0,) * len(shape))
    rows = lambda width: pl.BlockSpec((tm, width), lambda i: (i, 0))
    out_shape = (
        jax.ShapeDtypeStruct((N_HEADS, s, LANES), BF16),
        jax.ShapeDtypeStruct((N_HEADS, s, LANES), BF16),
        jax.ShapeDtypeStruct((s, WIDTH), F32),
        jax.ShapeDtypeStruct((s, WIDTH), F32),
        jax.ShapeDtypeStruct((N_HEADS, nblk, _VT_ROWS, tm), BF16),
        jax.ShapeDtypeStruct((s, LANES), F32),
        jax.ShapeDtypeStruct((32, s), F32),
        jax.ShapeDtypeStruct((s, WIDTH), F32),
        jax.ShapeDtypeStruct((s, WIDTH), F32),
        jax.ShapeDtypeStruct((s, WIDTH), F32),
        jax.ShapeDtypeStruct((s, WIDTH), F32),
        jax.ShapeDtypeStruct((SUBLANES, 2 * WIDTH), F32),
    )
    out_specs = (
        pl.BlockSpec((N_HEADS, tm, LANES), lambda i: (0, i, 0)),
        pl.BlockSpec((N_HEADS, tm, LANES), lambda i: (0, i, 0)),
        rows(WIDTH), rows(WIDTH),
        pl.BlockSpec((N_HEADS, 1, _VT_ROWS, tm), lambda i: (0, i, 0, 0)),
        rows(LANES),
        pl.BlockSpec((32, tm), lambda i: (0, i)),
        rows(WIDTH), rows(WIDTH), rows(WIDTH), rows(WIDTH),
        full((SUBLANES, 2 * WIDTH)),
    )
    return pl.pallas_call(
        functools.partial(_inproj_prompt_kernel, tm=tm),
        grid=(nblk,),
        in_specs=[rows(d), full((1, d)), full(w_all.shape), full(w_t.shape), full((1, LANES)),
                  full((32, 1)), full((CONV_WIDTH, 2 * WIDTH)), full((1, 2 * WIDTH)),
                  full(p2q.shape), full(p2k.shape)],
        out_specs=out_specs,
        out_shape=out_shape,
        scratch_shapes=[pltpu.VMEM((tm + 2 * SUBLANES, 2 * WIDTH), F32),
                        pltpu.VMEM((1, LANES), F32),
                        pltpu.VMEM((SUBLANES, LANES), F32)],
        compiler_params=_params(("arbitrary",)),
        name="inproj_prompt",
    )(x, g_mix.reshape(1, d), w_all, w_t, gate_bias.reshape(1, LANES), gate_bias[0:32].reshape(32, 1),
      w_conv, b_conv.reshape(1, 2 * WIDTH), p2q, p2k)


_S_Q, _S_K, _S_V, _S_MQK, _S_MV, _S_MO, _S_G, _S_END = 0, 512, 1024, 1536, 2560, 3072, 3584, 3712


def _inproj_sample_kernel(x_ref, gmix_ref, w_ref, brow_ref, wconv_ref, bconv_ref, cstate_ref,
                          q_ref, k_ref, v_ref, g_ref, mq_ref, mk_ref, mv_ref, mo_ref, cnew_ref):
    xn = _rms(x_ref[...], gmix_ref[...])
    q_ref[...] = _mm_f32(xn, w_ref[:, _S_Q:_S_K])
    k_ref[...] = _mm_f32(xn, w_ref[:, _S_K:_S_V])
    v_ref[...] = _mm_f32(xn, w_ref[:, _S_V:_S_MQK])
    mv_ref[...] = _mm_f32(xn, w_ref[:, _S_MV:_S_MO])
    mo_ref[...] = _mm_f32(xn, w_ref[:, _S_MO:_S_G])
    g_ref[...] = _gate_tile(_mm_f32(xn, w_ref[:, _S_G:_S_END]) + brow_ref[...], 1)
    u = _mm_f32(xn, w_ref[:, _S_MQK:_S_MV])
    acc = bconv_ref[...] + u * wconv_ref[CONV_WIDTH - 1:CONV_WIDTH, :]
    for j in range(CONV_WIDTH - 1):
        acc = acc + cstate_ref[j] * wconv_ref[j:j + 1, :]
    qk = acc * jax.nn.sigmoid(acc)
    mq_ref[...] = qk[:, 0:WIDTH]
    mk_ref[...] = qk[:, WIDTH:2 * WIDTH] * QK_SCALE
    for j in range(CONV_WIDTH - 2):
        cnew_ref[j] = cstate_ref[j + 1]
    cnew_ref[CONV_WIDTH - 2] = u


def _inproj_sample(x, g_mix, w_in, gate_bias, w_conv, b_conv, conv_state_t):
    n, d = x.shape
    fq, fk, fv, mqk, mv, mo, gates = _split_w_in(w_in)
    w_all = jnp.concatenate([fq, fk, fv, mqk, mv, mo, gates], axis=1)
    wide = jax.ShapeDtypeStruct((n, WIDTH), F32)
    out_shape = (wide, wide, wide, jax.ShapeDtypeStruct((n, LANES), F32), wide, wide, wide, wide,
                 jax.ShapeDtypeStruct((CONV_WIDTH - 1, n, 2 * WIDTH), F32))
    return pl.pallas_call(
        _inproj_sample_kernel,
        out_shape=out_shape,
        compiler_params=pltpu.CompilerParams(vmem_limit_bytes=VMEM_LIMIT),
        name="inproj_sample",
    )(x, g_mix.reshape(1, d), w_all, gate_bias.reshape(1, LANES), w_conv, b_conv.reshape(1, 2 * WIDTH),
      conv_state_t)


_ATTN_SUB = 256
_ATTN_CHUNKS = 4
_VT_ROWS = HEAD_DIM + 16


def _fox_prompt_kernel(qa_ref, ka_ref, vt_ref, o_ref, *, tq, tk):
    qi = pl.program_id(1)
    nsub = tq // _ATTN_SUB
    per_q = tq // tk
    qs = [qa_ref[0, c * _ATTN_SUB:(c + 1) * _ATTN_SUB, :] for c in range(nsub)]

    def chunks(rows):
        step = max(rows // _ATTN_CHUNKS, SUBLANES * 2)
        return [(r0, min(r0 + step, rows)) for r0 in range(0, rows, step)]

    def col_max(s, span):
        return jnp.max(s[span[0]:span[1]], axis=0, keepdims=True)

    def finish(state, m_new, p, vblk):
        m, l, acc = state
        alpha = jnp.exp2(m - m_new)
        pv = _mm(vblk, p)
        return m_new, alpha * l + pv[HEAD_DIM:HEAD_DIM + 1], alpha * acc + pv[0:HEAD_DIM]

    def body(j, carry):
        kblk = ka_ref[0, pl.ds(pl.multiple_of(j * tk, tk), tk), :]
        vblk = vt_ref[0, j]
        spans = chunks(tk)
        states = list(carry)
        s_cur = _mm_nt(kblk, qs[0])
        maxes = [col_max(s_cur, sp) for sp in spans]
        for c in range(nsub):
            m_new = functools.reduce(jnp.maximum, maxes, states[c][0])
            last = c + 1 == nsub
            if not last:
                s_next = _mm_nt(kblk, qs[c + 1])
            p_parts, maxes = [], []
            for sp in spans:
                p_parts.append(jnp.exp2(s_cur[sp[0]:sp[1]] - m_new).astype(BF16))
                if not last:
                    maxes.append(col_max(s_next, sp))
            states[c] = finish(states[c], m_new, jnp.concatenate(p_parts, axis=0), vblk)
            if not last:
                s_cur = s_next
        return tuple(states)

    init = tuple((jnp.full((1, _ATTN_SUB), NEG_INF, F32), jnp.zeros((1, _ATTN_SUB), F32),
                  jnp.zeros((HEAD_DIM, _ATTN_SUB), F32)) for _ in range(nsub))
    carry = lax.fori_loop(0, qi * per_q, body, init)

    for c in range(nsub):
        q_lo, q_hi = c * _ATTN_SUB, (c + 1) * _ATTN_SUB
        state = carry[c]
        for d in range(per_q):
            k_lo = d * tk
            rows = min(q_hi - k_lo, tk)
            if rows <= 0:
                continue
            kblk = ka_ref[0, pl.ds(pl.multiple_of(qi * tq + k_lo, tk), rows), :]
            s = _mm_nt(kblk, qs[c])
            if k_lo + rows > q_lo:
                kpos = lax.broadcasted_iota(jnp.int32, (rows, _ATTN_SUB), 0) + k_lo
                qpos = lax.broadcasted_iota(jnp.int32, (rows, _ATTN_SUB), 1) + q_lo
                s = jnp.where(kpos <= qpos, s, NEG_INF)
            m_new = jnp.maximum(state[0], jnp.max(s, axis=0, keepdims=True))
            p = jnp.exp2(s - m_new).astype(BF16)
            state = finish(state, m_new, p, vt_ref[0, qi * per_q + d][:, 0:rows])
        m, l, acc = state
        o_ref[:, q_lo:q_hi] = acc / l


def _fox_prompt(qa, ka, vt, tq):
    _, s, _ = qa.shape
    _, nkv, _, tk = vt.shape
    assert tq % tk == 0 and tq % _ATTN_SUB == 0 and tk % _ATTN_SUB == 0
    return pl.pallas_call(
        functools.partial(_fox_prompt_kernel, tq=tq, tk=tk),
        grid=(N_HEADS, s // tq),
        in_specs=[pl.BlockSpec((1, tq, LANES), lambda h, i: (h, i, 0)),
                  pl.BlockSpec((1, s, LANES), lambda h, i: (h, 0, 0)),
                  pl.BlockSpec((1, nkv, _VT_ROWS, tk), lambda h, i: (h, 0, 0, 0))],
        out_specs=pl.BlockSpec((HEAD_DIM, tq), lambda h, i: (h, i)),
        out_shape=jax.ShapeDtypeStruct((WIDTH, s), F32),
        compiler_params=_params(("arbitrary", "arbitrary")),
        name="fox_prompt",
    )(qa, ka, vt)


_PAGES_PER_STEP = 8


def _fox_sample_kernel(pt_ref, qb_ref, vnb_ref, q_ref, kn_ref, lfn_ref, tsuf_ref, *rest, page, npp):
    del pt_ref
    k_refs = rest[0:npp]
    v_refs = rest[npp:2 * npp]
    lf_refs = rest[2 * npp:3 * npp]
    o_ref = rest[3 * npp]
    qs_s, m_s, l_s, acc_s, carry_s = rest[3 * npp + 1:]
    g = pl.program_id(1)

    @pl.when(g == 0)
    def _():
        qs_s[...] = qb_ref[0] * QK_SCALE
        s_self = jnp.sum(q_ref[0] * QK_SCALE * kn_ref[0], axis=1, keepdims=True)
        m_s[...] = jnp.broadcast_to(s_self, m_s.shape)
        l_s[...] = jnp.ones(l_s.shape, F32)
        lane = lax.broadcasted_iota(jnp.int32, acc_s.shape, 2)
        acc_s[...] = jnp.where(lane == 0, vnb_ref[0], 0.0)
        carry_s[...] = lfn_ref[0]

    lf_all = jnp.concatenate([lf_refs[r][...] for r in range(npp)], axis=0)
    suffix = _mm3_right(lf_all, tsuf_ref[...])
    page_sum = jnp.sum(lf_all, axis=1, keepdims=True)
    carry = carry_s[...][:, 0:1]
    scores = []
    for r in range(npp):
        rows = slice(r * N_HEADS, (r + 1) * N_HEADS)
        qk = jnp.concatenate(
            [jnp.sum(k_refs[r][h] * qs_s[h], axis=0, keepdims=True) for h in range(N_HEADS)], axis=0)
        scores.append(qk + suffix[rows] + carry)
        carry = carry + page_sum[rows]
    carry_s[...] = jnp.broadcast_to(carry, carry_s.shape)

    m = m_s[...][:, 0:1]
    m_new = m
    for s in scores:
        m_new = jnp.maximum(m_new, jnp.max(s, axis=1, keepdims=True))
    alpha = jnp.exp(m - m_new)
    probs = [jnp.exp(s - m_new) for s in scores]
    l_new = alpha * l_s[...][:, 0:1]
    for p in probs:
        l_new = l_new + jnp.sum(p, axis=1, keepdims=True)
    l_s[...] = jnp.broadcast_to(l_new, l_s.shape)
    m_s[...] = jnp.broadcast_to(m_new, m_s.shape)
    for h in range(N_HEADS):
        upd = alpha[h:h + 1, :] * acc_s[h]
        for r in range(npp):
            upd = upd + probs[r][h:h + 1, :] * v_refs[r][h]
        acc_s[h] = upd

    @pl.when(g == pl.num_programs(1) - 1)
    def _():
        rr = lax.broadcasted_iota(jnp.int32, (HEAD_DIM, HEAD_DIM), 0)
        cc = lax.broadcasted_iota(jnp.int32, (HEAD_DIM, HEAD_DIM), 1)
        for h in range(N_HEADS):
            col = jnp.sum(acc_s[h], axis=1, keepdims=True) / l_s[...][h:h + 1, 0:1]
            row = jnp.sum(jnp.where(rr == cc, jnp.broadcast_to(col, (HEAD_DIM, HEAD_DIM)), 0.0),
                          axis=0, keepdims=True)
            o_ref[0, h:h + 1, :] = row


def _fox_sample(q, k_new, v_new, logf_new, cache_k, cache_v, cache_logf, page_table):
    nb, n_pages = page_table.shape
    page = cache_k.shape[2]
    npp = _PAGES_PER_STEP
    assert page == LANES and n_pages % npp == 0
    kt = jnp.transpose(cache_k, (0, 1, 3, 4, 2))
    vt = jnp.transpose(cache_v, (0, 1, 3, 4, 2))
    lft = jnp.transpose(cache_logf, (0, 1, 3, 2))
    q3, kn3, vn3 = (a.reshape(nb, N_HEADS, HEAD_DIM) for a in (q, k_new, v_new))
    lanes = lambda a: jnp.broadcast_to(a[..., None], a.shape + (LANES,))
    tsuf = jnp.asarray(np.tril(np.ones((page, page), np.float32), -1), BF16)

    def page_map(r):
        return lambda b, g, pt: (0, pt[b, n_pages - 1 - (g * npp + r)], 0, 0, 0)

    def lf_map(r):
        return lambda b, g, pt: (0, pt[b, n_pages - 1 - (g * npp + r)], 0, 0)

    per_b = pl.BlockSpec((1, N_HEADS, HEAD_DIM), lambda b, g, pt: (b, 0, 0))
    per_b_lanes = pl.BlockSpec((1, N_HEADS, HEAD_DIM, LANES), lambda b, g, pt: (b, 0, 0, 0))
    kv_specs = [pl.BlockSpec((None, None, N_HEADS, HEAD_DIM, page), page_map(r)) for r in range(npp)]
    lf_specs = [pl.BlockSpec((None, None, N_HEADS, page), lf_map(r)) for r in range(npp)]
    stat = pltpu.VMEM((N_HEADS, LANES), F32)
    grid_spec = pltpu.PrefetchScalarGridSpec(
        num_scalar_prefetch=1,
        grid=(nb, n_pages // npp),
        in_specs=[per_b_lanes, per_b_lanes, per_b, per_b,
                  pl.BlockSpec((1, N_HEADS, LANES), lambda b, g, pt: (b, 0, 0)),
                  pl.BlockSpec((page, page), lambda b, g, pt: (0, 0))] + kv_specs + kv_specs + lf_specs,
        out_specs=per_b,
        scratch_shapes=[pltpu.VMEM((N_HEADS, HEAD_DIM, LANES), F32), stat, stat,
                        pltpu.VMEM((N_HEADS, HEAD_DIM, page), F32), stat],
    )
    out = pl.pallas_call(
        functools.partial(_fox_sample_kernel, page=page, npp=npp),
        grid_spec=grid_spec,
        out_shape=jax.ShapeDtypeStruct((nb, N_HEADS, HEAD_DIM), F32),
        compiler_params=_params(("arbitrary", "arbitrary")),
        name="fox_sample",
    )(page_table, lanes(q3), lanes(vn3), q3, kn3, lanes(logf_new), tsuf,
      *([kt] * npp), *([vt] * npp), *([lft] * npp))
    return out.reshape(nb, WIDTH)


def _gated_head_norm(h, o_pre, g):
    hg = h * jax.nn.sigmoid(o_pre)
    return hg * lax.rsqrt(jnp.mean(hg * hg, axis=-1, keepdims=True) + RMS_EPS) * g


def _mlstm_prompt_kernel(q_ref, k_ref, v_ref, o_ref, gcol_ref, grow_ref, gml_ref,
                         y_ref, c_ref, n_ref, m_ref, *, chunk):
    @pl.when(pl.program_id(0) == 0)
    def _():
        c_ref[...] = jnp.zeros(c_ref.shape, F32)
        n_ref[...] = jnp.zeros(n_ref.shape, F32)
        m_ref[...] = jnp.zeros(m_ref.shape, F32)

    r = lax.broadcasted_iota(jnp.int32, (chunk, chunk), 0)
    c = lax.broadcasted_iota(jnp.int32, (chunk, chunk), 1)
    causal = c <= r
    gcol = gcol_ref[...]
    grow = grow_ref[...]
    a_col_all = _mm3_left(causal.astype(BF16), gcol)
    a_row_all = _mm3_right(grow[_G_LF:_G_LF + 8], (r <= c).astype(BF16))
    for h in range(N_HEADS):
        sl = slice(h * HEAD_DIM, (h + 1) * HEAD_DIM)
        q = q_ref[:, sl]
        k = k_ref[:, sl]
        v = v_ref[:, sl]
        a_c = a_col_all[:, _G_LF + h:_G_LF + h + 1]
        ig_c = gcol[:, _G_IG + h:_G_IG + h + 1]
        a_r = a_row_all[h:h + 1, :]
        ig_r = grow[_G_IG + h:_G_IG + h + 1, :]
        m_prev = m_ref[h][:, 0:1]
        c_prev = c_ref[h]
        n_prev = n_ref[h]
        d = jnp.where(causal, a_c - a_r + ig_r, NEG_INF)
        b = a_c + m_prev
        m_t = jnp.maximum(b, jnp.max(d, axis=1, keepdims=True))
        w_intra = jnp.exp(d - m_t)
        w_inter = jnp.exp(b - m_t)
        qb = q.astype(BF16)
        scores = _mm_nt(qb, k.astype(BF16)) * w_intra
        num = w_inter * _mm(qb, c_prev.astype(BF16)) + _mm(scores.astype(BF16), v.astype(BF16))
        den = w_inter * jnp.sum(q * n_prev, axis=1, keepdims=True) + jnp.sum(scores, axis=1, keepdims=True)
        hh = num / jnp.maximum(jnp.abs(den), jnp.exp(-m_t))
        m_new = m_t[chunk - 1:chunk, :]
        a_last = a_c[chunk - 1:chunk, :]
        decay = jnp.exp(a_last + m_prev - m_new)
        w_write = jnp.exp(a_last - a_c + ig_c - m_new)
        kw = k * w_write
        c_ref[h] = decay * c_prev + _mm_tn(kw.astype(BF16), v.astype(BF16))
        n_ref[h] = decay * n_prev + jnp.sum(kw, axis=0, keepdims=True)
        m_ref[h] = jnp.broadcast_to(m_new, (1, LANES))
        y_ref[:, sl] = _gated_head_norm(hh, o_ref[:, sl], gml_ref[:, sl])


def _mlstm_prompt(mq, mk, mv, mo, gcol, grow, g_ml):
    s = mq.shape[0]
    chunk = int(np.gcd(s, MLSTM_CHUNK))
    rows = pl.BlockSpec((chunk, WIDTH), lambda i: (i, 0))
    state = lambda shape: pl.BlockSpec(shape, lambda i: (0,) * len(shape))
    return pl.pallas_call(
        functools.partial(_mlstm_prompt_kernel, chunk=chunk),
        grid=(s // chunk,),
        in_specs=[rows, rows, rows, rows, pl.BlockSpec((chunk, LANES), lambda i: (i, 0)),
                  pl.BlockSpec((32, chunk), lambda i: (0, i)), state((1, WIDTH))],
        out_specs=(rows, state((N_HEADS, HEAD_DIM, HEAD_DIM)), state((N_HEADS, 1, HEAD_DIM)),
                   state((N_HEADS, 1, LANES))),
        out_shape=(jax.ShapeDtypeStruct((s, WIDTH), F32),
                   jax.ShapeDtypeStruct((N_HEADS, HEAD_DIM, HEAD_DIM), F32),
                   jax.ShapeDtypeStruct((N_HEADS, 1, HEAD_DIM), F32),
                   jax.ShapeDtypeStruct((N_HEADS, 1, LANES), F32)),
        compiler_params=_params(("arbitrary",)),
        name="mlstm_prompt",
    )(mq, mk, mv, mo, gcol, grow, g_ml.reshape(1, WIDTH))


def _mlstm_sample_kernel(q_ref, k_ref, v_ref, o_ref, ig_ref, lf_ref, m_ref, c_ref, n_ref, gml_ref,
                         y_ref, cn_ref, nn_ref, mn_ref):
    r = lax.broadcasted_iota(jnp.int32, (HEAD_DIM, HEAD_DIM), 0)
    c = lax.broadcasted_iota(jnp.int32, (HEAD_DIM, HEAD_DIM), 1)
    eye = r == c

    def column(row):
        return jnp.sum(jnp.where(eye, jnp.broadcast_to(row, (HEAD_DIM, HEAD_DIM)), 0.0), axis=1, keepdims=True)

    for h in range(N_HEADS):
        q = q_ref[0, h:h + 1, :]
        k = k_ref[0, h:h + 1, :]
        v = v_ref[0, h:h + 1, :]
        ig = ig_ref[0, h:h + 1, 0:1]
        lf = lf_ref[0, h:h + 1, 0:1]
        m_prev = m_ref[0, h:h + 1, 0:1]
        c_prev = c_ref[0, h]
        n_prev = n_ref[0, h:h + 1, :]
        b = lf + m_prev
        m_t = jnp.maximum(b, ig)
        w_intra = jnp.exp(ig - m_t)
        w_inter = jnp.exp(b - m_t)
        scores = jnp.sum(q * k, axis=1, keepdims=True) * w_intra
        qc = jnp.sum(column(q) * c_prev, axis=0, keepdims=True)
        num = w_inter * qc + scores * v
        den = w_inter * jnp.sum(q * n_prev, axis=1, keepdims=True) + scores
        hh = num / jnp.maximum(jnp.abs(den), jnp.exp(-m_t))
        cn_ref[0, h] = w_inter * c_prev + w_intra * (column(k) * v)
        nn_ref[0, h:h + 1, :] = w_inter * n_prev + w_intra * k
        mn_ref[0, h:h + 1, :] = jnp.broadcast_to(m_t, (1, LANES))
        y_ref[0, h:h + 1, :] = _gated_head_norm(hh, o_ref[0, h:h + 1, :], gml_ref[h:h + 1, :])


def _mlstm_sample(mq, mk, mv, mo, ig, lf, state_c, state_n, state_m, g_ml):
    nb = mq.shape[0]
    heads = lambda a: a.reshape(nb, N_HEADS, HEAD_DIM)
    lanes = lambda a: jnp.broadcast_to(a[:, :, None], (nb, N_HEADS, LANES))
    vec = pl.BlockSpec((1, N_HEADS, HEAD_DIM), lambda b: (b, 0, 0))
    sca = pl.BlockSpec((1, N_HEADS, LANES), lambda b: (b, 0, 0))
    mat = pl.BlockSpec((1, N_HEADS, HEAD_DIM, HEAD_DIM), lambda b: (b, 0, 0, 0))
    y, cn, nn, mn = pl.pallas_call(
        _mlstm_sample_kernel,
        grid=(nb,),
        in_specs=[vec, vec, vec, vec, sca, sca, sca, mat, vec,
                  pl.BlockSpec((N_HEADS, HEAD_DIM), lambda b: (0, 0))],
        out_specs=(vec, mat, vec, sca),
        out_shape=(jax.ShapeDtypeStruct((nb, N_HEADS, HEAD_DIM), F32),
                   jax.ShapeDtypeStruct((nb, N_HEADS, HEAD_DIM, HEAD_DIM), F32),
                   jax.ShapeDtypeStruct((nb, N_HEADS, HEAD_DIM), F32),
                   jax.ShapeDtypeStruct((nb, N_HEADS, LANES), F32)),
        compiler_params=_params(("arbitrary",)),
        name="mlstm_sample",
    )(heads(mq), heads(mk), heads(mv), heads(mo), lanes(ig), lanes(lf), lanes(state_m), state_c, state_n,
      g_ml.reshape(N_HEADS, HEAD_DIM))
    return y.reshape(nb, WIDTH), cn, nn, mn[:, :, 0]


_R_EXPERT, _R_GROUP = 0, N_EXPERTS


def _outproj_router_kernel(yf_ref, yml_ref, x_ref, wf_ref, wm_ref, gfox_ref, gffn_ref, wr_ref, br_ref,
                           x1_ref, xn_ref, comb_ref, *, prompt):
    yf = yf_ref[...]
    if prompt:
        ms = jnp.mean(yf * yf, axis=0, keepdims=True)
        yfn = (yf * lax.rsqrt(ms + RMS_EPS) * gfox_ref[...]).astype(BF16)
        y = _mm_tn(yfn, wf_ref[...]) + _mm(yml_ref[...].astype(BF16), wm_ref[...])
    else:
        y = _mm_f32(_rms(yf, gfox_ref[...]), wf_ref[...]) + _mm_f32(yml_ref[...], wm_ref[...])
    x1 = x_ref[...] + y
    x1_ref[...] = x1
    xn = _rms(x1, gffn_ref[...])
    xb = xn.astype(BF16)
    xn_ref[...] = xb

    router = _mm(xb, wr_ref[...]) if prompt else _mm_f32(xn, wr_ref[...])
    logits = router + br_ref[...]
    lane = lax.broadcasted_iota(jnp.int32, logits.shape, 1)
    big = jnp.int32(2 * LANES)

    def first_argmax(vals):
        top = jnp.max(vals, axis=1, keepdims=True)
        idx = jnp.min(jnp.where(vals == top, lane, big), axis=1, keepdims=True)
        return top, idx

    is_group = (lane >= _R_GROUP) & (lane < _R_GROUP + N_GROUPS)
    lg = jnp.where(is_group, logits, NEG_INF)
    lg_top, lg_idx = first_argmax(lg)
    gate_g = 1.0 / jnp.sum(jnp.exp(lg - lg_top), axis=1, keepdims=True)
    grp = lg_idx - _R_GROUP
    in_grp = (lane >= grp * EXPERTS_PER_GROUP) & (lane < (grp + 1) * EXPERTS_PER_GROUP)
    le = jnp.where(in_grp, logits, NEG_INF)
    top1, idx1 = first_argmax(le)
    top2, idx2 = first_argmax(jnp.where(lane == idx1, NEG_INF, le))
    e2 = jnp.exp(top2 - top1)
    w1 = gate_g / (1.0 + e2)
    w2 = gate_g * e2 / (1.0 + e2)
    comb_ref[...] = jnp.where(lane == idx1, w1, 0.0) + jnp.where(lane == idx2, w2, 0.0)


def _outproj_router(yf, yml, x, w_out, g_fox, g_ffn, w_rg, b_rg, w_re, b_re, tm, prompt):
    n, d = x.shape
    wdt = BF16 if prompt else F32
    wf = w_out[0:WIDTH].astype(wdt)
    wm = w_out[WIDTH:2 * WIDTH].astype(wdt)
    pad = LANES - N_EXPERTS - N_GROUPS
    wr = jnp.concatenate([w_re, w_rg, jnp.zeros((d, pad), F32)], axis=1).astype(wdt)
    br = jnp.concatenate([b_re, b_rg, jnp.zeros((pad,), F32)]).reshape(1, LANES)
    full = lambda shape: pl.BlockSpec(shape, lambda i: (0,) * len(shape))
    rows = lambda width: pl.BlockSpec((tm, width), lambda i: (i, 0))
    if prompt:
        yf_spec = pl.BlockSpec((WIDTH, tm), lambda i: (0, i))
        gfox = g_fox.reshape(WIDTH, 1)
    else:
        yf_spec = rows(WIDTH)
        gfox = g_fox.reshape(1, WIDTH)
    return pl.pallas_call(
        functools.partial(_outproj_router_kernel, prompt=prompt),
        grid=(n // tm,),
        in_specs=[yf_spec, rows(WIDTH), rows(d), full((WIDTH, d)), full((WIDTH, d)), full(gfox.shape),
                  full((1, d)), full((d, LANES)), full((1, LANES))],
        out_specs=(rows(d), rows(d), rows(LANES)),
        out_shape=(jax.ShapeDtypeStruct((n, d), F32), jax.ShapeDtypeStruct((n, d), BF16),
                   jax.ShapeDtypeStruct((n, LANES), F32)),
        compiler_params=_params(("arbitrary",)),
        name="outproj_router_prompt" if prompt else "outproj_router_sample",
    )(yf, yml, x, wf, wm, gfox, g_ffn.reshape(1, d), wr, br)


def _moe_kernel(xn_ref, comb_ref, x1_ref, wg_ref, wu_ref, wd_ref, gfin_ref, y_ref, acc_s):
    e = pl.program_id(1)

    @pl.when(e == 0)
    def _():
        acc_s[...] = jnp.zeros(acc_s.shape, F32)

    xb = xn_ref[...]
    gate = _mm(xb, wg_ref[0])
    he = gate * jax.nn.sigmoid(gate) * _mm(xb, wu_ref[0])
    out = _mm(he.astype(BF16), wd_ref[0])
    comb = comb_ref[...]
    lane = lax.broadcasted_iota(jnp.int32, comb.shape, 1)
    w_e = jnp.sum(jnp.where(lane == e, comb, 0.0), axis=1, keepdims=True)
    acc_s[...] += w_e * out

    @pl.when(e == pl.num_programs(1) - 1)
    def _():
        y_ref[...] = _rms(x1_ref[...] + acc_s[...], gfin_ref[...])


def _moe(xn, comb, x1, wg, wu, wd, g_final, tm):
    n, d = x1.shape
    de = wg.shape[2]
    rows = lambda width: pl.BlockSpec((tm, width), lambda i, e: (i, 0))
    return pl.pallas_call(
        _moe_kernel,
        grid=(n // tm, N_EXPERTS),
        in_specs=[rows(d), rows(LANES), rows(d),
                  pl.BlockSpec((1, d, de), lambda i, e: (e, 0, 0)),
                  pl.BlockSpec((1, d, de), lambda i, e: (e, 0, 0)),
                  pl.BlockSpec((1, de, d), lambda i, e: (e, 0, 0)),
                  pl.BlockSpec((1, d), lambda i, e: (0, 0))],
        out_specs=rows(d),
        out_shape=jax.ShapeDtypeStruct((n, d), F32),
        scratch_shapes=[pltpu.VMEM((tm, d), F32)],
        compiler_params=_params(("arbitrary", "arbitrary")),
        name="moe",
    )(xn, comb, x1, wg, wu, wd, g_final.reshape(1, d))


def kernel(x_prompt, x_sample, cache_k, cache_v, cache_logf, state_conv, state_C, state_n, state_m,
           page_table, g_mix, w_in, b_fox_f, b_ml_i, b_ml_f, w_conv, b_conv, g_fox_out, g_ml_out,
           w_out, g_ffn, w_router_group, b_router_group, w_router_expert, b_router_expert,
           w_exp_gate, w_exp_up, w_exp_down, g_final):
    depth = w_in.shape[0]
    batch, seq, d = x_prompt.shape
    nb, dec_seq, _ = x_sample.shape
    assert depth == 1 and batch == 1 and dec_seq == 1
    l = 0
    gate_bias = _gate_bias(b_fox_f[l], b_ml_i[l], b_ml_f[l])
    wg, wu, wd = (w[l].astype(BF16) for w in (w_exp_gate, w_exp_up, w_exp_down))
    router = (w_router_group[l], b_router_group[l], w_router_expert[l], b_router_expert[l])

    t_attn = min(512, seq)
    (qa, ka, k_p, v_p, vt, gcol, grow, mq, mk, mv, mo, tail) = _inproj_prompt(
        x_prompt[0], g_mix[l], w_in[l], gate_bias, w_conv[l], b_conv[l], tm=t_attn)
    y_fox_t = _fox_prompt(qa, ka, vt, tq=min(2048, seq))
    y_ml, c_p, n_p, m_p = _mlstm_prompt(mq, mk, mv, mo, gcol, grow, g_ml_out[l])
    x1, xn, comb = _outproj_router(y_fox_t, y_ml, x_prompt[0], w_out[l], g_fox_out[l], g_ffn[l], *router,
                                   tm=min(512, seq), prompt=True)
    y_prompt = _moe(xn, comb, x1, wg, wu, wd, g_final, tm=min(1024, seq))

    xs = x_sample[:, 0, :]
    (q_s, k_s, v_s, g_s, mq_s, mk_s, mv_s, mo_s, conv_new) = _inproj_sample(
        xs, g_mix[l], w_in[l], gate_bias, w_conv[l], b_conv[l], jnp.transpose(state_conv[l], (1, 0, 2)))
    logf_s = g_s[:, _G_LOGF:_G_LOGF + N_HEADS]
    y_fox_s = _fox_sample(q_s, k_s, v_s, logf_s, cache_k[l:l + 1], cache_v[l:l + 1], cache_logf[l:l + 1],
                          page_table)
    y_ml_s, c_s, n_s, m_s = _mlstm_sample(mq_s, mk_s, mv_s, mo_s, g_s[:, _G_IG:_G_IG + N_HEADS],
                                          g_s[:, _G_LF:_G_LF + N_HEADS], state_C[l], state_n[l], state_m[l],
                                          g_ml_out[l])
    x1_s, xn_s, comb_s = _outproj_router(y_fox_s, y_ml_s, xs, w_out[l], g_fox_out[l], g_ffn[l], *router,
                                         tm=nb, prompt=False)
    y_sample = _moe(xn_s, comb_s, x1_s, wg, wu, wd, g_final, tm=nb)

    heads = lambda a, n: a.reshape(1, n, -1, N_HEADS, HEAD_DIM)
    return (
        y_prompt[None], y_sample[:, None, :],
        heads(k_p, 1), heads(v_p, 1), gcol[:, _G_LOGF:_G_LOGF + N_HEADS].reshape(1, 1, seq, N_HEADS),
        tail[SUBLANES - (CONV_WIDTH - 1):][None, None],
        c_p[None, None], n_p[:, 0, :][None, None], m_p[:, 0, 0][None, None],
        heads(k_s, nb), heads(v_s, nb), logf_s.reshape(1, nb, 1, N_HEADS),
        jnp.transpose(conv_new, (1, 0, 2))[None],
        c_s[None], n_s[None], m_s[None],
    )
```

```python
import functools

import numpy as np
import jax
import jax.numpy as jnp
from jax import lax
from jax.experimental import pallas as pl
from jax.experimental.pallas import tpu as pltpu

HEAD_DIM = 64
N_HEADS = 8
WIDTH = N_HEADS * HEAD_DIM
CONV_WIDTH = 4
MLSTM_CHUNK = 128
N_GROUPS = 4
EXPERTS_PER_GROUP = 8
N_EXPERTS = N_GROUPS * EXPERTS_PER_GROUP
RMS_EPS = 1e-6
NEG_INF = -1e30
QK_SCALE = HEAD_DIM ** -0.5
LOG2E = 1.4426950408889634

LANES = 128
SUBLANES = 8
VMEM_LIMIT = 56 * 1024 * 1024

F32 = jnp.float32
BF16 = jnp.bfloat16


def _mm(a, b):
    return jnp.dot(a, b, preferred_element_type=F32)


def _mm_f32(a, b):
    return jnp.dot(a, b, preferred_element_type=F32, precision=lax.Precision.HIGHEST)


def _mm_nt(a, b):
    return lax.dot_general(a, b, (((1,), (1,)), ((), ())), preferred_element_type=F32)


def _mm_tn(a, b):
    return lax.dot_general(a, b, (((0,), (0,)), ((), ())), preferred_element_type=F32)


def _split3(a):
    hi = a.astype(BF16)
    r = a - hi.astype(F32)
    mid = r.astype(BF16)
    lo = (r - mid.astype(F32)).astype(BF16)
    return hi, mid, lo


def _mm3_right(a, b01):
    hi, mid, lo = _split3(a)
    return _mm(hi, b01) + _mm(mid, b01) + _mm(lo, b01)


def _mm3_left(a01, b):
    hi, mid, lo = _split3(b)
    return _mm(a01, hi) + _mm(a01, mid) + _mm(a01, lo)


def _log_sigmoid(x):
    return jnp.minimum(x, 0.0) - jnp.log1p(jnp.exp(-jnp.abs(x)))


def _rms(x, g):
    return x * lax.rsqrt(jnp.mean(x * x, axis=-1, keepdims=True) + RMS_EPS) * g


def _params(sem):
    return pltpu.CompilerParams(dimension_semantics=sem, vmem_limit_bytes=VMEM_LIMIT)


_C_Q, _C_K, _C_KRAW, _C_V, _C_MQK, _C_MV, _C_MO, _C_G, _C_END = (
    0, 1024, 2048, 2560, 3072, 4096, 4608, 5120, 5248)
_G_LOGF, _G_IG, _G_LF, _G_CUM = 0, 8, 16, 24
_AUG = HEAD_DIM


def _gate_tile(z, lane_axis):
    idx = lax.broadcasted_iota(jnp.int32, z.shape, lane_axis)
    is_ig = (idx >= _G_IG) & (idx < _G_LF)
    return jnp.where(is_ig, z, _log_sigmoid(z))


def _inproj_prompt_kernel(x_ref, gmix_ref, w_ref, wt_ref, brow_ref, bcol_ref, wconv_ref, bconv_ref,
                          p2q_ref, p2k_ref,
                          qa_ref, ka_ref, k_ref, v_ref, vt_ref, gcol_ref, grow_ref,
                          mq_ref, mk_ref, mv_ref, mo_ref, tail_ref,
                          conv_s, ccol_s, crow_s, *, tm):
    i = pl.program_id(0)

    @pl.when(i == 0)
    def _():
        conv_s[0:SUBLANES, :] = jnp.zeros((SUBLANES, conv_s.shape[1]), F32)
        ccol_s[...] = jnp.zeros(ccol_s.shape, F32)
        crow_s[...] = jnp.zeros(crow_s.shape, F32)

    xb = _rms(x_ref[...], gmix_ref[...]).astype(BF16)

    lane = lax.broadcasted_iota(jnp.int32, (tm, LANES), 1)
    g = _gate_tile(_mm(xb, w_ref[:, _C_G:_C_END]) + brow_ref[...], 1)
    zt = _mm_nt(wt_ref[...], xb)
    for h in range(N_HEADS):
        vt_ref[h, 0, 0:HEAD_DIM, :] = zt[h * HEAD_DIM:(h + 1) * HEAD_DIM].astype(BF16)
        vt_ref[h, 0, HEAD_DIM:_VT_ROWS, :] = jnp.ones((_VT_ROWS - HEAD_DIM, tm), BF16)
    gt = _gate_tile(zt[WIDTH:WIDTH + 32] + bcol_ref[...], 0)

    r = lax.broadcasted_iota(jnp.int32, (tm, tm), 0)
    c = lax.broadcasted_iota(jnp.int32, (tm, tm), 1)
    ltri = (c <= r).astype(BF16)
    utri = (r <= c).astype(BF16)
    cs = _mm3_left(ltri, g) + ccol_s[...]
    ccol_s[...] = cs[tm - 1:tm, :]
    cst = _mm3_right(gt[0:8], utri) + crow_s[...][:, 0:1]
    crow_s[...] = jnp.broadcast_to(cst[:, tm - 1:tm], crow_s.shape)
    in_cum = (lane >= _G_CUM) & (lane < _G_CUM + 8)
    gcol_ref[...] = jnp.where(in_cum, pltpu.roll(cs, _G_CUM, 1), g)
    grow_ref[...] = jnp.concatenate([gt[0:24], cst], axis=0)

    hi, mid, lo = _split3(cs * LOG2E)
    caug = jnp.where(lane < 8, hi.astype(F32),
                     jnp.where(lane < 16, pltpu.roll(mid.astype(F32), 8, 1),
                               jnp.where(lane < 24, pltpu.roll(lo.astype(F32), 16, 1),
                                         jnp.where(lane == 24, 1.0, 0.0)))).astype(BF16)
    zq = _mm(xb, w_ref[:, _C_Q:_C_K]) * (QK_SCALE * LOG2E) + _mm(caug, p2q_ref[...])
    zk = _mm(xb, w_ref[:, _C_K:_C_KRAW]) + _mm(caug, p2k_ref[...])
    for h in range(N_HEADS):
        qa_ref[h] = zq[:, h * LANES:(h + 1) * LANES].astype(BF16)
        ka_ref[h] = zk[:, h * LANES:(h + 1) * LANES].astype(BF16)

    k_ref[...] = _mm(xb, w_ref[:, _C_KRAW:_C_V])
    v_ref[...] = _mm(xb, w_ref[:, _C_V:_C_MQK])
    mv_ref[...] = _mm(xb, w_ref[:, _C_MV:_C_MO])
    mo_ref[...] = _mm(xb, w_ref[:, _C_MO:_C_G])

    u = _mm(xb, w_ref[:, _C_MQK:_C_MV])
    conv_s[SUBLANES:SUBLANES + tm, :] = u
    acc = bconv_ref[...] + u * wconv_ref[CONV_WIDTH - 1:CONV_WIDTH, :]
    for j in range(CONV_WIDTH - 1):
        back = CONV_WIDTH - 1 - j
        acc = acc + conv_s[SUBLANES - back:SUBLANES - back + tm, :] * wconv_ref[j:j + 1, :]
    qk = acc * jax.nn.sigmoid(acc)
    mq_ref[...] = qk[:, 0:WIDTH]
    mk_ref[...] = qk[:, WIDTH:2 * WIDTH] * QK_SCALE
    tail = conv_s[tm:tm + SUBLANES, :]
    tail_ref[...] = tail
    conv_s[0:SUBLANES, :] = tail


def _aug_placement():
    p2q = np.zeros((LANES, N_HEADS * LANES), np.float32)
    p2k = np.zeros((LANES, N_HEADS * LANES), np.float32)
    for h in range(N_HEADS):
        base = h * LANES + _AUG
        for part in range(3):
            p2q[part * 8 + h, base + part] = 1.0
            p2q[24, base + 3 + part] = 1.0
            p2k[24, base + part] = 1.0
            p2k[part * 8 + h, base + 3 + part] = -1.0
    return jnp.asarray(p2q, BF16), jnp.asarray(p2k, BF16)


def _split_w_in(w):
    d = w.shape[0]
    sizes = [WIDTH, WIDTH, WIDTH, N_HEADS, 2 * WIDTH, WIDTH, N_HEADS, N_HEADS, WIDTH]
    pts = np.cumsum([0] + sizes)
    parts = [w[:, pts[j]:pts[j + 1]] for j in range(len(sizes))]
    fq, fk, fv, ff, mqk, mv, mi, mf, mo = parts
    gates = jnp.concatenate([ff, mi, mf, jnp.zeros((d, LANES - 3 * N_HEADS), w.dtype)], axis=1)
    return fq, fk, fv, mqk, mv, mo, gates


def _gate_bias(b_fox_f, b_ml_i, b_ml_f):
    return jnp.concatenate([b_fox_f, b_ml_i, b_ml_f, jnp.zeros((LANES - 3 * N_HEADS,), F32)])


def _pad_heads(w):
    d = w.shape[0]
    w3 = w.reshape(d, N_HEADS, HEAD_DIM)
    return jnp.pad(w3, ((0, 0), (0, 0), (0, LANES - HEAD_DIM))).reshape(d, N_HEADS * LANES)


def _inproj_prompt(x, g_mix, w_in, gate_bias, w_conv, b_conv, tm=256):
    s, d = x.shape
    fq, fk, fv, mqk, mv, mo, gates = _split_w_in(w_in)
    w_all = jnp.concatenate([_pad_heads(fq), _pad_heads(fk), fk, fv, mqk, mv, mo, gates], axis=1).astype(BF16)
    w_t = jnp.concatenate([fv.T, gates[:, 0:32].T], axis=0).astype(BF16)
    p2q, p2k = _aug_placement()
    nblk = s // tm
    full = lambda shape: pl.BlockSpec(shape, lambda i: (0,) * len(shape))
    rows = lambda width: pl.BlockSpec((tm, width), lambda i: (i, 0))
    out_shape = (
        jax.ShapeDtypeStruct((N_HEADS, s, LANES), BF16),
        jax.ShapeDtypeStruct((N_HEADS, s, LANES), BF16),
        jax.ShapeDtypeStruct((s, WIDTH), F32),
        jax.ShapeDtypeStruct((s, WIDTH), F32),
        jax.ShapeDtypeStruct((N_HEADS, nblk, _VT_ROWS, tm), BF16),
        jax.ShapeDtypeStruct((s, LANES), F32),
        jax.ShapeDtypeStruct((32, s), F32),
        jax.ShapeDtypeStruct((s, WIDTH), F32),
        jax.ShapeDtypeStruct((s, WIDTH), F32),
        jax.ShapeDtypeStruct((s, WIDTH), F32),
        jax.ShapeDtypeStruct((s, WIDTH), F32),
        jax.ShapeDtypeStruct((SUBLANES, 2 * WIDTH), F32),
    )
    out_specs = (
        pl.BlockSpec((N_HEADS, tm, LANES), lambda i: (0, i, 0)),
        pl.BlockSpec((N_HEADS, tm, LANES), lambda i: (0, i, 0)),
        rows(WIDTH), rows(WIDTH),
        pl.BlockSpec((N_HEADS, 1, _VT_ROWS, tm), lambda i: (0, i, 0, 0)),
        rows(LANES),
        pl.BlockSpec((32, tm), lambda i: (0, i)),
        rows(WIDTH), rows(WIDTH), rows(WIDTH), rows(WIDTH),
        full((SUBLANES, 2 * WIDTH)),
    )
    return pl.pallas_call(
        functools.partial(_inproj_prompt_kernel, tm=tm),
        grid=(nblk,),
        in_specs=[rows(d), full((1, d)), full(w_all.shape), full(w_t.shape), full((1, LANES)),
                  full((32, 1)), full((CONV_WIDTH, 2 * WIDTH)), full((1, 2 * WIDTH)),
                  full(p2q.shape), full(p2k.shape)],
        out_specs=out_specs,
        out_shape=out_shape,
        scratch_shapes=[pltpu.VMEM((tm + 2 * SUBLANES, 2 * WIDTH), F32),
                        pltpu.VMEM((1, LANES), F32),
                        pltpu.VMEM((SUBLANES, LANES), F32)],
        compiler_params=_params(("arbitrary",)),
        name="inproj_prompt",
    )(x, g_mix.reshape(1, d), w_all, w_t, gate_bias.reshape(1, LANES), gate_bias[0:32].reshape(32, 1),
      w_conv, b_conv.reshape(1, 2 * WIDTH), p2q, p2k)


_S_Q, _S_K, _S_V, _S_MQK, _S_MV, _S_MO, _S_G, _S_END = 0, 512, 1024, 1536, 2560, 3072, 3584, 3712


def _inproj_sample_kernel(x_ref, gmix_ref, w_ref, brow_ref, wconv_ref, bconv_ref, cstate_ref,
                          q_ref, k_ref, v_ref, g_ref, mq_ref, mk_ref, mv_ref, mo_ref, cnew_ref):
    xn = _rms(x_ref[...], gmix_ref[...])
    q_ref[...] = _mm_f32(xn, w_ref[:, _S_Q:_S_K])
    k_ref[...] = _mm_f32(xn, w_ref[:, _S_K:_S_V])
    v_ref[...] = _mm_f32(xn, w_ref[:, _S_V:_S_MQK])
    mv_ref[...] = _mm_f32(xn, w_ref[:, _S_MV:_S_MO])
    mo_ref[...] = _mm_f32(xn, w_ref[:, _S_MO:_S_G])
    g_ref[...] = _gate_tile(_mm_f32(xn, w_ref[:, _S_G:_S_END]) + brow_ref[...], 1)
    u = _mm_f32(xn, w_ref[:, _S_MQK:_S_MV])
    acc = bconv_ref[...] + u * wconv_ref[CONV_WIDTH - 1:CONV_WIDTH, :]
    for j in range(CONV_WIDTH - 1):
        acc = acc + cstate_ref[j] * wconv_ref[j:j + 1, :]
    qk = acc * jax.nn.sigmoid(acc)
    mq_ref[...] = qk[:, 0:WIDTH]
    mk_ref[...] = qk[:, WIDTH:2 * WIDTH] * QK_SCALE
    for j in range(CONV_WIDTH - 2):
        cnew_ref[j] = cstate_ref[j + 1]
    cnew_ref[CONV_WIDTH - 2] = u


def _inproj_sample(x, g_mix, w_in, gate_bias, w_conv, b_conv, conv_state_t):
    n, d = x.shape
    fq, fk, fv, mqk, mv, mo, gates = _split_w_in(w_in)
    w_all = jnp.concatenate([fq, fk, fv, mqk, mv, mo, gates], axis=1)
    wide = jax.ShapeDtypeStruct((n, WIDTH), F32)
    out_shape = (wide, wide, wide, jax.ShapeDtypeStruct((n, LANES), F32), wide, wide, wide, wide,
                 jax.ShapeDtypeStruct((CONV_WIDTH - 1, n, 2 * WIDTH), F32))
    return pl.pallas_call(
        _inproj_sample_kernel,
        out_shape=out_shape,
        compiler_params=pltpu.CompilerParams(vmem_limit_bytes=VMEM_LIMIT),
        name="inproj_sample",
    )(x, g_mix.reshape(1, d), w_all, gate_bias.reshape(1, LANES), w_conv, b_conv.reshape(1, 2 * WIDTH),
      conv_state_t)


_ATTN_SUB = 256
_ATTN_AHEAD = 8
_ATTN_CHUNKS = 1
_VT_ROWS = HEAD_DIM + 16


def _fox_prompt_kernel(qa_ref, ka_ref, vt_ref, o_ref, *, tq, tk):
    qi = pl.program_id(1)
    nsub = tq // _ATTN_SUB
    per_q = tq // tk
    qs = [qa_ref[0, c * _ATTN_SUB:(c + 1) * _ATTN_SUB, :] for c in range(nsub)]

    def chunks(rows):
        step = max(rows // _ATTN_CHUNKS, SUBLANES * 2)
        return [(r0, min(r0 + step, rows)) for r0 in range(0, rows, step)]

    def col_max(s, span):
        return jnp.max(s[span[0]:span[1]], axis=0, keepdims=True)

    def finish(state, m_new, p, vblk):
        m, l, acc = state
        alpha = jnp.exp2(m - m_new)
        pv = _mm(vblk, p)
        return m_new, alpha * l + pv[HEAD_DIM:HEAD_DIM + 1], alpha * acc + pv[0:HEAD_DIM]

    def run_block(states, k_start, v_idx, plan):
        vblk = vt_ref[0, v_idx]

        def logits_of(c, rows, k_lo):
            s = _mm_nt(ka_ref[0, pl.ds(k_start, rows), :], qs[c])
            if k_lo is not None:
                kpos = lax.broadcasted_iota(jnp.int32, (rows, _ATTN_SUB), 0) + k_lo
                qpos = lax.broadcasted_iota(jnp.int32, (rows, _ATTN_SUB), 1) + c * _ATTN_SUB
                s = jnp.where(kpos <= qpos, s, NEG_INF)
            return s

        logits, maxes = {}, {}
        for i in range(min(_ATTN_AHEAD, len(plan))):
            c, rows, k_lo = plan[i]
            logits[i] = logits_of(c, rows, k_lo)
            maxes[i] = [col_max(logits[i], sp) for sp in chunks(rows)]
        for i, (c, rows, _) in enumerate(plan):
            m_new = functools.reduce(jnp.maximum, maxes.pop(i), states[c][0])
            nxt = i + _ATTN_AHEAD
            nxt_spans = []
            if nxt < len(plan):
                logits[nxt] = logits_of(*plan[nxt])
                maxes[nxt] = []
                nxt_spans = chunks(plan[nxt][1])
            s_cur = logits.pop(i)
            p_parts = []
            for n, sp in enumerate(chunks(rows)):
                p_parts.append(jnp.exp2(s_cur[sp[0]:sp[1]] - m_new).astype(BF16))
                if n < len(nxt_spans):
                    maxes[nxt].append(col_max(logits[nxt], nxt_spans[n]))
            for sp in nxt_spans[len(p_parts):]:
                maxes[nxt].append(col_max(logits[nxt], sp))
            states[c] = finish(states[c], m_new, jnp.concatenate(p_parts, axis=0), vblk[:, 0:rows])
        return states

    def body(j, carry):
        full = [(c, tk, None) for c in range(nsub)]
        return tuple(run_block(list(carry), pl.multiple_of(j * tk, tk), j, full))

    init = tuple((jnp.full((1, _ATTN_SUB), NEG_INF, F32), jnp.zeros((1, _ATTN_SUB), F32),
                  jnp.zeros((HEAD_DIM, _ATTN_SUB), F32)) for _ in range(nsub))
    states = list(lax.fori_loop(0, qi * per_q, body, init))

    for d in range(per_q):
        k_lo = d * tk
        plan = []
        for c in range(nsub):
            q_lo, q_hi = c * _ATTN_SUB, (c + 1) * _ATTN_SUB
            rows = min(q_hi - k_lo, tk)
            if rows > 0:
                plan.append((c, rows, k_lo if k_lo + rows > q_lo else None))
        states = run_block(states, pl.multiple_of(qi * tq + k_lo, tk), qi * per_q + d, plan)
    for c in range(nsub):
        m, l, acc = states[c]
        o_ref[:, c * _ATTN_SUB:(c + 1) * _ATTN_SUB] = acc / l


def _fox_prompt(qa, ka, vt, tq):
    _, s, _ = qa.shape
    _, nkv, _, tk = vt.shape
    assert tq % tk == 0 and tq % _ATTN_SUB == 0 and tk % _ATTN_SUB == 0
    return pl.pallas_call(
        functools.partial(_fox_prompt_kernel, tq=tq, tk=tk),
        grid=(N_HEADS, s // tq),
        in_specs=[pl.BlockSpec((1, tq, LANES), lambda h, i: (h, i, 0)),
                  pl.BlockSpec((1, s, LANES), lambda h, i: (h, 0, 0)),
                  pl.BlockSpec((1, nkv, _VT_ROWS, tk), lambda h, i: (h, 0, 0, 0))],
        out_specs=pl.BlockSpec((HEAD_DIM, tq), lambda h, i: (h, i)),
        out_shape=jax.ShapeDtypeStruct((WIDTH, s), F32),
        compiler_params=_params(("arbitrary", "arbitrary")),
        name="fox_prompt",
    )(qa, ka, vt)


_PAGES_PER_STEP = 8


def _fox_sample_kernel(pt_ref, qb_ref, vnb_ref, q_ref, kn_ref, lfn_ref, tsuf_ref, *rest, page, npp):
    del pt_ref
    k_refs = rest[0:npp]
    v_refs = rest[npp:2 * npp]
    lf_refs = rest[2 * npp:3 * npp]
    o_ref = rest[3 * npp]
    qs_s, m_s, l_s, acc_s, carry_s = rest[3 * npp + 1:]
    g = pl.program_id(1)

    @pl.when(g == 0)
    def _():
        qs_s[...] = qb_ref[0] * QK_SCALE
        s_self = jnp.sum(q_ref[0] * QK_SCALE * kn_ref[0], axis=1, keepdims=True)
        m_s[...] = jnp.broadcast_to(s_self, m_s.shape)
        l_s[...] = jnp.ones(l_s.shape, F32)
        lane = lax.broadcasted_iota(jnp.int32, acc_s.shape, 2)
        acc_s[...] = jnp.where(lane == 0, vnb_ref[0], 0.0)
        carry_s[...] = lfn_ref[0]

    lf_all = jnp.concatenate([lf_refs[r][...] for r in range(npp)], axis=0)
    suffix = _mm3_right(lf_all, tsuf_ref[...])
    page_sum = jnp.sum(lf_all, axis=1, keepdims=True)
    carry = carry_s[...][:, 0:1]
    scores = []
    for r in range(npp):
        rows = slice(r * N_HEADS, (r + 1) * N_HEADS)
        qk = jnp.concatenate(
            [jnp.sum(k_refs[r][h] * qs_s[h], axis=0, keepdims=True) for h in range(N_HEADS)], axis=0)
        scores.append(qk + suffix[rows] + carry)
        carry = carry + page_sum[rows]
    carry_s[...] = jnp.broadcast_to(carry, carry_s.shape)

    m = m_s[...][:, 0:1]
    m_new = m
    for s in scores:
        m_new = jnp.maximum(m_new, jnp.max(s, axis=1, keepdims=True))
    alpha = jnp.exp(m - m_new)
    probs = [jnp.exp(s - m_new) for s in scores]
    l_new = alpha * l_s[...][:, 0:1]
    for p in probs:
        l_new = l_new + jnp.sum(p, axis=1, keepdims=True)
    l_s[...] = jnp.broadcast_to(l_new, l_s.shape)
    m_s[...] = jnp.broadcast_to(m_new, m_s.shape)
    for h in range(N_HEADS):
        upd = alpha[h:h + 1, :] * acc_s[h]
        for r in range(npp):
            upd = upd + probs[r][h:h + 1, :] * v_refs[r][h]
        acc_s[h] = upd

    @pl.when(g == pl.num_programs(1) - 1)
    def _():
        rr = lax.broadcasted_iota(jnp.int32, (HEAD_DIM, HEAD_DIM), 0)
        cc = lax.broadcasted_iota(jnp.int32, (HEAD_DIM, HEAD_DIM), 1)
        for h in range(N_HEADS):
            col = jnp.sum(acc_s[h], axis=1, keepdims=True) / l_s[...][h:h + 1, 0:1]
            row = jnp.sum(jnp.where(rr == cc, jnp.broadcast_to(col, (HEAD_DIM, HEAD_DIM)), 0.0),
                          axis=0, keepdims=True)
            o_ref[0, h:h + 1, :] = row


def _fox_sample(q, k_new, v_new, logf_new, cache_k, cache_v, cache_logf, page_table):
    nb, n_pages = page_table.shape
    page = cache_k.shape[2]
    npp = _PAGES_PER_STEP
    assert page == LANES and n_pages % npp == 0
    kt = jnp.transpose(cache_k, (0, 1, 3, 4, 2))
    vt = jnp.transpose(cache_v, (0, 1, 3, 4, 2))
    lft = jnp.transpose(cache_logf, (0, 1, 3, 2))
    q3, kn3, vn3 = (a.reshape(nb, N_HEADS, HEAD_DIM) for a in (q, k_new, v_new))
    lanes = lambda a: jnp.broadcast_to(a[..., None], a.shape + (LANES,))
    tsuf = jnp.asarray(np.tril(np.ones((page, page), np.float32), -1), BF16)

    def page_map(r):
        return lambda b, g, pt: (0, pt[b, n_pages - 1 - (g * npp + r)], 0, 0, 0)

    def lf_map(r):
        return lambda b, g, pt: (0, pt[b, n_pages - 1 - (g * npp + r)], 0, 0)

    per_b = pl.BlockSpec((1, N_HEADS, HEAD_DIM), lambda b, g, pt: (b, 0, 0))
    per_b_lanes = pl.BlockSpec((1, N_HEADS, HEAD_DIM, LANES), lambda b, g, pt: (b, 0, 0, 0))
    kv_specs = [pl.BlockSpec((None, None, N_HEADS, HEAD_DIM, page), page_map(r)) for r in range(npp)]
    lf_specs = [pl.BlockSpec((None, None, N_HEADS, page), lf_map(r)) for r in range(npp)]
    stat = pltpu.VMEM((N_HEADS, LANES), F32)
    grid_spec = pltpu.PrefetchScalarGridSpec(
        num_scalar_prefetch=1,
        grid=(nb, n_pages // npp),
        in_specs=[per_b_lanes, per_b_lanes, per_b, per_b,
                  pl.BlockSpec((1, N_HEADS, LANES), lambda b, g, pt: (b, 0, 0)),
                  pl.BlockSpec((page, page), lambda b, g, pt: (0, 0))] + kv_specs + kv_specs + lf_specs,
        out_specs=per_b,
        scratch_shapes=[pltpu.VMEM((N_HEADS, HEAD_DIM, LANES), F32), stat, stat,
                        pltpu.VMEM((N_HEADS, HEAD_DIM, page), F32), stat],
    )
    out = pl.pallas_call(
        functools.partial(_fox_sample_kernel, page=page, npp=npp),
        grid_spec=grid_spec,
        out_shape=jax.ShapeDtypeStruct((nb, N_HEADS, HEAD_DIM), F32),
        compiler_params=_params(("arbitrary", "arbitrary")),
        name="fox_sample",
    )(page_table, lanes(q3), lanes(vn3), q3, kn3, lanes(logf_new), tsuf,
      *([kt] * npp), *([vt] * npp), *([lft] * npp))
    return out.reshape(nb, WIDTH)


def _gated_head_norm(h, o_pre, g):
    hg = h * jax.nn.sigmoid(o_pre)
    return hg * lax.rsqrt(jnp.mean(hg * hg, axis=-1, keepdims=True) + RMS_EPS) * g


def _mlstm_prompt_kernel(q_ref, k_ref, v_ref, o_ref, gcol_ref, grow_ref, gml_ref,
                         y_ref, c_ref, n_ref, m_ref, *, chunk):
    @pl.when(pl.program_id(0) == 0)
    def _():
        c_ref[...] = jnp.zeros(c_ref.shape, F32)
        n_ref[...] = jnp.zeros(n_ref.shape, F32)
        m_ref[...] = jnp.zeros(m_ref.shape, F32)

    r = lax.broadcasted_iota(jnp.int32, (chunk, chunk), 0)
    c = lax.broadcasted_iota(jnp.int32, (chunk, chunk), 1)
    causal = c <= r
    gcol = gcol_ref[...]
    grow = grow_ref[...]
    a_col_all = _mm3_left(causal.astype(BF16), gcol)
    a_row_all = _mm3_right(grow[_G_LF:_G_LF + 8], (r <= c).astype(BF16))
    for h in range(N_HEADS):
        sl = slice(h * HEAD_DIM, (h + 1) * HEAD_DIM)
        q = q_ref[:, sl]
        k = k_ref[:, sl]
        v = v_ref[:, sl]
        a_c = a_col_all[:, _G_LF + h:_G_LF + h + 1]
        ig_c = gcol[:, _G_IG + h:_G_IG + h + 1]
        a_r = a_row_all[h:h + 1, :]
        ig_r = grow[_G_IG + h:_G_IG + h + 1, :]
        m_prev = m_ref[h][:, 0:1]
        c_prev = c_ref[h]
        n_prev = n_ref[h]
        d = jnp.where(causal, a_c - a_r + ig_r, NEG_INF)
        b = a_c + m_prev
        m_t = jnp.maximum(b, jnp.max(d, axis=1, keepdims=True))
        w_intra = jnp.exp(d - m_t)
        w_inter = jnp.exp(b - m_t)
        qb = q.astype(BF16)
        scores = _mm_nt(qb, k.astype(BF16)) * w_intra
        num = w_inter * _mm(qb, c_prev.astype(BF16)) + _mm(scores.astype(BF16), v.astype(BF16))
        den = w_inter * jnp.sum(q * n_prev, axis=1, keepdims=True) + jnp.sum(scores, axis=1, keepdims=True)
        hh = num / jnp.maximum(jnp.abs(den), jnp.exp(-m_t))
        m_new = m_t[chunk - 1:chunk, :]
        a_last = a_c[chunk - 1:chunk, :]
        decay = jnp.exp(a_last + m_prev - m_new)
        w_write = jnp.exp(a_last - a_c + ig_c - m_new)
        kw = k * w_write
        c_ref[h] = decay * c_prev + _mm_tn(kw.astype(BF16), v.astype(BF16))
        n_ref[h] = decay * n_prev + jnp.sum(kw, axis=0, keepdims=True)
        m_ref[h] = jnp.broadcast_to(m_new, (1, LANES))
        y_ref[:, sl] = _gated_head_norm(hh, o_ref[:, sl], gml_ref[:, sl])


def _mlstm_prompt(mq, mk, mv, mo, gcol, grow, g_ml):
    s = mq.shape[0]
    chunk = int(np.gcd(s, MLSTM_CHUNK))
    rows = pl.BlockSpec((chunk, WIDTH), lambda i: (i, 0))
    state = lambda shape: pl.BlockSpec(shape, lambda i: (0,) * len(shape))
    return pl.pallas_call(
        functools.partial(_mlstm_prompt_kernel, chunk=chunk),
        grid=(s // chunk,),
        in_specs=[rows, rows, rows, rows, pl.BlockSpec((chunk, LANES), lambda i: (i, 0)),
                  pl.BlockSpec((32, chunk), lambda i: (0, i)), state((1, WIDTH))],
        out_specs=(rows, state((N_HEADS, HEAD_DIM, HEAD_DIM)), state((N_HEADS, 1, HEAD_DIM)),
                   state((N_HEADS, 1, LANES))),
        out_shape=(jax.ShapeDtypeStruct((s, WIDTH), F32),
                   jax.ShapeDtypeStruct((N_HEADS, HEAD_DIM, HEAD_DIM), F32),
                   jax.ShapeDtypeStruct((N_HEADS, 1, HEAD_DIM), F32),
                   jax.ShapeDtypeStruct((N_HEADS, 1, LANES), F32)),
        compiler_params=_params(("arbitrary",)),
        name="mlstm_prompt",
    )(mq, mk, mv, mo, gcol, grow, g_ml.reshape(1, WIDTH))


def _mlstm_sample_kernel(q_ref, k_ref, v_ref, o_ref, ig_ref, lf_ref, m_ref, c_ref, n_ref, gml_ref,
                         y_ref, cn_ref, nn_ref, mn_ref):
    r = lax.broadcasted_iota(jnp.int32, (HEAD_DIM, HEAD_DIM), 0)
    c = lax.broadcasted_iota(jnp.int32, (HEAD_DIM, HEAD_DIM), 1)
    eye = r == c

    def column(row):
        return jnp.sum(jnp.where(eye, jnp.broadcast_to(row, (HEAD_DIM, HEAD_DIM)), 0.0), axis=1, keepdims=True)

    for h in range(N_HEADS):
        q = q_ref[0, h:h + 1, :]
        k = k_ref[0, h:h + 1, :]
        v = v_ref[0, h:h + 1, :]
        ig = ig_ref[0, h:h + 1, 0:1]
        lf = lf_ref[0, h:h + 1, 0:1]
        m_prev = m_ref[0, h:h + 1, 0:1]
        c_prev = c_ref[0, h]
        n_prev = n_ref[0, h:h + 1, :]
        b = lf + m_prev
        m_t = jnp.maximum(b, ig)
        w_intra = jnp.exp(ig - m_t)
        w_inter = jnp.exp(b - m_t)
        scores = jnp.sum(q * k, axis=1, keepdims=True) * w_intra
        qc = jnp.sum(column(q) * c_prev, axis=0, keepdims=True)
        num = w_inter * qc + scores * v
        den = w_inter * jnp.sum(q * n_prev, axis=1, keepdims=True) + scores
        hh = num / jnp.maximum(jnp.abs(den), jnp.exp(-m_t))
        cn_ref[0, h] = w_inter * c_prev + w_intra * (column(k) * v)
        nn_ref[0, h:h + 1, :] = w_inter * n_prev + w_intra * k
        mn_ref[0, h:h + 1, :] = jnp.broadcast_to(m_t, (1, LANES))
        y_ref[0, h:h + 1, :] = _gated_head_norm(hh, o_ref[0, h:h + 1, :], gml_ref[h:h + 1, :])


def _mlstm_sample(mq, mk, mv, mo, ig, lf, state_c, state_n, state_m, g_ml):
    nb = mq.shape[0]
    heads = lambda a: a.reshape(nb, N_HEADS, HEAD_DIM)
    lanes = lambda a: jnp.broadcast_to(a[:, :, None], (nb, N_HEADS, LANES))
    vec = pl.BlockSpec((1, N_HEADS, HEAD_DIM), lambda b: (b, 0, 0))
    sca = pl.BlockSpec((1, N_HEADS, LANES), lambda b: (b, 0, 0))
    mat = pl.BlockSpec((1, N_HEADS, HEAD_DIM, HEAD_DIM), lambda b: (b, 0, 0, 0))
    y, cn, nn, mn = pl.pallas_call(
        _mlstm_sample_kernel,
        grid=(nb,),
        in_specs=[vec, vec, vec, vec, sca, sca, sca, mat, vec,
                  pl.BlockSpec((N_HEADS, HEAD_DIM), lambda b: (0, 0))],
        out_specs=(vec, mat, vec, sca),
        out_shape=(jax.ShapeDtypeStruct((nb, N_HEADS, HEAD_DIM), F32),
                   jax.ShapeDtypeStruct((nb, N_HEADS, HEAD_DIM, HEAD_DIM), F32),
                   jax.ShapeDtypeStruct((nb, N_HEADS, HEAD_DIM), F32),
                   jax.ShapeDtypeStruct((nb, N_HEADS, LANES), F32)),
        compiler_params=_params(("arbitrary",)),
        name="mlstm_sample",
    )(heads(mq), heads(mk), heads(mv), heads(mo), lanes(ig), lanes(lf), lanes(state_m), state_c, state_n,
      g_ml.reshape(N_HEADS, HEAD_DIM))
    return y.reshape(nb, WIDTH), cn, nn, mn[:, :, 0]


_R_EXPERT, _R_GROUP = 0, N_EXPERTS


def _outproj_router_kernel(yf_ref, yml_ref, x_ref, wf_ref, wm_ref, gfox_ref, gffn_ref, wr_ref, br_ref,
                           x1_ref, xn_ref, comb_ref, *, prompt):
    yf = yf_ref[...]
    if prompt:
        ms = jnp.mean(yf * yf, axis=0, keepdims=True)
        yfn = (yf * lax.rsqrt(ms + RMS_EPS) * gfox_ref[...]).astype(BF16)
        y = _mm_tn(yfn, wf_ref[...]) + _mm(yml_ref[...].astype(BF16), wm_ref[...])
    else:
        y = _mm_f32(_rms(yf, gfox_ref[...]), wf_ref[...]) + _mm_f32(yml_ref[...], wm_ref[...])
    x1 = x_ref[...] + y
    x1_ref[...] = x1
    xn = _rms(x1, gffn_ref[...])
    xb = xn.astype(BF16)
    xn_ref[...] = xb

    router = _mm(xb, wr_ref[...]) if prompt else _mm_f32(xn, wr_ref[...])
    logits = router + br_ref[...]
    lane = lax.broadcasted_iota(jnp.int32, logits.shape, 1)
    big = jnp.int32(2 * LANES)

    def first_argmax(vals):
        top = jnp.max(vals, axis=1, keepdims=True)
        idx = jnp.min(jnp.where(vals == top, lane, big), axis=1, keepdims=True)
        return top, idx

    is_group = (lane >= _R_GROUP) & (lane < _R_GROUP + N_GROUPS)
    lg = jnp.where(is_group, logits, NEG_INF)
    lg_top, lg_idx = first_argmax(lg)
    gate_g = 1.0 / jnp.sum(jnp.exp(lg - lg_top), axis=1, keepdims=True)
    grp = lg_idx - _R_GROUP
    in_grp = (lane >= grp * EXPERTS_PER_GROUP) & (lane < (grp + 1) * EXPERTS_PER_GROUP)
    le = jnp.where(in_grp, logits, NEG_INF)
    top1, idx1 = first_argmax(le)
    top2, idx2 = first_argmax(jnp.where(lane == idx1, NEG_INF, le))
    e2 = jnp.exp(top2 - top1)
    w1 = gate_g / (1.0 + e2)
    w2 = gate_g * e2 / (1.0 + e2)
    comb_ref[...] = jnp.where(lane == idx1, w1, 0.0) + jnp.where(lane == idx2, w2, 0.0)


def _outproj_router(yf, yml, x, w_out, g_fox, g_ffn, w_rg, b_rg, w_re, b_re, tm, prompt):
    n, d = x.shape
    wdt = BF16 if prompt else F32
    wf = w_out[0:WIDTH].astype(wdt)
    wm = w_out[WIDTH:2 * WIDTH].astype(wdt)
    pad = LANES - N_EXPERTS - N_GROUPS
    wr = jnp.concatenate([w_re, w_rg, jnp.zeros((d, pad), F32)], axis=1).astype(wdt)
    br = jnp.concatenate([b_re, b_rg, jnp.zeros((pad,), F32)]).reshape(1, LANES)
    full = lambda shape: pl.BlockSpec(shape, lambda i: (0,) * len(shape))
    rows = lambda width: pl.BlockSpec((tm, width), lambda i: (i, 0))
    if prompt:
        yf_spec = pl.BlockSpec((WIDTH, tm), lambda i: (0, i))
        gfox = g_fox.reshape(WIDTH, 1)
    else:
        yf_spec = rows(WIDTH)
        gfox = g_fox.reshape(1, WIDTH)
    return pl.pallas_call(
        functools.partial(_outproj_router_kernel, prompt=prompt),
        grid=(n // tm,),
        in_specs=[yf_spec, rows(WIDTH), rows(d), full((WIDTH, d)), full((WIDTH, d)), full(gfox.shape),
                  full((1, d)), full((d, LANES)), full((1, LANES))],
        out_specs=(rows(d), rows(d), rows(LANES)),
        out_shape=(jax.ShapeDtypeStruct((n, d), F32), jax.ShapeDtypeStruct((n, d), BF16),
                   jax.ShapeDtypeStruct((n, LANES), F32)),
        compiler_params=_params(("arbitrary",)),
        name="outproj_router_prompt" if prompt else "outproj_router_sample",
    )(yf, yml, x, wf, wm, gfox, g_ffn.reshape(1, d), wr, br)


def _moe_kernel(xn_ref, comb_ref, x1_ref, wg_ref, wu_ref, wd_ref, gfin_ref, y_ref, acc_s):
    e = pl.program_id(1)

    @pl.when(e == 0)
    def _():
        acc_s[...] = jnp.zeros(acc_s.shape, F32)

    xb = xn_ref[...]
    gate = _mm(xb, wg_ref[0])
    he = gate * jax.nn.sigmoid(gate) * _mm(xb, wu_ref[0])
    out = _mm(he.astype(BF16), wd_ref[0])
    comb = comb_ref[...]
    lane = lax.broadcasted_iota(jnp.int32, comb.shape, 1)
    w_e = jnp.sum(jnp.where(lane == e, comb, 0.0), axis=1, keepdims=True)
    acc_s[...] += w_e * out

    @pl.when(e == pl.num_programs(1) - 1)
    def _():
        y_ref[...] = _rms(x1_ref[...] + acc_s[...], gfin_ref[...])


def _moe(xn, comb, x1, wg, wu, wd, g_final, tm):
    n, d = x1.shape
    de = wg.shape[2]
    rows = lambda width: pl.BlockSpec((tm, width), lambda i, e: (i, 0))
    return pl.pallas_call(
        _moe_kernel,
        grid=(n // tm, N_EXPERTS),
        in_specs=[rows(d), rows(LANES), rows(d),
                  pl.BlockSpec((1, d, de), lambda i, e: (e, 0, 0)),
                  pl.BlockSpec((1, d, de), lambda i, e: (e, 0, 0)),
                  pl.BlockSpec((1, de, d), lambda i, e: (e, 0, 0)),
                  pl.BlockSpec((1, d), lambda i, e: (0, 0))],
        out_specs=rows(d),
        out_shape=jax.ShapeDtypeStruct((n, d), F32),
        scratch_shapes=[pltpu.VMEM((tm, d), F32)],
        compiler_params=_params(("arbitrary", "arbitrary")),
        name="moe",
    )(xn, comb, x1, wg, wu, wd, g_final.reshape(1, d))


def kernel(x_prompt, x_sample, cache_k, cache_v, cache_logf, state_conv, state_C, state_n, state_m,
           page_table, g_mix, w_in, b_fox_f, b_ml_i, b_ml_f, w_conv, b_conv, g_fox_out, g_ml_out,
           w_out, g_ffn, w_router_group, b_router_group, w_router_expert, b_router_expert,
           w_exp_gate, w_exp_up, w_exp_down, g_final):
    depth = w_in.shape[0]
    batch, seq, d = x_prompt.shape
    nb, dec_seq, _ = x_sample.shape
    assert depth == 1 and batch == 1 and dec_seq == 1
    l = 0
    gate_bias = _gate_bias(b_fox_f[l], b_ml_i[l], b_ml_f[l])
    wg, wu, wd = (w[l].astype(BF16) for w in (w_exp_gate, w_exp_up, w_exp_down))
    router = (w_router_group[l], b_router_group[l], w_router_expert[l], b_router_expert[l])

    t_attn = min(512, seq)
    (qa, ka, k_p, v_p, vt, gcol, grow, mq, mk, mv, mo, tail) = _inproj_prompt(
        x_prompt[0], g_mix[l], w_in[l], gate_bias, w_conv[l], b_conv[l], tm=t_attn)
    y_fox_t = _fox_prompt(qa, ka, vt, tq=min(2048, seq))
    y_ml, c_p, n_p, m_p = _mlstm_prompt(mq, mk, mv, mo, gcol, grow, g_ml_out[l])
    x1, xn, comb = _outproj_router(y_fox_t, y_ml, x_prompt[0], w_out[l], g_fox_out[l], g_ffn[l], *router,
                                   tm=min(512, seq), prompt=True)
    y_prompt = _moe(xn, comb, x1, wg, wu, wd, g_final, tm=min(1024, seq))

    xs = x_sample[:, 0, :]
    (q_s, k_s, v_s, g_s, mq_s, mk_s, mv_s, mo_s, conv_new) = _inproj_sample(
        xs, g_mix[l], w_in[l], gate_bias, w_conv[l], b_conv[l], jnp.transpose(state_conv[l], (1, 0, 2)))
    logf_s = g_s[:, _G_LOGF:_G_LOGF + N_HEADS]
    y_fox_s = _fox_sample(q_s, k_s, v_s, logf_s, cache_k[l:l + 1], cache_v[l:l + 1], cache_logf[l:l + 1],
                          page_table)
    y_ml_s, c_s, n_s, m_s = _mlstm_sample(mq_s, mk_s, mv_s, mo_s, g_s[:, _G_IG:_G_IG + N_HEADS],
                                          g_s[:, _G_LF:_G_LF + N_HEADS], state_C[l], state_n[l], state_m[l],
                                          g_ml_out[l])
    x1_s, xn_s, comb_s = _outproj_router(y_fox_s, y_ml_s, xs, w_out[l], g_fox_out[l], g_ffn[l], *router,
                                         tm=nb, prompt=False)
    y_sample = _moe(xn_s, comb_s, x1_s, wg, wu, wd, g_final, tm=nb)

    heads = lambda a, n: a.reshape(1, n, -1, N_HEADS, HEAD_DIM)
    return (
        y_prompt[None], y_sample[:, None, :],
        heads(k_p, 1), heads(v_p, 1), gcol[:, _G_LOGF:_G_LOGF + N_HEADS].reshape(1, 1, seq, N_HEADS),
        tail[SUBLANES - (CONV_WIDTH - 1):][None, None],
        c_p[None, None], n_p[:, 0, :][None, None], m_p[:, 0, 0][None, None],
        heads(k_s, nb), heads(v_s, nb), logf_s.reshape(1, nb, 1, N_HEADS),
        jnp.transpose(conv_new, (1, 0, 2))[None],
        c_s[None], n_s[None], m_s[None],
    )
```

```python
import functools

import numpy as np
import jax
import jax.numpy as jnp
from jax import lax
from jax.experimental import pallas as pl
from jax.experimental.pallas import tpu as pltpu

HEAD_DIM = 64
N_HEADS = 8
WIDTH = N_HEADS * HEAD_DIM
CONV_WIDTH = 4
MLSTM_CHUNK = 128
N_GROUPS = 4
EXPERTS_PER_GROUP = 8
N_EXPERTS = N_GROUPS * EXPERTS_PER_GROUP
RMS_EPS = 1e-6
NEG_INF = -1e30
QK_SCALE = HEAD_DIM ** -0.5
LOG2E = 1.4426950408889634

LANES = 128
SUBLANES = 8
VMEM_LIMIT = 56 * 1024 * 1024

F32 = jnp.float32
BF16 = jnp.bfloat16


def _mm(a, b):
    return jnp.dot(a, b, preferred_element_type=F32)


def _mm_f32(a, b):
    return jnp.dot(a, b, preferred_element_type=F32, precision=lax.Precision.HIGHEST)


def _mm_nt(a, b):
    return lax.dot_general(a, b, (((1,), (1,)), ((), ())), preferred_element_type=F32)


def _mm_tn(a, b):
    return lax.dot_general(a, b, (((0,), (0,)), ((), ())), preferred_element_type=F32)


def _split3(a):
    hi = a.astype(BF16)
    r = a - hi.astype(F32)
    mid = r.astype(BF16)
    lo = (r - mid.astype(F32)).astype(BF16)
    return hi, mid, lo


def _mm3_right(a, b01):
    hi, mid, lo = _split3(a)
    return _mm(hi, b01) + _mm(mid, b01) + _mm(lo, b01)


def _mm3_left(a01, b):
    hi, mid, lo = _split3(b)
    return _mm(a01, hi) + _mm(a01, mid) + _mm(a01, lo)


def _log_sigmoid(x):
    return jnp.minimum(x, 0.0) - jnp.log1p(jnp.exp(-jnp.abs(x)))


def _rms(x, g):
    return x * lax.rsqrt(jnp.mean(x * x, axis=-1, keepdims=True) + RMS_EPS) * g


def _params(sem):
    return pltpu.CompilerParams(dimension_semantics=sem, vmem_limit_bytes=VMEM_LIMIT)


_C_Q, _C_K, _C_KRAW, _C_V, _C_MQK, _C_G, _C_END = 0, 1024, 2048, 2560, 3072, 4096, 4224
_T_V, _T_G, _T_MV, _T_MO, _T_END = 0, 512, 544, 1056, 1568
_G_LOGF, _G_IG, _G_LF, _G_CUM = 0, 8, 16, 24
_AUG = HEAD_DIM


def _gate_tile(z, lane_axis):
    idx = lax.broadcasted_iota(jnp.int32, z.shape, lane_axis)
    is_ig = (idx >= _G_IG) & (idx < _G_LF)
    return jnp.where(is_ig, z, _log_sigmoid(z))


def _inproj_prompt_kernel(x_ref, gmix_ref, w_ref, wt_ref, brow_ref, bcol_ref, wconv_ref, bconv_ref,
                          p2q_ref, p2k_ref,
                          qa_ref, ka_ref, k_ref, v_ref, vt_ref, gcol_ref, grow_ref,
                          mq_ref, mk_ref, mvt_ref, mot_ref, tail_ref,
                          conv_s, ccol_s, crow_s, *, tm):
    i = pl.program_id(0)

    @pl.when(i == 0)
    def _():
        conv_s[0:SUBLANES, :] = jnp.zeros((SUBLANES, conv_s.shape[1]), F32)
        ccol_s[...] = jnp.zeros(ccol_s.shape, F32)
        crow_s[...] = jnp.zeros(crow_s.shape, F32)

    xb = _rms(x_ref[...], gmix_ref[...]).astype(BF16)

    lane = lax.broadcasted_iota(jnp.int32, (tm, LANES), 1)
    g = _gate_tile(_mm(xb, w_ref[:, _C_G:_C_END]) + brow_ref[...], 1)
    zt = _mm_nt(wt_ref[...], xb)
    for h in range(N_HEADS):
        vt_ref[h, 0, 0:HEAD_DIM, :] = zt[h * HEAD_DIM:(h + 1) * HEAD_DIM].astype(BF16)
        vt_ref[h, 0, HEAD_DIM:_VT_ROWS, :] = jnp.ones((_VT_ROWS - HEAD_DIM, tm), BF16)
    mvt_ref[...] = zt[_T_MV:_T_MO].astype(BF16)
    mot_ref[...] = zt[_T_MO:_T_END]
    gt = _gate_tile(zt[_T_G:_T_MV] + bcol_ref[...], 0)

    r = lax.broadcasted_iota(jnp.int32, (tm, tm), 0)
    c = lax.broadcasted_iota(jnp.int32, (tm, tm), 1)
    ltri = (c <= r).astype(BF16)
    utri = (r <= c).astype(BF16)
    cs = _mm3_left(ltri, g) + ccol_s[...]
    ccol_s[...] = cs[tm - 1:tm, :]
    cst = _mm3_right(gt[0:8], utri) + crow_s[...][:, 0:1]
    crow_s[...] = jnp.broadcast_to(cst[:, tm - 1:tm], crow_s.shape)
    in_cum = (lane >= _G_CUM) & (lane < _G_CUM + 8)
    gcol_ref[...] = jnp.where(in_cum, pltpu.roll(cs, _G_CUM, 1), g)
    grow_ref[...] = jnp.concatenate([gt[0:24], cst], axis=0)

    hi, mid, lo = _split3(cs * LOG2E)
    caug = jnp.where(lane < 8, hi.astype(F32),
                     jnp.where(lane < 16, pltpu.roll(mid.astype(F32), 8, 1),
                               jnp.where(lane < 24, pltpu.roll(lo.astype(F32), 16, 1),
                                         jnp.where(lane == 24, 1.0, 0.0)))).astype(BF16)
    zq = _mm(xb, w_ref[:, _C_Q:_C_K]) * (QK_SCALE * LOG2E) + _mm(caug, p2q_ref[...])
    zk = _mm(xb, w_ref[:, _C_K:_C_KRAW]) + _mm(caug, p2k_ref[...])
    for h in range(N_HEADS):
        qa_ref[h] = zq[:, h * LANES:(h + 1) * LANES].astype(BF16)
        ka_ref[h] = zk[:, h * LANES:(h + 1) * LANES].astype(BF16)

    k_ref[...] = _mm(xb, w_ref[:, _C_KRAW:_C_V])
    v_ref[...] = _mm(xb, w_ref[:, _C_V:_C_MQK])

    u = _mm(xb, w_ref[:, _C_MQK:_C_G])
    conv_s[SUBLANES:SUBLANES + tm, :] = u
    acc = bconv_ref[...] + u * wconv_ref[CONV_WIDTH - 1:CONV_WIDTH, :]
    for j in range(CONV_WIDTH - 1):
        back = CONV_WIDTH - 1 - j
        acc = acc + conv_s[SUBLANES - back:SUBLANES - back + tm, :] * wconv_ref[j:j + 1, :]
    qk = acc * jax.nn.sigmoid(acc)
    mq_ref[...] = qk[:, 0:WIDTH]
    mk_ref[...] = qk[:, WIDTH:2 * WIDTH] * QK_SCALE
    tail = conv_s[tm:tm + SUBLANES, :]
    tail_ref[...] = tail
    conv_s[0:SUBLANES, :] = tail


def _aug_placement():
    p2q = np.zeros((LANES, N_HEADS * LANES), np.float32)
    p2k = np.zeros((LANES, N_HEADS * LANES), np.float32)
    for h in range(N_HEADS):
        base = h * LANES + _AUG
        for part in range(3):
            p2q[part * 8 + h, base + part] = 1.0
            p2q[24, base + 3 + part] = 1.0
            p2k[24, base + part] = 1.0
            p2k[part * 8 + h, base + 3 + part] = -1.0
    return jnp.asarray(p2q, BF16), jnp.asarray(p2k, BF16)


def _split_w_in(w):
    d = w.shape[0]
    sizes = [WIDTH, WIDTH, WIDTH, N_HEADS, 2 * WIDTH, WIDTH, N_HEADS, N_HEADS, WIDTH]
    pts = np.cumsum([0] + sizes)
    parts = [w[:, pts[j]:pts[j + 1]] for j in range(len(sizes))]
    fq, fk, fv, ff, mqk, mv, mi, mf, mo = parts
    gates = jnp.concatenate([ff, mi, mf, jnp.zeros((d, LANES - 3 * N_HEADS), w.dtype)], axis=1)
    return fq, fk, fv, mqk, mv, mo, gates


def _gate_bias(b_fox_f, b_ml_i, b_ml_f):
    return jnp.concatenate([b_fox_f, b_ml_i, b_ml_f, jnp.zeros((LANES - 3 * N_HEADS,), F32)])


def _pad_heads(w):
    d = w.shape[0]
    w3 = w.reshape(d, N_HEADS, HEAD_DIM)
    return jnp.pad(w3, ((0, 0), (0, 0), (0, LANES - HEAD_DIM))).reshape(d, N_HEADS * LANES)


def _inproj_prompt(x, g_mix, w_in, gate_bias, w_conv, b_conv, tm=256):
    s, d = x.shape
    fq, fk, fv, mqk, mv, mo, gates = _split_w_in(w_in)
    w_all = jnp.concatenate([_pad_heads(fq), _pad_heads(fk), fk, fv, mqk, gates], axis=1).astype(BF16)
    w_t = jnp.concatenate([fv.T, gates[:, 0:32].T, mv.T, mo.T], axis=0).astype(BF16)
    p2q, p2k = _aug_placement()
    nblk = s // tm
    full = lambda shape: pl.BlockSpec(shape, lambda i: (0,) * len(shape))
    rows = lambda width: pl.BlockSpec((tm, width), lambda i: (i, 0))
    out_shape = (
        jax.ShapeDtypeStruct((N_HEADS, s, LANES), BF16),
        jax.ShapeDtypeStruct((N_HEADS, s, LANES), BF16),
        jax.ShapeDtypeStruct((s, WIDTH), F32),
        jax.ShapeDtypeStruct((s, WIDTH), F32),
        jax.ShapeDtypeStruct((N_HEADS, nblk, _VT_ROWS, tm), BF16),
        jax.ShapeDtypeStruct((s, LANES), F32),
        jax.ShapeDtypeStruct((32, s), F32),
        jax.ShapeDtypeStruct((s, WIDTH), F32),
        jax.ShapeDtypeStruct((s, WIDTH), F32),
        jax.ShapeDtypeStruct((WIDTH, s), BF16),
        jax.ShapeDtypeStruct((WIDTH, s), F32),
        jax.ShapeDtypeStruct((SUBLANES, 2 * WIDTH), F32),
    )
    out_specs = (
        pl.BlockSpec((N_HEADS, tm, LANES), lambda i: (0, i, 0)),
        pl.BlockSpec((N_HEADS, tm, LANES), lambda i: (0, i, 0)),
        rows(WIDTH), rows(WIDTH),
        pl.BlockSpec((N_HEADS, 1, _VT_ROWS, tm), lambda i: (0, i, 0, 0)),
        rows(LANES),
        pl.BlockSpec((32, tm), lambda i: (0, i)),
        rows(WIDTH), rows(WIDTH),
        pl.BlockSpec((WIDTH, tm), lambda i: (0, i)), pl.BlockSpec((WIDTH, tm), lambda i: (0, i)),
        full((SUBLANES, 2 * WIDTH)),
    )
    return pl.pallas_call(
        functools.partial(_inproj_prompt_kernel, tm=tm),
        grid=(nblk,),
        in_specs=[rows(d), full((1, d)), full(w_all.shape), full(w_t.shape), full((1, LANES)),
                  full((32, 1)), full((CONV_WIDTH, 2 * WIDTH)), full((1, 2 * WIDTH)),
                  full(p2q.shape), full(p2k.shape)],
        out_specs=out_specs,
        out_shape=out_shape,
        scratch_shapes=[pltpu.VMEM((tm + 2 * SUBLANES, 2 * WIDTH), F32),
                        pltpu.VMEM((1, LANES), F32),
                        pltpu.VMEM((SUBLANES, LANES), F32)],
        compiler_params=_params(("arbitrary",)),
        name="inproj_prompt",
    )(x, g_mix.reshape(1, d), w_all, w_t, gate_bias.reshape(1, LANES), gate_bias[0:32].reshape(32, 1),
      w_conv, b_conv.reshape(1, 2 * WIDTH), p2q, p2k)


_S_Q, _S_K, _S_V, _S_MQK, _S_MV, _S_MO, _S_G, _S_END = 0, 512, 1024, 1536, 2560, 3072, 3584, 3712


def _inproj_sample_kernel(x_ref, gmix_ref, w_ref, brow_ref, wconv_ref, bconv_ref, cstate_ref,
                          q_ref, k_ref, v_ref, g_ref, mq_ref, mk_ref, mv_ref, mo_ref, cnew_ref):
    xn = _rms(x_ref[...], gmix_ref[...])
    q_ref[...] = _mm_f32(xn, w_ref[:, _S_Q:_S_K])
    k_ref[...] = _mm_f32(xn, w_ref[:, _S_K:_S_V])
    v_ref[...] = _mm_f32(xn, w_ref[:, _S_V:_S_MQK])
    mv_ref[...] = _mm_f32(xn, w_ref[:, _S_MV:_S_MO])
    mo_ref[...] = _mm_f32(xn, w_ref[:, _S_MO:_S_G])
    g_ref[...] = _gate_tile(_mm_f32(xn, w_ref[:, _S_G:_S_END]) + brow_ref[...], 1)
    u = _mm_f32(xn, w_ref[:, _S_MQK:_S_MV])
    acc = bconv_ref[...] + u * wconv_ref[CONV_WIDTH - 1:CONV_WIDTH, :]
    for j in range(CONV_WIDTH - 1):
        acc = acc + cstate_ref[j] * wconv_ref[j:j + 1, :]
    qk = acc * jax.nn.sigmoid(acc)
    mq_ref[...] = qk[:, 0:WIDTH]
    mk_ref[...] = qk[:, WIDTH:2 * WIDTH] * QK_SCALE
    for j in range(CONV_WIDTH - 2):
        cnew_ref[j] = cstate_ref[j + 1]
    cnew_ref[CONV_WIDTH - 2] = u


def _inproj_sample(x, g_mix, w_in, gate_bias, w_conv, b_conv, conv_state_t):
    n, d = x.shape
    fq, fk, fv, mqk, mv, mo, gates = _split_w_in(w_in)
    w_all = jnp.concatenate([fq, fk, fv, mqk, mv, mo, gates], axis=1)
    wide = jax.ShapeDtypeStruct((n, WIDTH), F32)
    out_shape = (wide, wide, wide, jax.ShapeDtypeStruct((n, LANES), F32), wide, wide, wide, wide,
                 jax.ShapeDtypeStruct((CONV_WIDTH - 1, n, 2 * WIDTH), F32))
    return pl.pallas_call(
        _inproj_sample_kernel,
        out_shape=out_shape,
        compiler_params=pltpu.CompilerParams(vmem_limit_bytes=VMEM_LIMIT),
        name="inproj_sample",
    )(x, g_mix.reshape(1, d), w_all, gate_bias.reshape(1, LANES), w_conv, b_conv.reshape(1, 2 * WIDTH),
      conv_state_t)


_ATTN_SUB = 256
_ATTN_AHEAD = 8
_ATTN_CHUNKS = 1
_VT_ROWS = HEAD_DIM + 16


def _fox_prompt_kernel(qa_ref, ka_ref, vt_ref, o_ref, *, tq, tk):
    qi = pl.program_id(1)
    nsub = tq // _ATTN_SUB
    per_q = tq // tk
    qs = [qa_ref[0, c * _ATTN_SUB:(c + 1) * _ATTN_SUB, :] for c in range(nsub)]

    def chunks(rows):
        step = max(rows // _ATTN_CHUNKS, SUBLANES * 2)
        return [(r0, min(r0 + step, rows)) for r0 in range(0, rows, step)]

    def col_max(s, span):
        return jnp.max(s[span[0]:span[1]], axis=0, keepdims=True)

    def finish(state, m_new, p, vblk):
        m, l, acc = state
        alpha = jnp.exp2(m - m_new)
        pv = _mm(vblk, p)
        return m_new, alpha * l + pv[HEAD_DIM:HEAD_DIM + 1], alpha * acc + pv[0:HEAD_DIM]

    def run_block(states, k_start, v_idx, plan):
        vblk = vt_ref[0, v_idx]

        def logits_of(c, rows, k_lo):
            s = _mm_nt(ka_ref[0, pl.ds(k_start, rows), :], qs[c])
            if k_lo is not None:
                kpos = lax.broadcasted_iota(jnp.int32, (rows, _ATTN_SUB), 0) + k_lo
                qpos = lax.broadcasted_iota(jnp.int32, (rows, _ATTN_SUB), 1) + c * _ATTN_SUB
                s = jnp.where(kpos <= qpos, s, NEG_INF)
            return s

        logits, maxes = {}, {}
        for i in range(min(_ATTN_AHEAD, len(plan))):
            c, rows, k_lo = plan[i]
            logits[i] = logits_of(c, rows, k_lo)
            maxes[i] = [col_max(logits[i], sp) for sp in chunks(rows)]
        for i, (c, rows, _) in enumerate(plan):
            m_new = functools.reduce(jnp.maximum, maxes.pop(i), states[c][0])
            nxt = i + _ATTN_AHEAD
            nxt_spans = []
            if nxt < len(plan):
                logits[nxt] = logits_of(*plan[nxt])
                maxes[nxt] = []
                nxt_spans = chunks(plan[nxt][1])
            s_cur = logits.pop(i)
            p_parts = []
            for n, sp in enumerate(chunks(rows)):
                p_parts.append(jnp.exp2(s_cur[sp[0]:sp[1]] - m_new).astype(BF16))
                if n < len(nxt_spans):
                    maxes[nxt].append(col_max(logits[nxt], nxt_spans[n]))
            for sp in nxt_spans[len(p_parts):]:
                maxes[nxt].append(col_max(logits[nxt], sp))
            states[c] = finish(states[c], m_new, jnp.concatenate(p_parts, axis=0), vblk[:, 0:rows])
        return states

    def body(j, carry):
        full = [(c, tk, None) for c in range(nsub)]
        return tuple(run_block(list(carry), pl.multiple_of(j * tk, tk), j, full))

    init = tuple((jnp.full((1, _ATTN_SUB), NEG_INF, F32), jnp.zeros((1, _ATTN_SUB), F32),
                  jnp.zeros((HEAD_DIM, _ATTN_SUB), F32)) for _ in range(nsub))
    states = list(lax.fori_loop(0, qi * per_q, body, init))

    for d in range(per_q):
        k_lo = d * tk
        plan = []
        for c in range(nsub):
            q_lo, q_hi = c * _ATTN_SUB, (c + 1) * _ATTN_SUB
            rows = min(q_hi - k_lo, tk)
            if rows > 0:
                plan.append((c, rows, k_lo if k_lo + rows > q_lo else None))
        states = run_block(states, pl.multiple_of(qi * tq + k_lo, tk), qi * per_q + d, plan)
    for c in range(nsub):
        m, l, acc = states[c]
        o_ref[:, c * _ATTN_SUB:(c + 1) * _ATTN_SUB] = acc / l


def _fox_prompt(qa, ka, vt, tq):
    _, s, _ = qa.shape
    _, nkv, _, tk = vt.shape
    assert tq % tk == 0 and tq % _ATTN_SUB == 0 and tk % _ATTN_SUB == 0
    return pl.pallas_call(
        functools.partial(_fox_prompt_kernel, tq=tq, tk=tk),
        grid=(N_HEADS, s // tq),
        in_specs=[pl.BlockSpec((1, tq, LANES), lambda h, i: (h, i, 0)),
                  pl.BlockSpec((1, s, LANES), lambda h, i: (h, 0, 0)),
                  pl.BlockSpec((1, nkv, _VT_ROWS, tk), lambda h, i: (h, 0, 0, 0))],
        out_specs=pl.BlockSpec((HEAD_DIM, tq), lambda h, i: (h, i)),
        out_shape=jax.ShapeDtypeStruct((WIDTH, s), F32),
        compiler_params=_params(("arbitrary", "arbitrary")),
        name="fox_prompt",
    )(qa, ka, vt)


_PAGES_PER_STEP = 8


def _fox_sample_kernel(pt_ref, qb_ref, vnb_ref, q_ref, kn_ref, lfn_ref, tsuf_ref, *rest, page, npp):
    del pt_ref
    k_refs = rest[0:npp]
    v_refs = rest[npp:2 * npp]
    lf_refs = rest[2 * npp:3 * npp]
    o_ref = rest[3 * npp]
    qs_s, m_s, l_s, acc_s, carry_s = rest[3 * npp + 1:]
    g = pl.program_id(1)

    @pl.when(g == 0)
    def _():
        qs_s[...] = qb_ref[0] * QK_SCALE
        s_self = jnp.sum(q_ref[0] * QK_SCALE * kn_ref[0], axis=1, keepdims=True)
        m_s[...] = jnp.broadcast_to(s_self, m_s.shape)
        l_s[...] = jnp.ones(l_s.shape, F32)
        lane = lax.broadcasted_iota(jnp.int32, acc_s.shape, 2)
        acc_s[...] = jnp.where(lane == 0, vnb_ref[0], 0.0)
        carry_s[...] = lfn_ref[0]

    lf_all = jnp.concatenate([lf_refs[r][...] for r in range(npp)], axis=0)
    suffix = _mm3_right(lf_all, tsuf_ref[...])
    page_sum = jnp.sum(lf_all, axis=1, keepdims=True)
    carry = carry_s[...][:, 0:1]
    scores = []
    for r in range(npp):
        rows = slice(r * N_HEADS, (r + 1) * N_HEADS)
        qk = jnp.concatenate(
            [jnp.sum(k_refs[r][h] * qs_s[h], axis=0, keepdims=True) for h in range(N_HEADS)], axis=0)
        scores.append(qk + suffix[rows] + carry)
        carry = carry + page_sum[rows]
    carry_s[...] = jnp.broadcast_to(carry, carry_s.shape)

    m = m_s[...][:, 0:1]
    m_new = m
    for s in scores:
        m_new = jnp.maximum(m_new, jnp.max(s, axis=1, keepdims=True))
    alpha = jnp.exp(m - m_new)
    probs = [jnp.exp(s - m_new) for s in scores]
    l_new = alpha * l_s[...][:, 0:1]
    for p in probs:
        l_new = l_new + jnp.sum(p, axis=1, keepdims=True)
    l_s[...] = jnp.broadcast_to(l_new, l_s.shape)
    m_s[...] = jnp.broadcast_to(m_new, m_s.shape)
    for h in range(N_HEADS):
        upd = alpha[h:h + 1, :] * acc_s[h]
        for r in range(npp):
            upd = upd + probs[r][h:h + 1, :] * v_refs[r][h]
        acc_s[h] = upd

    @pl.when(g == pl.num_programs(1) - 1)
    def _():
        rr = lax.broadcasted_iota(jnp.int32, (HEAD_DIM, HEAD_DIM), 0)
        cc = lax.broadcasted_iota(jnp.int32, (HEAD_DIM, HEAD_DIM), 1)
        for h in range(N_HEADS):
            col = jnp.sum(acc_s[h], axis=1, keepdims=True) / l_s[...][h:h + 1, 0:1]
            row = jnp.sum(jnp.where(rr == cc, jnp.broadcast_to(col, (HEAD_DIM, HEAD_DIM)), 0.0),
                          axis=0, keepdims=True)
            o_ref[0, h:h + 1, :] = row


def _fox_sample(q, k_new, v_new, logf_new, cache_k, cache_v, cache_logf, page_table):
    nb, n_pages = page_table.shape
    page = cache_k.shape[2]
    npp = _PAGES_PER_STEP
    assert page == LANES and n_pages % npp == 0
    kt = jnp.transpose(cache_k, (0, 1, 3, 4, 2))
    vt = jnp.transpose(cache_v, (0, 1, 3, 4, 2))
    lft = jnp.transpose(cache_logf, (0, 1, 3, 2))
    q3, kn3, vn3 = (a.reshape(nb, N_HEADS, HEAD_DIM) for a in (q, k_new, v_new))
    lanes = lambda a: jnp.broadcast_to(a[..., None], a.shape + (LANES,))
    tsuf = jnp.asarray(np.tril(np.ones((page, page), np.float32), -1), BF16)

    def page_map(r):
        return lambda b, g, pt: (0, pt[b, n_pages - 1 - (g * npp + r)], 0, 0, 0)

    def lf_map(r):
        return lambda b, g, pt: (0, pt[b, n_pages - 1 - (g * npp + r)], 0, 0)

    per_b = pl.BlockSpec((1, N_HEADS, HEAD_DIM), lambda b, g, pt: (b, 0, 0))
    per_b_lanes = pl.BlockSpec((1, N_HEADS, HEAD_DIM, LANES), lambda b, g, pt: (b, 0, 0, 0))
    kv_specs = [pl.BlockSpec((None, None, N_HEADS, HEAD_DIM, page), page_map(r)) for r in range(npp)]
    lf_specs = [pl.BlockSpec((None, None, N_HEADS, page), lf_map(r)) for r in range(npp)]
    stat = pltpu.VMEM((N_HEADS, LANES), F32)
    grid_spec = pltpu.PrefetchScalarGridSpec(
        num_scalar_prefetch=1,
        grid=(nb, n_pages // npp),
        in_specs=[per_b_lanes, per_b_lanes, per_b, per_b,
                  pl.BlockSpec((1, N_HEADS, LANES), lambda b, g, pt: (b, 0, 0)),
                  pl.BlockSpec((page, page), lambda b, g, pt: (0, 0))] + kv_specs + kv_specs + lf_specs,
        out_specs=per_b,
        scratch_shapes=[pltpu.VMEM((N_HEADS, HEAD_DIM, LANES), F32), stat, stat,
                        pltpu.VMEM((N_HEADS, HEAD_DIM, page), F32), stat],
    )
    out = pl.pallas_call(
        functools.partial(_fox_sample_kernel, page=page, npp=npp),
        grid_spec=grid_spec,
        out_shape=jax.ShapeDtypeStruct((nb, N_HEADS, HEAD_DIM), F32),
        compiler_params=_params(("arbitrary", "arbitrary")),
        name="fox_sample",
    )(page_table, lanes(q3), lanes(vn3), q3, kn3, lanes(logf_new), tsuf,
      *([kt] * npp), *([vt] * npp), *([lft] * npp))
    return out.reshape(nb, WIDTH)


def _gated_head_norm(h, o_pre, g):
    hg = h * jax.nn.sigmoid(o_pre)
    return hg * lax.rsqrt(jnp.mean(hg * hg, axis=-1, keepdims=True) + RMS_EPS) * g


def _mlstm_prompt_kernel(q_ref, k_ref, vt_ref, ot_ref, gcol_ref, grow_ref, gml_ref,
                         yt_ref, c_ref, n_ref, m_ref, *, chunk):
    @pl.when(pl.program_id(0) == 0)
    def _():
        c_ref[...] = jnp.zeros(c_ref.shape, F32)
        n_ref[...] = jnp.zeros(n_ref.shape, F32)
        m_ref[...] = jnp.zeros(m_ref.shape, F32)

    r = lax.broadcasted_iota(jnp.int32, (chunk, chunk), 0)
    c = lax.broadcasted_iota(jnp.int32, (chunk, chunk), 1)
    causal_t = r <= c
    gcol = gcol_ref[...]
    grow = grow_ref[...]
    a_col_all = _mm3_left((c <= r).astype(BF16), gcol)
    a_row_all = _mm3_right(grow[_G_LF:_G_LF + 8], causal_t.astype(BF16))
    lane = lax.broadcasted_iota(jnp.int32, (1, LANES), 1)

    heads = range(N_HEADS)
    own = [(lane >= (h % 2) * HEAD_DIM) & (lane < (h % 2 + 1) * HEAD_DIM) for h in heads]
    pair = [slice((h // 2) * LANES, (h // 2 + 1) * LANES) for h in heads]
    rows = [slice(h * HEAD_DIM, (h + 1) * HEAD_DIM) for h in heads]
    kb = [k_ref[:, pair[h]].astype(BF16) for h in range(0, N_HEADS, 2)]
    qb = [jnp.where(own[h], q_ref[:, pair[h]], 0.0).astype(BF16) for h in heads]
    c_prev = [c_ref[h] for h in heads]
    n_prev = [n_ref[h] for h in heads]
    m_prev = [m_ref[h][:, 0:1] for h in heads]
    st = [_mm_nt(kb[h // 2], qb[h]) for h in heads]
    ctq = [_mm_nt(c_prev[h].astype(BF16), qb[h]) for h in heads]
    qn = [_mm_nt(n_prev[h].astype(BF16), qb[h])[0:1] for h in heads]

    m_t, w_intra, w_inter, decay, m_new, vw, w_rows = [], [], [], [], [], [], []
    for h in heads:
        a_r = a_row_all[h:h + 1, :]
        ig_r = grow[_G_IG + h:_G_IG + h + 1, :]
        key_term = gcol[:, _G_IG + h:_G_IG + h + 1] - a_col_all[:, _G_LF + h:_G_LF + h + 1]
        d = jnp.where(causal_t, a_r + key_term, NEG_INF)
        b = a_r + m_prev[h]
        m_t.append(jnp.maximum(b, jnp.max(d, axis=0, keepdims=True)))
        w_intra.append(jnp.exp(d - m_t[h]))
        w_inter.append(jnp.exp(b - m_t[h]))
        m_new.append(m_t[h][:, chunk - 1:chunk])
        a_last = a_r[:, chunk - 1:chunk]
        decay.append(jnp.exp(a_last + m_prev[h] - m_new[h]))
        w_write = jnp.exp(a_last - a_r + ig_r - m_new[h])
        w_rows.append(jnp.broadcast_to(w_write, (SUBLANES, chunk)).astype(BF16))
        vw.append((vt_ref[rows[h], :].astype(F32) * w_write).astype(BF16))
    scores = [st[h] * w_intra[h] for h in heads]
    sv = [_mm(vt_ref[rows[h], :], scores[h].astype(BF16)) for h in heads]
    c_add = [_mm(vw[h], kb[h // 2]) for h in heads]
    n_add = [_mm(w_rows[h], kb[h // 2]) for h in heads]
    for h in heads:
        num = w_inter[h] * ctq[h] + sv[h]
        den = w_inter[h] * qn[h] + jnp.sum(scores[h], axis=0, keepdims=True)
        hh = num / jnp.maximum(jnp.abs(den), jnp.exp(-m_t[h]))
        c_ref[h] = decay[h] * c_prev[h] + jnp.where(own[h], c_add[h], 0.0)
        n_ref[h] = decay[h] * n_prev[h] + jnp.where(own[h], n_add[h], 0.0)
        m_ref[h] = jnp.broadcast_to(m_new[h], (1, LANES))
        hg = hh * jax.nn.sigmoid(ot_ref[rows[h], :])
        yt_ref[rows[h], :] = hg * lax.rsqrt(jnp.mean(hg * hg, axis=0, keepdims=True) + RMS_EPS) \
            * gml_ref[rows[h], :]


def _mlstm_prompt(mq, mk, mvt, mot, gcol, grow, g_ml):
    s = mq.shape[0]
    chunk = int(np.gcd(s, MLSTM_CHUNK))
    tok_rows = pl.BlockSpec((chunk, WIDTH), lambda i: (i, 0))
    tok_lanes = pl.BlockSpec((WIDTH, chunk), lambda i: (0, i))
    state = lambda shape: pl.BlockSpec(shape, lambda i: (0,) * len(shape))
    yt, ct, n, m = pl.pallas_call(
        functools.partial(_mlstm_prompt_kernel, chunk=chunk),
        grid=(s // chunk,),
        in_specs=[tok_rows, tok_rows, tok_lanes, tok_lanes, pl.BlockSpec((chunk, LANES), lambda i: (i, 0)),
                  pl.BlockSpec((32, chunk), lambda i: (0, i)), state((WIDTH, 1))],
        out_specs=(tok_lanes, state((N_HEADS, HEAD_DIM, LANES)), state((N_HEADS, SUBLANES, LANES)),
                   state((N_HEADS, 1, LANES))),
        out_shape=(jax.ShapeDtypeStruct((WIDTH, s), F32),
                   jax.ShapeDtypeStruct((N_HEADS, HEAD_DIM, LANES), F32),
                   jax.ShapeDtypeStruct((N_HEADS, SUBLANES, LANES), F32),
                   jax.ShapeDtypeStruct((N_HEADS, 1, LANES), F32)),
        compiler_params=_params(("arbitrary",)),
        name="mlstm_prompt",
    )(mq, mk, mvt, mot, gcol, grow, g_ml.reshape(WIDTH, 1))
    half = lambda a: jnp.stack([a[h, ..., (h % 2) * HEAD_DIM:(h % 2 + 1) * HEAD_DIM] for h in range(N_HEADS)])
    return yt, jnp.swapaxes(half(ct), 1, 2), half(n)[:, 0, :], m[:, 0, 0]


def _mlstm_sample_kernel(q_ref, k_ref, v_ref, o_ref, ig_ref, lf_ref, m_ref, c_ref, n_ref, gml_ref,
                         y_ref, cn_ref, nn_ref, mn_ref):
    r = lax.broadcasted_iota(jnp.int32, (HEAD_DIM, HEAD_DIM), 0)
    c = lax.broadcasted_iota(jnp.int32, (HEAD_DIM, HEAD_DIM), 1)
    eye = r == c

    def column(row):
        return jnp.sum(jnp.where(eye, jnp.broadcast_to(row, (HEAD_DIM, HEAD_DIM)), 0.0), axis=1, keepdims=True)

    for h in range(N_HEADS):
        q = q_ref[0, h:h + 1, :]
        k = k_ref[0, h:h + 1, :]
        v = v_ref[0, h:h + 1, :]
        ig = ig_ref[0, h:h + 1, 0:1]
        lf = lf_ref[0, h:h + 1, 0:1]
        m_prev = m_ref[0, h:h + 1, 0:1]
        c_prev = c_ref[0, h]
        n_prev = n_ref[0, h:h + 1, :]
        b = lf + m_prev
        m_t = jnp.maximum(b, ig)
        w_intra = jnp.exp(ig - m_t)
        w_inter = jnp.exp(b - m_t)
        scores = jnp.sum(q * k, axis=1, keepdims=True) * w_intra
        qc = jnp.sum(column(q) * c_prev, axis=0, keepdims=True)
        num = w_inter * qc + scores * v
        den = w_inter * jnp.sum(q * n_prev, axis=1, keepdims=True) + scores
        hh = num / jnp.maximum(jnp.abs(den), jnp.exp(-m_t))
        cn_ref[0, h] = w_inter * c_prev + w_intra * (column(k) * v)
        nn_ref[0, h:h + 1, :] = w_inter * n_prev + w_intra * k
        mn_ref[0, h:h + 1, :] = jnp.broadcast_to(m_t, (1, LANES))
        y_ref[0, h:h + 1, :] = _gated_head_norm(hh, o_ref[0, h:h + 1, :], gml_ref[h:h + 1, :])


def _mlstm_sample(mq, mk, mv, mo, ig, lf, state_c, state_n, state_m, g_ml):
    nb = mq.shape[0]
    heads = lambda a: a.reshape(nb, N_HEADS, HEAD_DIM)
    lanes = lambda a: jnp.broadcast_to(a[:, :, None], (nb, N_HEADS, LANES))
    vec = pl.BlockSpec((1, N_HEADS, HEAD_DIM), lambda b: (b, 0, 0))
    sca = pl.BlockSpec((1, N_HEADS, LANES), lambda b: (b, 0, 0))
    mat = pl.BlockSpec((1, N_HEADS, HEAD_DIM, HEAD_DIM), lambda b: (b, 0, 0, 0))
    y, cn, nn, mn = pl.pallas_call(
        _mlstm_sample_kernel,
        grid=(nb,),
        in_specs=[vec, vec, vec, vec, sca, sca, sca, mat, vec,
                  pl.BlockSpec((N_HEADS, HEAD_DIM), lambda b: (0, 0))],
        out_specs=(vec, mat, vec, sca),
        out_shape=(jax.ShapeDtypeStruct((nb, N_HEADS, HEAD_DIM), F32),
                   jax.ShapeDtypeStruct((nb, N_HEADS, HEAD_DIM, HEAD_DIM), F32),
                   jax.ShapeDtypeStruct((nb, N_HEADS, HEAD_DIM), F32),
                   jax.ShapeDtypeStruct((nb, N_HEADS, LANES), F32)),
        compiler_params=_params(("arbitrary",)),
        name="mlstm_sample",
    )(heads(mq), heads(mk), heads(mv), heads(mo), lanes(ig), lanes(lf), lanes(state_m), state_c, state_n,
      g_ml.reshape(N_HEADS, HEAD_DIM))
    return y.reshape(nb, WIDTH), cn, nn, mn[:, :, 0]


_R_EXPERT, _R_GROUP = 0, N_EXPERTS


def _outproj_router_kernel(yf_ref, yml_ref, x_ref, wf_ref, wm_ref, gfox_ref, gffn_ref, wr_ref, br_ref,
                           x1_ref, xn_ref, comb_ref, *, prompt):
    yf = yf_ref[...]
    if prompt:
        ms = jnp.mean(yf * yf, axis=0, keepdims=True)
        yfn = (yf * lax.rsqrt(ms + RMS_EPS) * gfox_ref[...]).astype(BF16)
        y = _mm_tn(yfn, wf_ref[...]) + _mm_tn(yml_ref[...].astype(BF16), wm_ref[...])
    else:
        y = _mm_f32(_rms(yf, gfox_ref[...]), wf_ref[...]) + _mm_f32(yml_ref[...], wm_ref[...])
    x1 = x_ref[...] + y
    x1_ref[...] = x1
    xn = _rms(x1, gffn_ref[...])
    xb = xn.astype(BF16)
    xn_ref[...] = xb

    router = _mm(xb, wr_ref[...]) if prompt else _mm_f32(xn, wr_ref[...])
    logits = router + br_ref[...]
    lane = lax.broadcasted_iota(jnp.int32, logits.shape, 1)
    big = jnp.int32(2 * LANES)

    def first_argmax(vals):
        top = jnp.max(vals, axis=1, keepdims=True)
        idx = jnp.min(jnp.where(vals == top, lane, big), axis=1, keepdims=True)
        return top, idx

    is_group = (lane >= _R_GROUP) & (lane < _R_GROUP + N_GROUPS)
    lg = jnp.where(is_group, logits, NEG_INF)
    lg_top, lg_idx = first_argmax(lg)
    gate_g = 1.0 / jnp.sum(jnp.exp(lg - lg_top), axis=1, keepdims=True)
    grp = lg_idx - _R_GROUP
    in_grp = (lane >= grp * EXPERTS_PER_GROUP) & (lane < (grp + 1) * EXPERTS_PER_GROUP)
    le = jnp.where(in_grp, logits, NEG_INF)
    top1, idx1 = first_argmax(le)
    top2, idx2 = first_argmax(jnp.where(lane == idx1, NEG_INF, le))
    e2 = jnp.exp(top2 - top1)
    w1 = gate_g / (1.0 + e2)
    w2 = gate_g * e2 / (1.0 + e2)
    comb_ref[...] = jnp.where(lane == idx1, w1, 0.0) + jnp.where(lane == idx2, w2, 0.0)


def _outproj_router(yf, yml, x, w_out, g_fox, g_ffn, w_rg, b_rg, w_re, b_re, tm, prompt):
    n, d = x.shape
    wdt = BF16 if prompt else F32
    wf = w_out[0:WIDTH].astype(wdt)
    wm = w_out[WIDTH:2 * WIDTH].astype(wdt)
    pad = LANES - N_EXPERTS - N_GROUPS
    wr = jnp.concatenate([w_re, w_rg, jnp.zeros((d, pad), F32)], axis=1).astype(wdt)
    br = jnp.concatenate([b_re, b_rg, jnp.zeros((pad,), F32)]).reshape(1, LANES)
    full = lambda shape: pl.BlockSpec(shape, lambda i: (0,) * len(shape))
    rows = lambda width: pl.BlockSpec((tm, width), lambda i: (i, 0))
    if prompt:
        yf_spec = pl.BlockSpec((WIDTH, tm), lambda i: (0, i))
        gfox = g_fox.reshape(WIDTH, 1)
    else:
        yf_spec = rows(WIDTH)
        gfox = g_fox.reshape(1, WIDTH)
    return pl.pallas_call(
        functools.partial(_outproj_router_kernel, prompt=prompt),
        grid=(n // tm,),
        in_specs=[yf_spec, yf_spec, rows(d), full((WIDTH, d)), full((WIDTH, d)), full(gfox.shape),
                  full((1, d)), full((d, LANES)), full((1, LANES))],
        out_specs=(rows(d), rows(d), rows(LANES)),
        out_shape=(jax.ShapeDtypeStruct((n, d), F32), jax.ShapeDtypeStruct((n, d), BF16),
                   jax.ShapeDtypeStruct((n, LANES), F32)),
        compiler_params=_params(("arbitrary",)),
        name="outproj_router_prompt" if prompt else "outproj_router_sample",
    )(yf, yml, x, wf, wm, gfox, g_ffn.reshape(1, d), wr, br)


_MOE_EXPERTS_PER_STEP = 2


def _moe_kernel(xn_ref, comb_ref, x1_ref, wg_ref, wu_ref, wd_ref, gfin_ref, y_ref, acc_s):
    g = pl.program_id(1)
    per_step = wg_ref.shape[0]

    @pl.when(g == 0)
    def _():
        acc_s[...] = jnp.zeros(acc_s.shape, F32)

    xb = xn_ref[...]
    comb = comb_ref[...]
    lane = lax.broadcasted_iota(jnp.int32, comb.shape, 1)
    gates = [_mm(xb, wg_ref[j]) for j in range(per_step)]
    ups = [_mm(xb, wu_ref[j]) for j in range(per_step)]
    for j in range(per_step):
        he = gates[j] * jax.nn.sigmoid(gates[j]) * ups[j]
        out = _mm(he.astype(BF16), wd_ref[j])
        w_e = jnp.sum(jnp.where(lane == g * per_step + j, comb, 0.0), axis=1, keepdims=True)
        acc_s[...] += w_e * out

    @pl.when(g == pl.num_programs(1) - 1)
    def _():
        y_ref[...] = _rms(x1_ref[...] + acc_s[...], gfin_ref[...])


def _moe(xn, comb, x1, wg, wu, wd, g_final, tm):
    n, d = x1.shape
    de = wg.shape[2]
    eps = _MOE_EXPERTS_PER_STEP
    rows = lambda width: pl.BlockSpec((tm, width), lambda i, e: (i, 0))
    return pl.pallas_call(
        _moe_kernel,
        grid=(n // tm, N_EXPERTS // eps),
        in_specs=[rows(d), rows(LANES), rows(d),
                  pl.BlockSpec((eps, d, de), lambda i, e: (e, 0, 0)),
                  pl.BlockSpec((eps, d, de), lambda i, e: (e, 0, 0)),
                  pl.BlockSpec((eps, de, d), lambda i, e: (e, 0, 0)),
                  pl.BlockSpec((1, d), lambda i, e: (0, 0))],
        out_specs=rows(d),
        out_shape=jax.ShapeDtypeStruct((n, d), F32),
        scratch_shapes=[pltpu.VMEM((tm, d), F32)],
        compiler_params=_params(("arbitrary", "arbitrary")),
        name="moe",
    )(xn, comb, x1, wg, wu, wd, g_final.reshape(1, d))


def kernel(x_prompt, x_sample, cache_k, cache_v, cache_logf, state_conv, state_C, state_n, state_m,
           page_table, g_mix, w_in, b_fox_f, b_ml_i, b_ml_f, w_conv, b_conv, g_fox_out, g_ml_out,
           w_out, g_ffn, w_router_group, b_router_group, w_router_expert, b_router_expert,
           w_exp_gate, w_exp_up, w_exp_down, g_final):
    depth = w_in.shape[0]
    batch, seq, d = x_prompt.shape
    nb, dec_seq, _ = x_sample.shape
    assert depth == 1 and batch == 1 and dec_seq == 1
    l = 0
    gate_bias = _gate_bias(b_fox_f[l], b_ml_i[l], b_ml_f[l])
    wg, wu, wd = (w[l].astype(BF16) for w in (w_exp_gate, w_exp_up, w_exp_down))
    router = (w_router_group[l], b_router_group[l], w_router_expert[l], b_router_expert[l])

    t_attn = min(512, seq)
    (qa, ka, k_p, v_p, vt, gcol, grow, mq, mk, mvt, mot, tail) = _inproj_prompt(
        x_prompt[0], g_mix[l], w_in[l], gate_bias, w_conv[l], b_conv[l], tm=t_attn)
    y_fox_t = _fox_prompt(qa, ka, vt, tq=min(2048, seq))
    y_ml_t, c_p, n_p, m_p = _mlstm_prompt(mq, mk, mvt, mot, gcol, grow, g_ml_out[l])
    x1, xn, comb = _outproj_router(y_fox_t, y_ml_t, x_prompt[0], w_out[l], g_fox_out[l], g_ffn[l], *router,
                                   tm=min(512, seq), prompt=True)
    y_prompt = _moe(xn, comb, x1, wg, wu, wd, g_final, tm=min(1024, seq))

    xs = x_sample[:, 0, :]
    (q_s, k_s, v_s, g_s, mq_s, mk_s, mv_s, mo_s, conv_new) = _inproj_sample(
        xs, g_mix[l], w_in[l], gate_bias, w_conv[l], b_conv[l], jnp.transpose(state_conv[l], (1, 0, 2)))
    logf_s = g_s[:, _G_LOGF:_G_LOGF + N_HEADS]
    y_fox_s = _fox_sample(q_s, k_s, v_s, logf_s, cache_k[l:l + 1], cache_v[l:l + 1], cache_logf[l:l + 1],
                          page_table)
    y_ml_s, c_s, n_s, m_s = _mlstm_sample(mq_s, mk_s, mv_s, mo_s, g_s[:, _G_IG:_G_IG + N_HEADS],
                                          g_s[:, _G_LF:_G_LF + N_HEADS], state_C[l], state_n[l], state_m[l],
                                          g_ml_out[l])
    x1_s, xn_s, comb_s = _outproj_router(y_fox_s, y_ml_s, xs, w_out[l], g_fox_out[l], g_ffn[l], *router,
                                         tm=nb, prompt=False)
    y_sample = _moe(xn_s, comb_s, x1_s, wg, wu, wd, g_final, tm=nb)

    heads = lambda a, n: a.reshape(1, n, -1, N_HEADS, HEAD_DIM)
    return (
        y_prompt[None], y_sample[:, None, :],
        heads(k_p, 1), heads(v_p, 1), gcol[:, _G_LOGF:_G_LOGF + N_HEADS].reshape(1, 1, seq, N_HEADS),
        tail[SUBLANES - (CONV_WIDTH - 1):][None, None],
        c_p[None, None], n_p[None, None], m_p[None, None],
        heads(k_s, nb), heads(v_s, nb), logf_s.reshape(1, nb, 1, N_HEADS),
        jnp.transpose(conv_new, (1, 0, 2))[None],
        c_s[None], n_s[None], m_s[None],
    )
```

```python
import functools

import numpy as np
import jax
import jax.numpy as jnp
from jax import lax
from jax.experimental import pallas as pl
from jax.experimental.pallas import tpu as pltpu

HEAD_DIM = 64
N_HEADS = 8
WIDTH = N_HEADS * HEAD_DIM
CONV_WIDTH = 4
MLSTM_CHUNK = 128
N_GROUPS = 4
EXPERTS_PER_GROUP = 8
N_EXPERTS = N_GROUPS * EXPERTS_PER_GROUP
RMS_EPS = 1e-6
NEG_INF = -1e30
QK_SCALE = HEAD_DIM ** -0.5
LOG2E = 1.4426950408889634

LANES = 128
SUBLANES = 8
VMEM_LIMIT = 56 * 1024 * 1024

F32 = jnp.float32
BF16 = jnp.bfloat16


def _mm(a, b):
    return jnp.dot(a, b, preferred_element_type=F32)


def _mm_f32(a, b):
    return jnp.dot(a, b, preferred_element_type=F32, precision=lax.Precision.HIGHEST)


def _mm_nt(a, b):
    return lax.dot_general(a, b, (((1,), (1,)), ((), ())), preferred_element_type=F32)


def _mm_tn(a, b):
    return lax.dot_general(a, b, (((0,), (0,)), ((), ())), preferred_element_type=F32)


def _split3(a):
    hi = a.astype(BF16)
    r = a - hi.astype(F32)
    mid = r.astype(BF16)
    lo = (r - mid.astype(F32)).astype(BF16)
    return hi, mid, lo


def _mm3_right(a, b01):
    hi, mid, lo = _split3(a)
    return _mm(hi, b01) + _mm(mid, b01) + _mm(lo, b01)


def _mm3_left(a01, b):
    hi, mid, lo = _split3(b)
    return _mm(a01, hi) + _mm(a01, mid) + _mm(a01, lo)


def _log_sigmoid(x):
    return jnp.minimum(x, 0.0) - jnp.log1p(jnp.exp(-jnp.abs(x)))


def _rms(x, g):
    return x * lax.rsqrt(jnp.mean(x * x, axis=-1, keepdims=True) + RMS_EPS) * g


def _params(sem):
    return pltpu.CompilerParams(dimension_semantics=sem, vmem_limit_bytes=VMEM_LIMIT)


_C_Q, _C_K, _C_KRAW, _C_V, _C_MQK, _C_G, _C_END = 0, 1024, 2048, 2560, 3072, 4096, 4224
_T_V, _T_G, _T_MV, _T_MO, _T_END = 0, 512, 544, 1056, 1568
_G_LOGF, _G_IG, _G_LF, _G_CUM = 0, 8, 16, 24
_AUG = HEAD_DIM


def _gate_tile(z, lane_axis):
    idx = lax.broadcasted_iota(jnp.int32, z.shape, lane_axis)
    is_ig = (idx >= _G_IG) & (idx < _G_LF)
    return jnp.where(is_ig, z, _log_sigmoid(z))


def _inproj_prompt_kernel(x_ref, gmix_ref, w_ref, wt_ref, brow_ref, bcol_ref, wconv_ref, bconv_ref,
                          p2q_ref, p2k_ref,
                          qa_ref, ka_ref, k_ref, v_ref, vt_ref, gcol_ref, grow_ref,
                          mq_ref, mk_ref, mvt_ref, mot_ref, tail_ref,
                          conv_s, ccol_s, crow_s, *, tm):
    i = pl.program_id(0)

    @pl.when(i == 0)
    def _():
        conv_s[0:SUBLANES, :] = jnp.zeros((SUBLANES, conv_s.shape[1]), F32)
        ccol_s[...] = jnp.zeros(ccol_s.shape, F32)
        crow_s[...] = jnp.zeros(crow_s.shape, F32)

    xb = _rms(x_ref[...], gmix_ref[...]).astype(BF16)

    lane = lax.broadcasted_iota(jnp.int32, (tm, LANES), 1)
    g = _gate_tile(_mm(xb, w_ref[:, _C_G:_C_END]) + brow_ref[...], 1)
    zt = _mm_nt(wt_ref[...], xb)
    for h in range(N_HEADS):
        vt_ref[h, 0, 0:HEAD_DIM, :] = zt[h * HEAD_DIM:(h + 1) * HEAD_DIM].astype(BF16)
        vt_ref[h, 0, HEAD_DIM:_VT_ROWS, :] = jnp.ones((_VT_ROWS - HEAD_DIM, tm), BF16)
    mvt_ref[...] = zt[_T_MV:_T_MO].astype(BF16)
    mot_ref[...] = zt[_T_MO:_T_END]
    gt = _gate_tile(zt[_T_G:_T_MV] + bcol_ref[...], 0)

    r = lax.broadcasted_iota(jnp.int32, (tm, tm), 0)
    c = lax.broadcasted_iota(jnp.int32, (tm, tm), 1)
    ltri = (c <= r).astype(BF16)
    utri = (r <= c).astype(BF16)
    cs = _mm3_left(ltri, g) + ccol_s[...]
    ccol_s[...] = cs[tm - 1:tm, :]
    cst = _mm3_right(gt[0:8], utri) + crow_s[...][:, 0:1]
    crow_s[...] = jnp.broadcast_to(cst[:, tm - 1:tm], crow_s.shape)
    in_cum = (lane >= _G_CUM) & (lane < _G_CUM + 8)
    gcol_ref[...] = jnp.where(in_cum, pltpu.roll(cs, _G_CUM, 1), g)
    grow_ref[...] = jnp.concatenate([gt[0:24], cst], axis=0)

    hi, mid, lo = _split3(cs * LOG2E)
    caug = jnp.where(lane < 8, hi.astype(F32),
                     jnp.where(lane < 16, pltpu.roll(mid.astype(F32), 8, 1),
                               jnp.where(lane < 24, pltpu.roll(lo.astype(F32), 16, 1),
                                         jnp.where(lane == 24, 1.0, 0.0)))).astype(BF16)
    zq = _mm(xb, w_ref[:, _C_Q:_C_K]) * (QK_SCALE * LOG2E) + _mm(caug, p2q_ref[...])
    zk = _mm(xb, w_ref[:, _C_K:_C_KRAW]) + _mm(caug, p2k_ref[...])
    for h in range(N_HEADS):
        qa_ref[h] = zq[:, h * LANES:(h + 1) * LANES].astype(BF16)
        ka_ref[h] = zk[:, h * LANES:(h + 1) * LANES].astype(BF16)

    k_ref[...] = _mm(xb, w_ref[:, _C_KRAW:_C_V])
    v_ref[...] = _mm(xb, w_ref[:, _C_V:_C_MQK])

    u = _mm(xb, w_ref[:, _C_MQK:_C_G])
    conv_s[SUBLANES:SUBLANES + tm, :] = u
    acc = bconv_ref[...] + u * wconv_ref[CONV_WIDTH - 1:CONV_WIDTH, :]
    for j in range(CONV_WIDTH - 1):
        back = CONV_WIDTH - 1 - j
        acc = acc + conv_s[SUBLANES - back:SUBLANES - back + tm, :] * wconv_ref[j:j + 1, :]
    qk = acc * jax.nn.sigmoid(acc)
    mq_ref[...] = qk[:, 0:WIDTH]
    mk_ref[...] = qk[:, WIDTH:2 * WIDTH] * QK_SCALE
    tail = conv_s[tm:tm + SUBLANES, :]
    tail_ref[...] = tail
    conv_s[0:SUBLANES, :] = tail


def _aug_placement():
    p2q = np.zeros((LANES, N_HEADS * LANES), np.float32)
    p2k = np.zeros((LANES, N_HEADS * LANES), np.float32)
    for h in range(N_HEADS):
        base = h * LANES + _AUG
        for part in range(3):
            p2q[part * 8 + h, base + part] = 1.0
            p2q[24, base + 3 + part] = 1.0
            p2k[24, base + part] = 1.0
            p2k[part * 8 + h, base + 3 + part] = -1.0
    return jnp.asarray(p2q, BF16), jnp.asarray(p2k, BF16)


def _split_w_in(w):
    d = w.shape[0]
    sizes = [WIDTH, WIDTH, WIDTH, N_HEADS, 2 * WIDTH, WIDTH, N_HEADS, N_HEADS, WIDTH]
    pts = np.cumsum([0] + sizes)
    parts = [w[:, pts[j]:pts[j + 1]] for j in range(len(sizes))]
    fq, fk, fv, ff, mqk, mv, mi, mf, mo = parts
    gates = jnp.concatenate([ff, mi, mf, jnp.zeros((d, LANES - 3 * N_HEADS), w.dtype)], axis=1)
    return fq, fk, fv, mqk, mv, mo, gates


def _gate_bias(b_fox_f, b_ml_i, b_ml_f):
    return jnp.concatenate([b_fox_f, b_ml_i, b_ml_f, jnp.zeros((LANES - 3 * N_HEADS,), F32)])


def _pad_heads(w):
    d = w.shape[0]
    w3 = w.reshape(d, N_HEADS, HEAD_DIM)
    return jnp.pad(w3, ((0, 0), (0, 0), (0, LANES - HEAD_DIM))).reshape(d, N_HEADS * LANES)


def _inproj_prompt(x, g_mix, w_in, gate_bias, w_conv, b_conv, tm=256):
    s, d = x.shape
    fq, fk, fv, mqk, mv, mo, gates = _split_w_in(w_in)
    w_all = jnp.concatenate([_pad_heads(fq), _pad_heads(fk), fk, fv, mqk, gates], axis=1).astype(BF16)
    w_t = jnp.concatenate([fv.T, gates[:, 0:32].T, mv.T, mo.T], axis=0).astype(BF16)
    p2q, p2k = _aug_placement()
    nblk = s // tm
    full = lambda shape: pl.BlockSpec(shape, lambda i: (0,) * len(shape))
    rows = lambda width: pl.BlockSpec((tm, width), lambda i: (i, 0))
    out_shape = (
        jax.ShapeDtypeStruct((N_HEADS, s, LANES), BF16),
        jax.ShapeDtypeStruct((N_HEADS, s, LANES), BF16),
        jax.ShapeDtypeStruct((s, WIDTH), F32),
        jax.ShapeDtypeStruct((s, WIDTH), F32),
        jax.ShapeDtypeStruct((N_HEADS, nblk, _VT_ROWS, tm), BF16),
        jax.ShapeDtypeStruct((s, LANES), F32),
        jax.ShapeDtypeStruct((32, s), F32),
        jax.ShapeDtypeStruct((s, WIDTH), F32),
        jax.ShapeDtypeStruct((s, WIDTH), F32),
        jax.ShapeDtypeStruct((WIDTH, s), BF16),
        jax.ShapeDtypeStruct((WIDTH, s), F32),
        jax.ShapeDtypeStruct((SUBLANES, 2 * WIDTH), F32),
    )
    out_specs = (
        pl.BlockSpec((N_HEADS, tm, LANES), lambda i: (0, i, 0)),
        pl.BlockSpec((N_HEADS, tm, LANES), lambda i: (0, i, 0)),
        rows(WIDTH), rows(WIDTH),
        pl.BlockSpec((N_HEADS, 1, _VT_ROWS, tm), lambda i: (0, i, 0, 0)),
        rows(LANES),
        pl.BlockSpec((32, tm), lambda i: (0, i)),
        rows(WIDTH), rows(WIDTH),
        pl.BlockSpec((WIDTH, tm), lambda i: (0, i)), pl.BlockSpec((WIDTH, tm), lambda i: (0, i)),
        full((SUBLANES, 2 * WIDTH)),
    )
    return pl.pallas_call(
        functools.partial(_inproj_prompt_kernel, tm=tm),
        grid=(nblk,),
        in_specs=[rows(d), full((1, d)), full(w_all.shape), full(w_t.shape), full((1, LANES)),
                  full((32, 1)), full((CONV_WIDTH, 2 * WIDTH)), full((1, 2 * WIDTH)),
                  full(p2q.shape), full(p2k.shape)],
        out_specs=out_specs,
        out_shape=out_shape,
        scratch_shapes=[pltpu.VMEM((tm + 2 * SUBLANES, 2 * WIDTH), F32),
                        pltpu.VMEM((1, LANES), F32),
                        pltpu.VMEM((SUBLANES, LANES), F32)],
        compiler_params=_params(("arbitrary",)),
        name="inproj_prompt",
    )(x, g_mix.reshape(1, d), w_all, w_t, gate_bias.reshape(1, LANES), gate_bias[0:32].reshape(32, 1),
      w_conv, b_conv.reshape(1, 2 * WIDTH), p2q, p2k)


_S_Q, _S_K, _S_V, _S_MQK, _S_MV, _S_MO, _S_G, _S_END = 0, 512, 1024, 1536, 2560, 3072, 3584, 3712


def _inproj_sample_kernel(x_ref, gmix_ref, w_ref, brow_ref, wconv_ref, bconv_ref, cstate_ref,
                          q_ref, k_ref, v_ref, g_ref, mq_ref, mk_ref, mv_ref, mo_ref, cnew_ref):
    xn = _rms(x_ref[...], gmix_ref[...])
    q_ref[...] = _mm_f32(xn, w_ref[:, _S_Q:_S_K])
    k_ref[...] = _mm_f32(xn, w_ref[:, _S_K:_S_V])
    v_ref[...] = _mm_f32(xn, w_ref[:, _S_V:_S_MQK])
    mv_ref[...] = _mm_f32(xn, w_ref[:, _S_MV:_S_MO])
    mo_ref[...] = _mm_f32(xn, w_ref[:, _S_MO:_S_G])
    g_ref[...] = _gate_tile(_mm_f32(xn, w_ref[:, _S_G:_S_END]) + brow_ref[...], 1)
    u = _mm_f32(xn, w_ref[:, _S_MQK:_S_MV])
    acc = bconv_ref[...] + u * wconv_ref[CONV_WIDTH - 1:CONV_WIDTH, :]
    for j in range(CONV_WIDTH - 1):
        acc = acc + cstate_ref[j] * wconv_ref[j:j + 1, :]
    qk = acc * jax.nn.sigmoid(acc)
    mq_ref[...] = qk[:, 0:WIDTH]
    mk_ref[...] = qk[:, WIDTH:2 * WIDTH] * QK_SCALE
    for j in range(CONV_WIDTH - 2):
        cnew_ref[j] = cstate_ref[j + 1]
    cnew_ref[CONV_WIDTH - 2] = u


def _inproj_sample(x, g_mix, w_in, gate_bias, w_conv, b_conv, conv_state_t):
    n, d = x.shape
    fq, fk, fv, mqk, mv, mo, gates = _split_w_in(w_in)
    w_all = jnp.concatenate([fq, fk, fv, mqk, mv, mo, gates], axis=1)
    wide = jax.ShapeDtypeStruct((n, WIDTH), F32)
    out_shape = (wide, wide, wide, jax.ShapeDtypeStruct((n, LANES), F32), wide, wide, wide, wide,
                 jax.ShapeDtypeStruct((CONV_WIDTH - 1, n, 2 * WIDTH), F32))
    return pl.pallas_call(
        _inproj_sample_kernel,
        out_shape=out_shape,
        compiler_params=pltpu.CompilerParams(vmem_limit_bytes=VMEM_LIMIT),
        name="inproj_sample",
    )(x, g_mix.reshape(1, d), w_all, gate_bias.reshape(1, LANES), w_conv, b_conv.reshape(1, 2 * WIDTH),
      conv_state_t)


_ATTN_SUB = 256
_ATTN_AHEAD = 16
_ATTN_CHUNKS = 1
_VT_ROWS = HEAD_DIM + 16


def _fox_prompt_kernel(qa_ref, ka_ref, vt_ref, o_ref, *, tq, tk):
    qi = pl.program_id(1)
    nsub = tq // _ATTN_SUB
    per_q = tq // tk
    qs = [qa_ref[0, c * _ATTN_SUB:(c + 1) * _ATTN_SUB, :] for c in range(nsub)]

    def chunks(rows):
        step = max(rows // _ATTN_CHUNKS, SUBLANES * 2)
        return [(r0, min(r0 + step, rows)) for r0 in range(0, rows, step)]

    def col_max(s, span):
        return jnp.max(s[span[0]:span[1]], axis=0, keepdims=True)

    def finish(state, m_new, p, vblk):
        m, l, acc = state
        alpha = jnp.exp2(m - m_new)
        pv = _mm(vblk, p)
        return m_new, alpha * l + pv[HEAD_DIM:HEAD_DIM + 1], alpha * acc + pv[0:HEAD_DIM]

    def run_block(states, k_start, v_idx, plan):
        vblk = vt_ref[0, v_idx]

        def logits_of(c, rows, k_lo):
            s = _mm_nt(ka_ref[0, pl.ds(k_start, rows), :], qs[c])
            if k_lo is not None:
                kpos = lax.broadcasted_iota(jnp.int32, (rows, _ATTN_SUB), 0) + k_lo
                qpos = lax.broadcasted_iota(jnp.int32, (rows, _ATTN_SUB), 1) + c * _ATTN_SUB
                s = jnp.where(kpos <= qpos, s, NEG_INF)
            return s

        logits, maxes = {}, {}
        for i in range(min(_ATTN_AHEAD, len(plan))):
            c, rows, k_lo = plan[i]
            logits[i] = logits_of(c, rows, k_lo)
            maxes[i] = [col_max(logits[i], sp) for sp in chunks(rows)]
        for i, (c, rows, _) in enumerate(plan):
            m_new = functools.reduce(jnp.maximum, maxes.pop(i), states[c][0])
            nxt = i + _ATTN_AHEAD
            nxt_spans = []
            if nxt < len(plan):
                logits[nxt] = logits_of(*plan[nxt])
                maxes[nxt] = []
                nxt_spans = chunks(plan[nxt][1])
            s_cur = logits.pop(i)
            p_parts = []
            for n, sp in enumerate(chunks(rows)):
                p_parts.append(jnp.exp2(s_cur[sp[0]:sp[1]] - m_new).astype(BF16))
                if n < len(nxt_spans):
                    maxes[nxt].append(col_max(logits[nxt], nxt_spans[n]))
            for sp in nxt_spans[len(p_parts):]:
                maxes[nxt].append(col_max(logits[nxt], sp))
            states[c] = finish(states[c], m_new, jnp.concatenate(p_parts, axis=0), vblk[:, 0:rows])
        return states

    def body(j, carry):
        full = [(c, tk, None) for c in range(nsub)]
        return tuple(run_block(list(carry), pl.multiple_of(j * tk, tk), j, full))

    init = tuple((jnp.full((1, _ATTN_SUB), NEG_INF, F32), jnp.zeros((1, _ATTN_SUB), F32),
                  jnp.zeros((HEAD_DIM, _ATTN_SUB), F32)) for _ in range(nsub))
    states = list(lax.fori_loop(0, qi * per_q, body, init))

    for d in range(per_q):
        k_lo = d * tk
        plan = []
        for c in range(nsub):
            q_lo, q_hi = c * _ATTN_SUB, (c + 1) * _ATTN_SUB
            rows = min(q_hi - k_lo, tk)
            if rows > 0:
                plan.append((c, rows, k_lo if k_lo + rows > q_lo else None))
        states = run_block(states, pl.multiple_of(qi * tq + k_lo, tk), qi * per_q + d, plan)
    for c in range(nsub):
        m, l, acc = states[c]
        o_ref[:, c * _ATTN_SUB:(c + 1) * _ATTN_SUB] = acc / l


def _fox_prompt(qa, ka, vt, tq):
    _, s, _ = qa.shape
    _, nkv, _, tk = vt.shape
    assert tq % tk == 0 and tq % _ATTN_SUB == 0 and tk % _ATTN_SUB == 0
    return pl.pallas_call(
        functools.partial(_fox_prompt_kernel, tq=tq, tk=tk),
        grid=(N_HEADS, s // tq),
        in_specs=[pl.BlockSpec((1, tq, LANES), lambda h, i: (h, i, 0)),
                  pl.BlockSpec((1, s, LANES), lambda h, i: (h, 0, 0)),
                  pl.BlockSpec((1, nkv, _VT_ROWS, tk), lambda h, i: (h, 0, 0, 0))],
        out_specs=pl.BlockSpec((HEAD_DIM, tq), lambda h, i: (h, i)),
        out_shape=jax.ShapeDtypeStruct((WIDTH, s), F32),
        compiler_params=_params(("arbitrary", "arbitrary")),
        name="fox_prompt",
    )(qa, ka, vt)


_PAGES_PER_STEP = 8


def _fox_sample_kernel(pt_ref, qb_ref, vnb_ref, q_ref, kn_ref, lfn_ref, tsuf_ref, *rest, page, npp):
    del pt_ref
    k_refs = rest[0:npp]
    v_refs = rest[npp:2 * npp]
    lf_refs = rest[2 * npp:3 * npp]
    o_ref = rest[3 * npp]
    qs_s, m_s, l_s, acc_s, carry_s = rest[3 * npp + 1:]
    g = pl.program_id(1)

    @pl.when(g == 0)
    def _():
        qs_s[...] = qb_ref[0] * QK_SCALE
        s_self = jnp.sum(q_ref[0] * QK_SCALE * kn_ref[0], axis=1, keepdims=True)
        m_s[...] = jnp.broadcast_to(s_self, m_s.shape)
        l_s[...] = jnp.ones(l_s.shape, F32)
        lane = lax.broadcasted_iota(jnp.int32, acc_s.shape, 2)
        acc_s[...] = jnp.where(lane == 0, vnb_ref[0], 0.0)
        carry_s[...] = lfn_ref[0]

    lf_all = jnp.concatenate([lf_refs[r][...] for r in range(npp)], axis=0)
    suffix = _mm3_right(lf_all, tsuf_ref[...])
    page_sum = jnp.sum(lf_all, axis=1, keepdims=True)
    carry = carry_s[...][:, 0:1]
    scores = []
    for r in range(npp):
        rows = slice(r * N_HEADS, (r + 1) * N_HEADS)
        qk = jnp.concatenate(
            [jnp.sum(k_refs[r][h] * qs_s[h], axis=0, keepdims=True) for h in range(N_HEADS)], axis=0)
        scores.append(qk + suffix[rows] + carry)
        carry = carry + page_sum[rows]
    carry_s[...] = jnp.broadcast_to(carry, carry_s.shape)

    m = m_s[...][:, 0:1]
    m_new = m
    for s in scores:
        m_new = jnp.maximum(m_new, jnp.max(s, axis=1, keepdims=True))
    alpha = jnp.exp(m - m_new)
    probs = [jnp.exp(s - m_new) for s in scores]
    l_new = alpha * l_s[...][:, 0:1]
    for p in probs:
        l_new = l_new + jnp.sum(p, axis=1, keepdims=True)
    l_s[...] = jnp.broadcast_to(l_new, l_s.shape)
    m_s[...] = jnp.broadcast_to(m_new, m_s.shape)
    for h in range(N_HEADS):
        upd = alpha[h:h + 1, :] * acc_s[h]
        for r in range(npp):
            upd = upd + probs[r][h:h + 1, :] * v_refs[r][h]
        acc_s[h] = upd

    @pl.when(g == pl.num_programs(1) - 1)
    def _():
        rr = lax.broadcasted_iota(jnp.int32, (HEAD_DIM, HEAD_DIM), 0)
        cc = lax.broadcasted_iota(jnp.int32, (HEAD_DIM, HEAD_DIM), 1)
        for h in range(N_HEADS):
            col = jnp.sum(acc_s[h], axis=1, keepdims=True) / l_s[...][h:h + 1, 0:1]
            row = jnp.sum(jnp.where(rr == cc, jnp.broadcast_to(col, (HEAD_DIM, HEAD_DIM)), 0.0),
                          axis=0, keepdims=True)
            o_ref[0, h:h + 1, :] = row


def _fox_sample(q, k_new, v_new, logf_new, cache_k, cache_v, cache_logf, page_table):
    nb, n_pages = page_table.shape
    page = cache_k.shape[2]
    npp = _PAGES_PER_STEP
    assert page == LANES and n_pages % npp == 0
    kt = jnp.transpose(cache_k, (0, 1, 3, 4, 2))
    vt = jnp.transpose(cache_v, (0, 1, 3, 4, 2))
    lft = jnp.transpose(cache_logf, (0, 1, 3, 2))
    q3, kn3, vn3 = (a.reshape(nb, N_HEADS, HEAD_DIM) for a in (q, k_new, v_new))
    lanes = lambda a: jnp.broadcast_to(a[..., None], a.shape + (LANES,))
    tsuf = jnp.asarray(np.tril(np.ones((page, page), np.float32), -1), BF16)

    def page_map(r):
        return lambda b, g, pt: (0, pt[b, n_pages - 1 - (g * npp + r)], 0, 0, 0)

    def lf_map(r):
        return lambda b, g, pt: (0, pt[b, n_pages - 1 - (g * npp + r)], 0, 0)

    per_b = pl.BlockSpec((1, N_HEADS, HEAD_DIM), lambda b, g, pt: (b, 0, 0))
    per_b_lanes = pl.BlockSpec((1, N_HEADS, HEAD_DIM, LANES), lambda b, g, pt: (b, 0, 0, 0))
    kv_specs = [pl.BlockSpec((None, None, N_HEADS, HEAD_DIM, page), page_map(r)) for r in range(npp)]
    lf_specs = [pl.BlockSpec((None, None, N_HEADS, page), lf_map(r)) for r in range(npp)]
    stat = pltpu.VMEM((N_HEADS, LANES), F32)
    grid_spec = pltpu.PrefetchScalarGridSpec(
        num_scalar_prefetch=1,
        grid=(nb, n_pages // npp),
        in_specs=[per_b_lanes, per_b_lanes, per_b, per_b,
                  pl.BlockSpec((1, N_HEADS, LANES), lambda b, g, pt: (b, 0, 0)),
                  pl.BlockSpec((page, page), lambda b, g, pt: (0, 0))] + kv_specs + kv_specs + lf_specs,
        out_specs=per_b,
        scratch_shapes=[pltpu.VMEM((N_HEADS, HEAD_DIM, LANES), F32), stat, stat,
                        pltpu.VMEM((N_HEADS, HEAD_DIM, page), F32), stat],
    )
    out = pl.pallas_call(
        functools.partial(_fox_sample_kernel, page=page, npp=npp),
        grid_spec=grid_spec,
        out_shape=jax.ShapeDtypeStruct((nb, N_HEADS, HEAD_DIM), F32),
        compiler_params=_params(("arbitrary", "arbitrary")),
        name="fox_sample",
    )(page_table, lanes(q3), lanes(vn3), q3, kn3, lanes(logf_new), tsuf,
      *([kt] * npp), *([vt] * npp), *([lft] * npp))
    return out.reshape(nb, WIDTH)


def _gated_head_norm(h, o_pre, g):
    hg = h * jax.nn.sigmoid(o_pre)
    return hg * lax.rsqrt(jnp.mean(hg * hg, axis=-1, keepdims=True) + RMS_EPS) * g


def _mlstm_prompt_kernel(q_ref, k_ref, vt_ref, ot_ref, gcol_ref, grow_ref, gml_ref,
                         yt_ref, c_ref, n_ref, m_ref, *, chunk):
    @pl.when(pl.program_id(0) == 0)
    def _():
        c_ref[...] = jnp.zeros(c_ref.shape, F32)
        n_ref[...] = jnp.zeros(n_ref.shape, F32)
        m_ref[...] = jnp.zeros(m_ref.shape, F32)

    r = lax.broadcasted_iota(jnp.int32, (chunk, chunk), 0)
    c = lax.broadcasted_iota(jnp.int32, (chunk, chunk), 1)
    causal_t = r <= c
    gcol = gcol_ref[...]
    grow = grow_ref[...]
    a_col_all = _mm3_left((c <= r).astype(BF16), gcol)
    a_row_all = _mm3_right(grow[_G_LF:_G_LF + 8], causal_t.astype(BF16))
    lane = lax.broadcasted_iota(jnp.int32, (1, LANES), 1)

    heads = range(N_HEADS)
    own = [(lane >= (h % 2) * HEAD_DIM) & (lane < (h % 2 + 1) * HEAD_DIM) for h in heads]
    pair = [slice((h // 2) * LANES, (h // 2 + 1) * LANES) for h in heads]
    rows = [slice(h * HEAD_DIM, (h + 1) * HEAD_DIM) for h in heads]
    kb = [k_ref[:, pair[h]].astype(BF16) for h in range(0, N_HEADS, 2)]
    qb = [jnp.where(own[h], q_ref[:, pair[h]], 0.0).astype(BF16) for h in heads]
    c_prev = [c_ref[h] for h in heads]
    n_prev = [n_ref[h] for h in heads]
    m_prev = [m_ref[h][:, 0:1] for h in heads]
    st = [_mm_nt(kb[h // 2], qb[h]) for h in heads]
    ctq = [_mm_nt(c_prev[h].astype(BF16), qb[h]) for h in heads]
    qn = [_mm_nt(n_prev[h].astype(BF16), qb[h])[0:1] for h in heads]

    m_t, w_intra, w_inter, decay, m_new, vw, w_rows = [], [], [], [], [], [], []
    for h in heads:
        a_r = a_row_all[h:h + 1, :]
        ig_r = grow[_G_IG + h:_G_IG + h + 1, :]
        key_term = gcol[:, _G_IG + h:_G_IG + h + 1] - a_col_all[:, _G_LF + h:_G_LF + h + 1]
        d = jnp.where(causal_t, a_r + key_term, NEG_INF)
        b = a_r + m_prev[h]
        m_t.append(jnp.maximum(b, jnp.max(d, axis=0, keepdims=True)))
        w_intra.append(jnp.exp(d - m_t[h]))
        w_inter.append(jnp.exp(b - m_t[h]))
        m_new.append(m_t[h][:, chunk - 1:chunk])
        a_last = a_r[:, chunk - 1:chunk]
        decay.append(jnp.exp(a_last + m_prev[h] - m_new[h]))
        w_write = jnp.exp(a_last - a_r + ig_r - m_new[h])
        w_rows.append(jnp.broadcast_to(w_write, (SUBLANES, chunk)).astype(BF16))
        vw.append((vt_ref[rows[h], :].astype(F32) * w_write).astype(BF16))
    scores = [st[h] * w_intra[h] for h in heads]
    sv = [_mm(vt_ref[rows[h], :], scores[h].astype(BF16)) for h in heads]
    c_add = [_mm(vw[h], kb[h // 2]) for h in heads]
    n_add = [_mm(w_rows[h], kb[h // 2]) for h in heads]
    for h in heads:
        num = w_inter[h] * ctq[h] + sv[h]
        den = w_inter[h] * qn[h] + jnp.sum(scores[h], axis=0, keepdims=True)
        hh = num / jnp.maximum(jnp.abs(den), jnp.exp(-m_t[h]))
        c_ref[h] = decay[h] * c_prev[h] + jnp.where(own[h], c_add[h], 0.0)
        n_ref[h] = decay[h] * n_prev[h] + jnp.where(own[h], n_add[h], 0.0)
        m_ref[h] = jnp.broadcast_to(m_new[h], (1, LANES))
        hg = hh * jax.nn.sigmoid(ot_ref[rows[h], :])
        yt_ref[rows[h], :] = hg * lax.rsqrt(jnp.mean(hg * hg, axis=0, keepdims=True) + RMS_EPS) \
            * gml_ref[rows[h], :]


def _mlstm_prompt(mq, mk, mvt, mot, gcol, grow, g_ml):
    s = mq.shape[0]
    chunk = int(np.gcd(s, MLSTM_CHUNK))
    tok_rows = pl.BlockSpec((chunk, WIDTH), lambda i: (i, 0))
    tok_lanes = pl.BlockSpec((WIDTH, chunk), lambda i: (0, i))
    state = lambda shape: pl.BlockSpec(shape, lambda i: (0,) * len(shape))
    yt, ct, n, m = pl.pallas_call(
        functools.partial(_mlstm_prompt_kernel, chunk=chunk),
        grid=(s // chunk,),
        in_specs=[tok_rows, tok_rows, tok_lanes, tok_lanes, pl.BlockSpec((chunk, LANES), lambda i: (i, 0)),
                  pl.BlockSpec((32, chunk), lambda i: (0, i)), state((WIDTH, 1))],
        out_specs=(tok_lanes, state((N_HEADS, HEAD_DIM, LANES)), state((N_HEADS, SUBLANES, LANES)),
                   state((N_HEADS, 1, LANES))),
        out_shape=(jax.ShapeDtypeStruct((WIDTH, s), F32),
                   jax.ShapeDtypeStruct((N_HEADS, HEAD_DIM, LANES), F32),
                   jax.ShapeDtypeStruct((N_HEADS, SUBLANES, LANES), F32),
                   jax.ShapeDtypeStruct((N_HEADS, 1, LANES), F32)),
        compiler_params=_params(("arbitrary",)),
        name="mlstm_prompt",
    )(mq, mk, mvt, mot, gcol, grow, g_ml.reshape(WIDTH, 1))
    half = lambda a: jnp.stack([a[h, ..., (h % 2) * HEAD_DIM:(h % 2 + 1) * HEAD_DIM] for h in range(N_HEADS)])
    return yt, jnp.swapaxes(half(ct), 1, 2), half(n)[:, 0, :], m[:, 0, 0]


_MLSTM_SAMPLE_PER_STEP = 4


def _mlstm_sample_kernel(q_ref, k_ref, v_ref, o_ref, ig_ref, lf_ref, m_ref, c_ref, n_ref, gml_ref,
                         y_ref, cn_ref, nn_ref, mn_ref):
    r = lax.broadcasted_iota(jnp.int32, (HEAD_DIM, HEAD_DIM), 0)
    c = lax.broadcasted_iota(jnp.int32, (HEAD_DIM, HEAD_DIM), 1)
    eye = r == c

    def column(row):
        return jnp.sum(jnp.where(eye, jnp.broadcast_to(row, (HEAD_DIM, HEAD_DIM)), 0.0), axis=1, keepdims=True)

    for i in range(q_ref.shape[0]):
        for h in range(N_HEADS):
            q = q_ref[i, h:h + 1, :]
            k = k_ref[i, h:h + 1, :]
            v = v_ref[i, h:h + 1, :]
            ig = ig_ref[i, h:h + 1, 0:1]
            lf = lf_ref[i, h:h + 1, 0:1]
            m_prev = m_ref[i, h:h + 1, 0:1]
            c_prev = c_ref[i, h]
            n_prev = n_ref[i, h:h + 1, :]
            b = lf + m_prev
            m_t = jnp.maximum(b, ig)
            w_intra = jnp.exp(ig - m_t)
            w_inter = jnp.exp(b - m_t)
            scores = jnp.sum(q * k, axis=1, keepdims=True) * w_intra
            qc = jnp.sum(column(q) * c_prev, axis=0, keepdims=True)
            num = w_inter * qc + scores * v
            den = w_inter * jnp.sum(q * n_prev, axis=1, keepdims=True) + scores
            hh = num / jnp.maximum(jnp.abs(den), jnp.exp(-m_t))
            cn_ref[i, h] = w_inter * c_prev + w_intra * (column(k) * v)
            nn_ref[i, h:h + 1, :] = w_inter * n_prev + w_intra * k
            mn_ref[i, h:h + 1, :] = jnp.broadcast_to(m_t, (1, LANES))
            y_ref[i, h:h + 1, :] = _gated_head_norm(hh, o_ref[i, h:h + 1, :], gml_ref[h:h + 1, :])


def _mlstm_sample(mq, mk, mv, mo, ig, lf, state_c, state_n, state_m, g_ml):
    nb = mq.shape[0]
    heads = lambda a: a.reshape(nb, N_HEADS, HEAD_DIM)
    lanes = lambda a: jnp.broadcast_to(a[:, :, None], (nb, N_HEADS, LANES))
    per = int(np.gcd(nb, _MLSTM_SAMPLE_PER_STEP))
    vec = pl.BlockSpec((per, N_HEADS, HEAD_DIM), lambda b: (b, 0, 0))
    sca = pl.BlockSpec((per, N_HEADS, LANES), lambda b: (b, 0, 0))
    mat = pl.BlockSpec((per, N_HEADS, HEAD_DIM, HEAD_DIM), lambda b: (b, 0, 0, 0))
    y, cn, nn, mn = pl.pallas_call(
        _mlstm_sample_kernel,
        grid=(nb // per,),
        in_specs=[vec, vec, vec, vec, sca, sca, sca, mat, vec,
                  pl.BlockSpec((N_HEADS, HEAD_DIM), lambda b: (0, 0))],
        out_specs=(vec, mat, vec, sca),
        out_shape=(jax.ShapeDtypeStruct((nb, N_HEADS, HEAD_DIM), F32),
                   jax.ShapeDtypeStruct((nb, N_HEADS, HEAD_DIM, HEAD_DIM), F32),
                   jax.ShapeDtypeStruct((nb, N_HEADS, HEAD_DIM), F32),
                   jax.ShapeDtypeStruct((nb, N_HEADS, LANES), F32)),
        compiler_params=_params(("arbitrary",)),
        name="mlstm_sample",
    )(heads(mq), heads(mk), heads(mv), heads(mo), lanes(ig), lanes(lf), lanes(state_m), state_c, state_n,
      g_ml.reshape(N_HEADS, HEAD_DIM))
    return y.reshape(nb, WIDTH), cn, nn, mn[:, :, 0]


_R_EXPERT, _R_GROUP = 0, N_EXPERTS


def _outproj_router_kernel(yf_ref, yml_ref, x_ref, wf_ref, wm_ref, gfox_ref, gffn_ref, wr_ref, br_ref,
                           x1_ref, xn_ref, comb_ref, *, prompt):
    yf = yf_ref[...]
    if prompt:
        ms = jnp.mean(yf * yf, axis=0, keepdims=True)
        yfn = (yf * lax.rsqrt(ms + RMS_EPS) * gfox_ref[...]).astype(BF16)
        y = _mm_tn(yfn, wf_ref[...]) + _mm_tn(yml_ref[...].astype(BF16), wm_ref[...])
    else:
        y = _mm_f32(_rms(yf, gfox_ref[...]), wf_ref[...]) + _mm_f32(yml_ref[...], wm_ref[...])
    x1 = x_ref[...] + y
    x1_ref[...] = x1
    xn = _rms(x1, gffn_ref[...])
    xb = xn.astype(BF16)
    xn_ref[...] = xb

    router = _mm(xb, wr_ref[...]) if prompt else _mm_f32(xn, wr_ref[...])
    logits = router + br_ref[...]
    lane = lax.broadcasted_iota(jnp.int32, logits.shape, 1)
    big = jnp.int32(2 * LANES)

    def first_argmax(vals):
        top = jnp.max(vals, axis=1, keepdims=True)
        idx = jnp.min(jnp.where(vals == top, lane, big), axis=1, keepdims=True)
        return top, idx

    is_group = (lane >= _R_GROUP) & (lane < _R_GROUP + N_GROUPS)
    lg = jnp.where(is_group, logits, NEG_INF)
    lg_top, lg_idx = first_argmax(lg)
    gate_g = 1.0 / jnp.sum(jnp.exp(lg - lg_top), axis=1, keepdims=True)
    grp = lg_idx - _R_GROUP
    in_grp = (lane >= grp * EXPERTS_PER_GROUP) & (lane < (grp + 1) * EXPERTS_PER_GROUP)
    le = jnp.where(in_grp, logits, NEG_INF)
    top1, idx1 = first_argmax(le)
    top2, idx2 = first_argmax(jnp.where(lane == idx1, NEG_INF, le))
    e2 = jnp.exp(top2 - top1)
    w1 = gate_g / (1.0 + e2)
    w2 = gate_g * e2 / (1.0 + e2)
    comb_ref[...] = jnp.where(lane == idx1, w1, 0.0) + jnp.where(lane == idx2, w2, 0.0)


def _outproj_router(yf, yml, x, w_out, g_fox, g_ffn, w_rg, b_rg, w_re, b_re, tm, prompt):
    n, d = x.shape
    wdt = BF16 if prompt else F32
    wf = w_out[0:WIDTH].astype(wdt)
    wm = w_out[WIDTH:2 * WIDTH].astype(wdt)
    pad = LANES - N_EXPERTS - N_GROUPS
    wr = jnp.concatenate([w_re, w_rg, jnp.zeros((d, pad), F32)], axis=1).astype(wdt)
    br = jnp.concatenate([b_re, b_rg, jnp.zeros((pad,), F32)]).reshape(1, LANES)
    full = lambda shape: pl.BlockSpec(shape, lambda i: (0,) * len(shape))
    rows = lambda width: pl.BlockSpec((tm, width), lambda i: (i, 0))
    if prompt:
        yf_spec = pl.BlockSpec((WIDTH, tm), lambda i: (0, i))
        gfox = g_fox.reshape(WIDTH, 1)
    else:
        yf_spec = rows(WIDTH)
        gfox = g_fox.reshape(1, WIDTH)
    return pl.pallas_call(
        functools.partial(_outproj_router_kernel, prompt=prompt),
        grid=(n // tm,),
        in_specs=[yf_spec, yf_spec, rows(d), full((WIDTH, d)), full((WIDTH, d)), full(gfox.shape),
                  full((1, d)), full((d, LANES)), full((1, LANES))],
        out_specs=(rows(d), rows(d), rows(LANES)),
        out_shape=(jax.ShapeDtypeStruct((n, d), F32), jax.ShapeDtypeStruct((n, d), BF16),
                   jax.ShapeDtypeStruct((n, LANES), F32)),
        compiler_params=_params(("arbitrary",)),
        name="outproj_router_prompt" if prompt else "outproj_router_sample",
    )(yf, yml, x, wf, wm, gfox, g_ffn.reshape(1, d), wr, br)


_MOE_EXPERTS_PER_STEP = 2


def _moe_kernel(xn_ref, comb_ref, x1_ref, wg_ref, wu_ref, wd_ref, gfin_ref, y_ref, acc_s):
    g = pl.program_id(1)
    per_step = wg_ref.shape[0]

    @pl.when(g == 0)
    def _():
        acc_s[...] = jnp.zeros(acc_s.shape, F32)

    xb = xn_ref[...]
    comb = comb_ref[...]
    lane = lax.broadcasted_iota(jnp.int32, comb.shape, 1)
    gates = [_mm(xb, wg_ref[j]) for j in range(per_step)]
    ups = [_mm(xb, wu_ref[j]) for j in range(per_step)]
    for j in range(per_step):
        he = gates[j] * jax.nn.sigmoid(gates[j]) * ups[j]
        out = _mm(he.astype(BF16), wd_ref[j])
        w_e = jnp.sum(jnp.where(lane == g * per_step + j, comb, 0.0), axis=1, keepdims=True)
        acc_s[...] += w_e * out

    @pl.when(g == pl.num_programs(1) - 1)
    def _():
        y_ref[...] = _rms(x1_ref[...] + acc_s[...], gfin_ref[...])


def _moe(xn, comb, x1, wg, wu, wd, g_final, tm):
    n, d = x1.shape
    de = wg.shape[2]
    eps = _MOE_EXPERTS_PER_STEP
    rows = lambda width: pl.BlockSpec((tm, width), lambda i, e: (i, 0))
    return pl.pallas_call(
        _moe_kernel,
        grid=(n // tm, N_EXPERTS // eps),
        in_specs=[rows(d), rows(LANES), rows(d),
                  pl.BlockSpec((eps, d, de), lambda i, e: (e, 0, 0)),
                  pl.BlockSpec((eps, d, de), lambda i, e: (e, 0, 0)),
                  pl.BlockSpec((eps, de, d), lambda i, e: (e, 0, 0)),
                  pl.BlockSpec((1, d), lambda i, e: (0, 0))],
        out_specs=rows(d),
        out_shape=jax.ShapeDtypeStruct((n, d), F32),
        scratch_shapes=[pltpu.VMEM((tm, d), F32)],
        compiler_params=_params(("arbitrary", "arbitrary")),
        name="moe",
    )(xn, comb, x1, wg, wu, wd, g_final.reshape(1, d))


def kernel(x_prompt, x_sample, cache_k, cache_v, cache_logf, state_conv, state_C, state_n, state_m,
           page_table, g_mix, w_in, b_fox_f, b_ml_i, b_ml_f, w_conv, b_conv, g_fox_out, g_ml_out,
           w_out, g_ffn, w_router_group, b_router_group, w_router_expert, b_router_expert,
           w_exp_gate, w_exp_up, w_exp_down, g_final):
    depth = w_in.shape[0]
    batch, seq, d = x_prompt.shape
    nb, dec_seq, _ = x_sample.shape
    assert depth == 1 and batch == 1 and dec_seq == 1
    l = 0
    gate_bias = _gate_bias(b_fox_f[l], b_ml_i[l], b_ml_f[l])
    wg, wu, wd = (w[l].astype(BF16) for w in (w_exp_gate, w_exp_up, w_exp_down))
    router = (w_router_group[l], b_router_group[l], w_router_expert[l], b_router_expert[l])

    t_attn = min(512, seq)
    (qa, ka, k_p, v_p, vt, gcol, grow, mq, mk, mvt, mot, tail) = _inproj_prompt(
        x_prompt[0], g_mix[l], w_in[l], gate_bias, w_conv[l], b_conv[l], tm=t_attn)
    y_fox_t = _fox_prompt(qa, ka, vt, tq=min(4096, seq))
    y_ml_t, c_p, n_p, m_p = _mlstm_prompt(mq, mk, mvt, mot, gcol, grow, g_ml_out[l])
    x1, xn, comb = _outproj_router(y_fox_t, y_ml_t, x_prompt[0], w_out[l], g_fox_out[l], g_ffn[l], *router,
                                   tm=min(512, seq), prompt=True)
    y_prompt = _moe(xn, comb, x1, wg, wu, wd, g_final, tm=min(1024, seq))

    xs = x_sample[:, 0, :]
    (q_s, k_s, v_s, g_s, mq_s, mk_s, mv_s, mo_s, conv_new) = _inproj_sample(
        xs, g_mix[l], w_in[l], gate_bias, w_conv[l], b_conv[l], jnp.transpose(state_conv[l], (1, 0, 2)))
    logf_s = g_s[:, _G_LOGF:_G_LOGF + N_HEADS]
    y_fox_s = _fox_sample(q_s, k_s, v_s, logf_s, cache_k[l:l + 1], cache_v[l:l + 1], cache_logf[l:l + 1],
                          page_table)
    y_ml_s, c_s, n_s, m_s = _mlstm_sample(mq_s, mk_s, mv_s, mo_s, g_s[:, _G_IG:_G_IG + N_HEADS],
                                          g_s[:, _G_LF:_G_LF + N_HEADS], state_C[l], state_n[l], state_m[l],
                                          g_ml_out[l])
    x1_s, xn_s, comb_s = _outproj_router(y_fox_s, y_ml_s, xs, w_out[l], g_fox_out[l], g_ffn[l], *router,
                                         tm=nb, prompt=False)
    y_sample = _moe(xn_s, comb_s, x1_s, wg, wu, wd, g_final, tm=nb)

    heads = lambda a, n: a.reshape(1, n, -1, N_HEADS, HEAD_DIM)
    return (
        y_prompt[None], y_sample[:, None, :],
        heads(k_p, 1), heads(v_p, 1), gcol[:, _G_LOGF:_G_LOGF + N_HEADS].reshape(1, 1, seq, N_HEADS),
        tail[SUBLANES - (CONV_WIDTH - 1):][None, None],
        c_p[None, None], n_p[None, None], m_p[None, None],
        heads(k_s, nb), heads(v_s, nb), logf_s.reshape(1, nb, 1, N_HEADS),
        jnp.transpose(conv_new, (1, 0, 2))[None],
        c_s[None], n_s[None], m_s[None],
    )
```

```python
import functools

import numpy as np
import jax
import jax.numpy as jnp
from jax import lax
from jax.experimental import pallas as pl
from jax.experimental.pallas import tpu as pltpu

HEAD_DIM = 64
N_HEADS = 8
WIDTH = N_HEADS * HEAD_DIM
CONV_WIDTH = 4
MLSTM_CHUNK = 128
N_GROUPS = 4
EXPERTS_PER_GROUP = 8
N_EXPERTS = N_GROUPS * EXPERTS_PER_GROUP
RMS_EPS = 1e-6
NEG_INF = -1e30
QK_SCALE = HEAD_DIM ** -0.5
LOG2E = 1.4426950408889634

LANES = 128
SUBLANES = 8
VMEM_LIMIT = 56 * 1024 * 1024

F32 = jnp.float32
BF16 = jnp.bfloat16


def _mm(a, b):
    return jnp.dot(a, b, preferred_element_type=F32)


def _mm_f32(a, b):
    return jnp.dot(a, b, preferred_element_type=F32, precision=lax.Precision.HIGHEST)


def _mm_nt(a, b):
    return lax.dot_general(a, b, (((1,), (1,)), ((), ())), preferred_element_type=F32)


def _mm_tn(a, b):
    return lax.dot_general(a, b, (((0,), (0,)), ((), ())), preferred_element_type=F32)


def _split3(a):
    hi = a.astype(BF16)
    r = a - hi.astype(F32)
    mid = r.astype(BF16)
    lo = (r - mid.astype(F32)).astype(BF16)
    return hi, mid, lo


def _mm3_right(a, b01):
    hi, mid, lo = _split3(a)
    return _mm(hi, b01) + _mm(mid, b01) + _mm(lo, b01)


def _mm3_left(a01, b):
    hi, mid, lo = _split3(b)
    return _mm(a01, hi) + _mm(a01, mid) + _mm(a01, lo)


def _log_sigmoid(x):
    return jnp.minimum(x, 0.0) - jnp.log1p(jnp.exp(-jnp.abs(x)))


def _rms(x, g):
    return x * lax.rsqrt(jnp.mean(x * x, axis=-1, keepdims=True) + RMS_EPS) * g


def _params(sem):
    return pltpu.CompilerParams(dimension_semantics=sem, vmem_limit_bytes=VMEM_LIMIT)


_C_Q, _C_K, _C_KRAW, _C_V, _C_MQK, _C_G, _C_END = 0, 1024, 2048, 2560, 3072, 4096, 4224
_T_V, _T_G, _T_MV, _T_MO, _T_END = 0, 512, 544, 1056, 1568
_G_LOGF, _G_IG, _G_LF, _G_CUM = 0, 8, 16, 24
_AUG = HEAD_DIM


def _gate_tile(z, lane_axis):
    idx = lax.broadcasted_iota(jnp.int32, z.shape, lane_axis)
    is_ig = (idx >= _G_IG) & (idx < _G_LF)
    return jnp.where(is_ig, z, _log_sigmoid(z))


def _inproj_prompt_kernel(x_ref, gmix_ref, w_ref, wt_ref, brow_ref, bcol_ref, wconv_ref, bconv_ref,
                          p2q_ref, p2k_ref,
                          qa_ref, ka_ref, k_ref, v_ref, vt_ref, gcol_ref, grow_ref,
                          mq_ref, mk_ref, mvt_ref, mot_ref, tail_ref,
                          conv_s, ccol_s, crow_s, *, tm):
    i = pl.program_id(0)

    @pl.when(i == 0)
    def _():
        conv_s[0:SUBLANES, :] = jnp.zeros((SUBLANES, conv_s.shape[1]), F32)
        ccol_s[...] = jnp.zeros(ccol_s.shape, F32)
        crow_s[...] = jnp.zeros(crow_s.shape, F32)

    xb = _rms(x_ref[...], gmix_ref[...]).astype(BF16)

    lane = lax.broadcasted_iota(jnp.int32, (tm, LANES), 1)
    g = _gate_tile(_mm(xb, w_ref[:, _C_G:_C_END]) + brow_ref[...], 1)
    zt = _mm_nt(wt_ref[...], xb)
    for h in range(N_HEADS):
        vt_ref[h, 0, 0:HEAD_DIM, :] = zt[h * HEAD_DIM:(h + 1) * HEAD_DIM].astype(BF16)
        vt_ref[h, 0, HEAD_DIM:_VT_ROWS, :] = jnp.ones((_VT_ROWS - HEAD_DIM, tm), BF16)
    mvt_ref[...] = zt[_T_MV:_T_MO].astype(BF16)
    mot_ref[...] = zt[_T_MO:_T_END]
    gt = _gate_tile(zt[_T_G:_T_MV] + bcol_ref[...], 0)

    r = lax.broadcasted_iota(jnp.int32, (tm, tm), 0)
    c = lax.broadcasted_iota(jnp.int32, (tm, tm), 1)
    ltri = (c <= r).astype(BF16)
    utri = (r <= c).astype(BF16)
    cs = _mm3_left(ltri, g) + ccol_s[...]
    ccol_s[...] = cs[tm - 1:tm, :]
    cst = _mm3_right(gt[0:8], utri) + crow_s[...][:, 0:1]
    crow_s[...] = jnp.broadcast_to(cst[:, tm - 1:tm], crow_s.shape)
    in_cum = (lane >= _G_CUM) & (lane < _G_CUM + 8)
    gcol_ref[...] = jnp.where(in_cum, pltpu.roll(cs, _G_CUM, 1), g)
    grow_ref[...] = jnp.concatenate([gt[0:24], cst], axis=0)

    hi, mid, lo = _split3(cs * LOG2E)
    caug = jnp.where(lane < 8, hi.astype(F32),
                     jnp.where(lane < 16, pltpu.roll(mid.astype(F32), 8, 1),
                               jnp.where(lane < 24, pltpu.roll(lo.astype(F32), 16, 1),
                                         jnp.where(lane == 24, 1.0, 0.0)))).astype(BF16)
    zq = _mm(xb, w_ref[:, _C_Q:_C_K]) * (QK_SCALE * LOG2E) + _mm(caug, p2q_ref[...])
    zk = _mm(xb, w_ref[:, _C_K:_C_KRAW]) + _mm(caug, p2k_ref[...])
    for h in range(N_HEADS):
        qa_ref[h] = zq[:, h * LANES:(h + 1) * LANES].astype(BF16)
        ka_ref[h] = zk[:, h * LANES:(h + 1) * LANES].astype(BF16)

    k_ref[...] = _mm(xb, w_ref[:, _C_KRAW:_C_V])
    v_ref[...] = _mm(xb, w_ref[:, _C_V:_C_MQK])

    u = _mm(xb, w_ref[:, _C_MQK:_C_G])
    conv_s[SUBLANES:SUBLANES + tm, :] = u
    acc = bconv_ref[...] + u * wconv_ref[CONV_WIDTH - 1:CONV_WIDTH, :]
    for j in range(CONV_WIDTH - 1):
        back = CONV_WIDTH - 1 - j
        acc = acc + conv_s[SUBLANES - back:SUBLANES - back + tm, :] * wconv_ref[j:j + 1, :]
    qk = acc * jax.nn.sigmoid(acc)
    mq_ref[...] = qk[:, 0:WIDTH]
    mk_ref[...] = qk[:, WIDTH:2 * WIDTH] * QK_SCALE
    tail = conv_s[tm:tm + SUBLANES, :]
    tail_ref[...] = tail
    conv_s[0:SUBLANES, :] = tail


def _aug_placement():
    p2q = np.zeros((LANES, N_HEADS * LANES), np.float32)
    p2k = np.zeros((LANES, N_HEADS * LANES), np.float32)
    for h in range(N_HEADS):
        base = h * LANES + _AUG
        for part in range(3):
            p2q[part * 8 + h, base + part] = 1.0
            p2q[24, base + 3 + part] = 1.0
            p2k[24, base + part] = 1.0
            p2k[part * 8 + h, base + 3 + part] = -1.0
    return jnp.asarray(p2q, BF16), jnp.asarray(p2k, BF16)


def _split_w_in(w):
    d = w.shape[0]
    sizes = [WIDTH, WIDTH, WIDTH, N_HEADS, 2 * WIDTH, WIDTH, N_HEADS, N_HEADS, WIDTH]
    pts = np.cumsum([0] + sizes)
    parts = [w[:, pts[j]:pts[j + 1]] for j in range(len(sizes))]
    fq, fk, fv, ff, mqk, mv, mi, mf, mo = parts
    gates = jnp.concatenate([ff, mi, mf, jnp.zeros((d, LANES - 3 * N_HEADS), w.dtype)], axis=1)
    return fq, fk, fv, mqk, mv, mo, gates


def _gate_bias(b_fox_f, b_ml_i, b_ml_f):
    return jnp.concatenate([b_fox_f, b_ml_i, b_ml_f, jnp.zeros((LANES - 3 * N_HEADS,), F32)])


def _pad_heads(w):
    d = w.shape[0]
    w3 = w.reshape(d, N_HEADS, HEAD_DIM)
    return jnp.pad(w3, ((0, 0), (0, 0), (0, LANES - HEAD_DIM))).reshape(d, N_HEADS * LANES)


def _inproj_prompt(x, g_mix, w_in, gate_bias, w_conv, b_conv, tm=256):
    s, d = x.shape
    fq, fk, fv, mqk, mv, mo, gates = _split_w_in(w_in)
    w_all = jnp.concatenate([_pad_heads(fq), _pad_heads(fk), fk, fv, mqk, gates], axis=1).astype(BF16)
    w_t = jnp.concatenate([fv.T, gates[:, 0:32].T, mv.T, mo.T], axis=0).astype(BF16)
    p2q, p2k = _aug_placement()
    nblk = s // tm
    full = lambda shape: pl.BlockSpec(shape, lambda i: (0,) * len(shape))
    rows = lambda width: pl.BlockSpec((tm, width), lambda i: (i, 0))
    out_shape = (
        jax.ShapeDtypeStruct((N_HEADS, s, LANES), BF16),
        jax.ShapeDtypeStruct((N_HEADS, s, LANES), BF16),
        jax.ShapeDtypeStruct((s, WIDTH), F32),
        jax.ShapeDtypeStruct((s, WIDTH), F32),
        jax.ShapeDtypeStruct((N_HEADS, nblk, _VT_ROWS, tm), BF16),
        jax.ShapeDtypeStruct((s, LANES), F32),
        jax.ShapeDtypeStruct((32, s), F32),
        jax.ShapeDtypeStruct((s, WIDTH), F32),
        jax.ShapeDtypeStruct((s, WIDTH), F32),
        jax.ShapeDtypeStruct((WIDTH, s), BF16),
        jax.ShapeDtypeStruct((WIDTH, s), F32),
        jax.ShapeDtypeStruct((SUBLANES, 2 * WIDTH), F32),
    )
    out_specs = (
        pl.BlockSpec((N_HEADS, tm, LANES), lambda i: (0, i, 0)),
        pl.BlockSpec((N_HEADS, tm, LANES), lambda i: (0, i, 0)),
        rows(WIDTH), rows(WIDTH),
        pl.BlockSpec((N_HEADS, 1, _VT_ROWS, tm), lambda i: (0, i, 0, 0)),
        rows(LANES),
        pl.BlockSpec((32, tm), lambda i: (0, i)),
        rows(WIDTH), rows(WIDTH),
        pl.BlockSpec((WIDTH, tm), lambda i: (0, i)), pl.BlockSpec((WIDTH, tm), lambda i: (0, i)),
        full((SUBLANES, 2 * WIDTH)),
    )
    return pl.pallas_call(
        functools.partial(_inproj_prompt_kernel, tm=tm),
        grid=(nblk,),
        in_specs=[rows(d), full((1, d)), full(w_all.shape), full(w_t.shape), full((1, LANES)),
                  full((32, 1)), full((CONV_WIDTH, 2 * WIDTH)), full((1, 2 * WIDTH)),
                  full(p2q.shape), full(p2k.shape)],
        out_specs=out_specs,
        out_shape=out_shape,
        scratch_shapes=[pltpu.VMEM((tm + 2 * SUBLANES, 2 * WIDTH), F32),
                        pltpu.VMEM((1, LANES), F32),
                        pltpu.VMEM((SUBLANES, LANES), F32)],
        compiler_params=_params(("arbitrary",)),
        name="inproj_prompt",
    )(x, g_mix.reshape(1, d), w_all, w_t, gate_bias.reshape(1, LANES), gate_bias[0:32].reshape(32, 1),
      w_conv, b_conv.reshape(1, 2 * WIDTH), p2q, p2k)


_S_Q, _S_K, _S_V, _S_MQK, _S_MV, _S_MO, _S_G, _S_END = 0, 512, 1024, 1536, 2560, 3072, 3584, 3712


def _inproj_sample_kernel(x_ref, gmix_ref, w_ref, brow_ref, wconv_ref, bconv_ref, cstate_ref,
                          q_ref, k_ref, v_ref, g_ref, mq_ref, mk_ref, mv_ref, mo_ref, cnew_ref):
    xn = _rms(x_ref[...], gmix_ref[...])
    q_ref[...] = _mm_f32(xn, w_ref[:, _S_Q:_S_K])
    k_ref[...] = _mm_f32(xn, w_ref[:, _S_K:_S_V])
    v_ref[...] = _mm_f32(xn, w_ref[:, _S_V:_S_MQK])
    mv_ref[...] = _mm_f32(xn, w_ref[:, _S_MV:_S_MO])
    mo_ref[...] = _mm_f32(xn, w_ref[:, _S_MO:_S_G])
    g_ref[...] = _gate_tile(_mm_f32(xn, w_ref[:, _S_G:_S_END]) + brow_ref[...], 1)
    u = _mm_f32(xn, w_ref[:, _S_MQK:_S_MV])
    acc = bconv_ref[...] + u * wconv_ref[CONV_WIDTH - 1:CONV_WIDTH, :]
    for j in range(CONV_WIDTH - 1):
        acc = acc + cstate_ref[j] * wconv_ref[j:j + 1, :]
    qk = acc * jax.nn.sigmoid(acc)
    mq_ref[...] = qk[:, 0:WIDTH]
    mk_ref[...] = qk[:, WIDTH:2 * WIDTH] * QK_SCALE
    for j in range(CONV_WIDTH - 2):
        cnew_ref[j] = cstate_ref[j + 1]
    cnew_ref[CONV_WIDTH - 2] = u


def _inproj_sample(x, g_mix, w_in, gate_bias, w_conv, b_conv, conv_state_t):
    n, d = x.shape
    fq, fk, fv, mqk, mv, mo, gates = _split_w_in(w_in)
    w_all = jnp.concatenate([fq, fk, fv, mqk, mv, mo, gates], axis=1)
    wide = jax.ShapeDtypeStruct((n, WIDTH), F32)
    out_shape = (wide, wide, wide, jax.ShapeDtypeStruct((n, LANES), F32), wide, wide, wide, wide,
                 jax.ShapeDtypeStruct((CONV_WIDTH - 1, n, 2 * WIDTH), F32))
    return pl.pallas_call(
        _inproj_sample_kernel,
        out_shape=out_shape,
        compiler_params=pltpu.CompilerParams(vmem_limit_bytes=VMEM_LIMIT),
        name="inproj_sample",
    )(x, g_mix.reshape(1, d), w_all, gate_bias.reshape(1, LANES), w_conv, b_conv.reshape(1, 2 * WIDTH),
      conv_state_t)


_ATTN_SUB = 256
_ATTN_AHEAD = 16
_ATTN_CHUNKS = 1
_VT_ROWS = HEAD_DIM + 16


def _fox_prompt_kernel(qa_ref, ka_ref, vt_ref, o_ref, *, tq, tk):
    qi = pl.program_id(1)
    nsub = tq // _ATTN_SUB
    per_q = tq // tk
    qs = [qa_ref[0, c * _ATTN_SUB:(c + 1) * _ATTN_SUB, :] for c in range(nsub)]

    def chunks(rows):
        step = max(rows // _ATTN_CHUNKS, SUBLANES * 2)
        return [(r0, min(r0 + step, rows)) for r0 in range(0, rows, step)]

    def col_max(s, span):
        return jnp.max(s[span[0]:span[1]], axis=0, keepdims=True)

    def finish(state, m_new, p, vblk):
        m, l, acc = state
        alpha = jnp.exp2(m - m_new)
        pv = _mm(vblk, p)
        return m_new, alpha * l + pv[HEAD_DIM:HEAD_DIM + 1], alpha * acc + pv[0:HEAD_DIM]

    def run_block(states, k_start, v_idx, plan):
        vblk = vt_ref[0, v_idx]

        def logits_of(c, rows, k_lo):
            s = _mm_nt(ka_ref[0, pl.ds(k_start, rows), :], qs[c])
            if k_lo is not None:
                kpos = lax.broadcasted_iota(jnp.int32, (rows, _ATTN_SUB), 0) + k_lo
                qpos = lax.broadcasted_iota(jnp.int32, (rows, _ATTN_SUB), 1) + c * _ATTN_SUB
                s = jnp.where(kpos <= qpos, s, NEG_INF)
            return s

        logits, maxes = {}, {}
        for i in range(min(_ATTN_AHEAD, len(plan))):
            c, rows, k_lo = plan[i]
            logits[i] = logits_of(c, rows, k_lo)
            maxes[i] = [col_max(logits[i], sp) for sp in chunks(rows)]
        for i, (c, rows, _) in enumerate(plan):
            m_new = functools.reduce(jnp.maximum, maxes.pop(i), states[c][0])
            nxt = i + _ATTN_AHEAD
            nxt_spans = []
            if nxt < len(plan):
                logits[nxt] = logits_of(*plan[nxt])
                maxes[nxt] = []
                nxt_spans = chunks(plan[nxt][1])
            s_cur = logits.pop(i)
            p_parts = []
            for n, sp in enumerate(chunks(rows)):
                p_parts.append(jnp.exp2(s_cur[sp[0]:sp[1]] - m_new).astype(BF16))
                if n < len(nxt_spans):
                    maxes[nxt].append(col_max(logits[nxt], nxt_spans[n]))
            for sp in nxt_spans[len(p_parts):]:
                maxes[nxt].append(col_max(logits[nxt], sp))
            states[c] = finish(states[c], m_new, jnp.concatenate(p_parts, axis=0), vblk[:, 0:rows])
        return states

    def body(j, carry):
        full = [(c, tk, None) for c in range(nsub)]
        return tuple(run_block(list(carry), pl.multiple_of(j * tk, tk), j, full))

    init = tuple((jnp.full((1, _ATTN_SUB), NEG_INF, F32), jnp.zeros((1, _ATTN_SUB), F32),
                  jnp.zeros((HEAD_DIM, _ATTN_SUB), F32)) for _ in range(nsub))
    states = list(lax.fori_loop(0, qi * per_q, body, init))

    for d in range(per_q):
        k_lo = d * tk
        plan = []
        for c in range(nsub):
            q_lo, q_hi = c * _ATTN_SUB, (c + 1) * _ATTN_SUB
            rows = min(q_hi - k_lo, tk)
            if rows > 0:
                plan.append((c, rows, k_lo if k_lo + rows > q_lo else None))
        states = run_block(states, pl.multiple_of(qi * tq + k_lo, tk), qi * per_q + d, plan)
    for c in range(nsub):
        m, l, acc = states[c]
        o_ref[:, c * _ATTN_SUB:(c + 1) * _ATTN_SUB] = acc / l


def _fox_prompt(qa, ka, vt, tq):
    _, s, _ = qa.shape
    _, nkv, _, tk = vt.shape
    assert tq % tk == 0 and tq % _ATTN_SUB == 0 and tk % _ATTN_SUB == 0
    return pl.pallas_call(
        functools.partial(_fox_prompt_kernel, tq=tq, tk=tk),
        grid=(N_HEADS, s // tq),
        in_specs=[pl.BlockSpec((1, tq, LANES), lambda h, i: (h, i, 0)),
                  pl.BlockSpec((1, s, LANES), lambda h, i: (h, 0, 0)),
                  pl.BlockSpec((1, nkv, _VT_ROWS, tk), lambda h, i: (h, 0, 0, 0))],
        out_specs=pl.BlockSpec((HEAD_DIM, tq), lambda h, i: (h, i)),
        out_shape=jax.ShapeDtypeStruct((WIDTH, s), F32),
        compiler_params=_params(("arbitrary", "arbitrary")),
        name="fox_prompt",
    )(qa, ka, vt)


_PAGES_PER_STEP = 16


def _fox_sample_kernel(pt_ref, qb_ref, vnb_ref, q_ref, kn_ref, lfn_ref, tsuf_ref, *rest, page, npp):
    del pt_ref
    k_refs = rest[0:npp]
    v_refs = rest[npp:2 * npp]
    lf_refs = rest[2 * npp:3 * npp]
    o_ref = rest[3 * npp]
    qs_s, m_s, l_s, acc_s, carry_s = rest[3 * npp + 1:]
    g = pl.program_id(1)

    @pl.when(g == 0)
    def _():
        qs_s[...] = qb_ref[0] * QK_SCALE
        s_self = jnp.sum(q_ref[0] * QK_SCALE * kn_ref[0], axis=1, keepdims=True)
        m_s[...] = jnp.broadcast_to(s_self, m_s.shape)
        l_s[...] = jnp.ones(l_s.shape, F32)
        lane = lax.broadcasted_iota(jnp.int32, acc_s.shape, 2)
        acc_s[...] = jnp.where(lane == 0, vnb_ref[0], 0.0)
        carry_s[...] = lfn_ref[0]

    lf_all = jnp.concatenate([lf_refs[r][...] for r in range(npp)], axis=0)
    suffix = _mm3_right(lf_all, tsuf_ref[...])
    page_sum = jnp.sum(lf_all, axis=1, keepdims=True)
    carry = carry_s[...][:, 0:1]
    scores = []
    for r in range(npp):
        rows = slice(r * N_HEADS, (r + 1) * N_HEADS)
        qk = jnp.concatenate(
            [jnp.sum(k_refs[r][h] * qs_s[h], axis=0, keepdims=True) for h in range(N_HEADS)], axis=0)
        scores.append(qk + suffix[rows] + carry)
        carry = carry + page_sum[rows]
    carry_s[...] = jnp.broadcast_to(carry, carry_s.shape)

    m = m_s[...][:, 0:1]
    m_new = m
    for s in scores:
        m_new = jnp.maximum(m_new, jnp.max(s, axis=1, keepdims=True))
    alpha = jnp.exp(m - m_new)
    probs = [jnp.exp(s - m_new) for s in scores]
    l_new = alpha * l_s[...][:, 0:1]
    for p in probs:
        l_new = l_new + jnp.sum(p, axis=1, keepdims=True)
    l_s[...] = jnp.broadcast_to(l_new, l_s.shape)
    m_s[...] = jnp.broadcast_to(m_new, m_s.shape)
    for h in range(N_HEADS):
        upd = alpha[h:h + 1, :] * acc_s[h]
        for r in range(npp):
            upd = upd + probs[r][h:h + 1, :] * v_refs[r][h]
        acc_s[h] = upd

    @pl.when(g == pl.num_programs(1) - 1)
    def _():
        rr = lax.broadcasted_iota(jnp.int32, (HEAD_DIM, HEAD_DIM), 0)
        cc = lax.broadcasted_iota(jnp.int32, (HEAD_DIM, HEAD_DIM), 1)
        for h in range(N_HEADS):
            col = jnp.sum(acc_s[h], axis=1, keepdims=True) / l_s[...][h:h + 1, 0:1]
            row = jnp.sum(jnp.where(rr == cc, jnp.broadcast_to(col, (HEAD_DIM, HEAD_DIM)), 0.0),
                          axis=0, keepdims=True)
            o_ref[0, h:h + 1, :] = row


def _fox_sample(q, k_new, v_new, logf_new, cache_k, cache_v, cache_logf, page_table):
    nb, n_pages = page_table.shape
    page = cache_k.shape[2]
    npp = _PAGES_PER_STEP
    assert page == LANES and n_pages % npp == 0
    kt = jnp.transpose(cache_k, (0, 1, 3, 4, 2))
    vt = jnp.transpose(cache_v, (0, 1, 3, 4, 2))
    lft = jnp.transpose(cache_logf, (0, 1, 3, 2))
    q3, kn3, vn3 = (a.reshape(nb, N_HEADS, HEAD_DIM) for a in (q, k_new, v_new))
    lanes = lambda a: jnp.broadcast_to(a[..., None], a.shape + (LANES,))
    tsuf = jnp.asarray(np.tril(np.ones((page, page), np.float32), -1), BF16)

    def page_map(r):
        return lambda b, g, pt: (0, pt[b, n_pages - 1 - (g * npp + r)], 0, 0, 0)

    def lf_map(r):
        return lambda b, g, pt: (0, pt[b, n_pages - 1 - (g * npp + r)], 0, 0)

    per_b = pl.BlockSpec((1, N_HEADS, HEAD_DIM), lambda b, g, pt: (b, 0, 0))
    per_b_lanes = pl.BlockSpec((1, N_HEADS, HEAD_DIM, LANES), lambda b, g, pt: (b, 0, 0, 0))
    kv_specs = [pl.BlockSpec((None, None, N_HEADS, HEAD_DIM, page), page_map(r)) for r in range(npp)]
    lf_specs = [pl.BlockSpec((None, None, N_HEADS, page), lf_map(r)) for r in range(npp)]
    stat = pltpu.VMEM((N_HEADS, LANES), F32)
    grid_spec = pltpu.PrefetchScalarGridSpec(
        num_scalar_prefetch=1,
        grid=(nb, n_pages // npp),
        in_specs=[per_b_lanes, per_b_lanes, per_b, per_b,
                  pl.BlockSpec((1, N_HEADS, LANES), lambda b, g, pt: (b, 0, 0)),
                  pl.BlockSpec((page, page), lambda b, g, pt: (0, 0))] + kv_specs + kv_specs + lf_specs,
        out_specs=per_b,
        scratch_shapes=[pltpu.VMEM((N_HEADS, HEAD_DIM, LANES), F32), stat, stat,
                        pltpu.VMEM((N_HEADS, HEAD_DIM, page), F32), stat],
    )
    out = pl.pallas_call(
        functools.partial(_fox_sample_kernel, page=page, npp=npp),
        grid_spec=grid_spec,
        out_shape=jax.ShapeDtypeStruct((nb, N_HEADS, HEAD_DIM), F32),
        compiler_params=_params(("arbitrary", "arbitrary")),
        name="fox_sample",
    )(page_table, lanes(q3), lanes(vn3), q3, kn3, lanes(logf_new), tsuf,
      *([kt] * npp), *([vt] * npp), *([lft] * npp))
    return out.reshape(nb, WIDTH)


def _gated_head_norm(h, o_pre, g):
    hg = h * jax.nn.sigmoid(o_pre)
    return hg * lax.rsqrt(jnp.mean(hg * hg, axis=-1, keepdims=True) + RMS_EPS) * g


def _mlstm_prompt_kernel(q_ref, k_ref, vt_ref, ot_ref, gcol_ref, grow_ref, gml_ref,
                         yt_ref, c_ref, n_ref, m_ref, *, chunk):
    @pl.when(pl.program_id(0) == 0)
    def _():
        c_ref[...] = jnp.zeros(c_ref.shape, F32)
        n_ref[...] = jnp.zeros(n_ref.shape, F32)
        m_ref[...] = jnp.zeros(m_ref.shape, F32)

    r = lax.broadcasted_iota(jnp.int32, (chunk, chunk), 0)
    c = lax.broadcasted_iota(jnp.int32, (chunk, chunk), 1)
    causal_t = r <= c
    gcol = gcol_ref[...]
    grow = grow_ref[...]
    a_col_all = _mm3_left((c <= r).astype(BF16), gcol)
    a_row_all = _mm3_right(grow[_G_LF:_G_LF + 8], causal_t.astype(BF16))
    lane = lax.broadcasted_iota(jnp.int32, (1, LANES), 1)

    heads = range(N_HEADS)
    own = [(lane >= (h % 2) * HEAD_DIM) & (lane < (h % 2 + 1) * HEAD_DIM) for h in heads]
    pair = [slice((h // 2) * LANES, (h // 2 + 1) * LANES) for h in heads]
    rows = [slice(h * HEAD_DIM, (h + 1) * HEAD_DIM) for h in heads]
    kb = [k_ref[:, pair[h]].astype(BF16) for h in range(0, N_HEADS, 2)]
    qb = [jnp.where(own[h], q_ref[:, pair[h]], 0.0).astype(BF16) for h in heads]
    c_prev = [c_ref[h] for h in heads]
    n_prev = [n_ref[h] for h in heads]
    m_prev = [m_ref[h][:, 0:1] for h in heads]
    st = [_mm_nt(kb[h // 2], qb[h]) for h in heads]
    ctq = [_mm_nt(c_prev[h].astype(BF16), qb[h]) for h in heads]
    qn = [_mm_nt(n_prev[h].astype(BF16), qb[h])[0:1] for h in heads]

    m_t, w_intra, w_inter, decay, m_new, vw, w_rows = [], [], [], [], [], [], []
    for h in heads:
        a_r = a_row_all[h:h + 1, :]
        ig_r = grow[_G_IG + h:_G_IG + h + 1, :]
        key_term = gcol[:, _G_IG + h:_G_IG + h + 1] - a_col_all[:, _G_LF + h:_G_LF + h + 1]
        d = jnp.where(causal_t, a_r + key_term, NEG_INF)
        b = a_r + m_prev[h]
        m_t.append(jnp.maximum(b, jnp.max(d, axis=0, keepdims=True)))
        w_intra.append(jnp.exp(d - m_t[h]))
        w_inter.append(jnp.exp(b - m_t[h]))
        m_new.append(m_t[h][:, chunk - 1:chunk])
        a_last = a_r[:, chunk - 1:chunk]
        decay.append(jnp.exp(a_last + m_prev[h] - m_new[h]))
        w_write = jnp.exp(a_last - a_r + ig_r - m_new[h])
        w_rows.append(jnp.broadcast_to(w_write, (SUBLANES, chunk)).astype(BF16))
        vw.append((vt_ref[rows[h], :].astype(F32) * w_write).astype(BF16))
    scores = [st[h] * w_intra[h] for h in heads]
    sv = [_mm(vt_ref[rows[h], :], scores[h].astype(BF16)) for h in heads]
    c_add = [_mm(vw[h], kb[h // 2]) for h in heads]
    n_add = [_mm(w_rows[h], kb[h // 2]) for h in heads]
    for h in heads:
        num = w_inter[h] * ctq[h] + sv[h]
        den = w_inter[h] * qn[h] + jnp.sum(scores[h], axis=0, keepdims=True)
        hh = num / jnp.maximum(jnp.abs(den), jnp.exp(-m_t[h]))
        c_ref[h] = decay[h] * c_prev[h] + jnp.where(own[h], c_add[h], 0.0)
        n_ref[h] = decay[h] * n_prev[h] + jnp.where(own[h], n_add[h], 0.0)
        m_ref[h] = jnp.broadcast_to(m_new[h], (1, LANES))
        hg = hh * jax.nn.sigmoid(ot_ref[rows[h], :])
        yt_ref[rows[h], :] = hg * lax.rsqrt(jnp.mean(hg * hg, axis=0, keepdims=True) + RMS_EPS) \
            * gml_ref[rows[h], :]


def _mlstm_prompt(mq, mk, mvt, mot, gcol, grow, g_ml):
    s = mq.shape[0]
    chunk = int(np.gcd(s, MLSTM_CHUNK))
    tok_rows = pl.BlockSpec((chunk, WIDTH), lambda i: (i, 0))
    tok_lanes = pl.BlockSpec((WIDTH, chunk), lambda i: (0, i))
    state = lambda shape: pl.BlockSpec(shape, lambda i: (0,) * len(shape))
    yt, ct, n, m = pl.pallas_call(
        functools.partial(_mlstm_prompt_kernel, chunk=chunk),
        grid=(s // chunk,),
        in_specs=[tok_rows, tok_rows, tok_lanes, tok_lanes, pl.BlockSpec((chunk, LANES), lambda i: (i, 0)),
                  pl.BlockSpec((32, chunk), lambda i: (0, i)), state((WIDTH, 1))],
        out_specs=(tok_lanes, state((N_HEADS, HEAD_DIM, LANES)), state((N_HEADS, SUBLANES, LANES)),
                   state((N_HEADS, 1, LANES))),
        out_shape=(jax.ShapeDtypeStruct((WIDTH, s), F32),
                   jax.ShapeDtypeStruct((N_HEADS, HEAD_DIM, LANES), F32),
                   jax.ShapeDtypeStruct((N_HEADS, SUBLANES, LANES), F32),
                   jax.ShapeDtypeStruct((N_HEADS, 1, LANES), F32)),
        compiler_params=_params(("arbitrary",)),
        name="mlstm_prompt",
    )(mq, mk, mvt, mot, gcol, grow, g_ml.reshape(WIDTH, 1))
    half = lambda a: jnp.stack([a[h, ..., (h % 2) * HEAD_DIM:(h % 2 + 1) * HEAD_DIM] for h in range(N_HEADS)])
    return yt, jnp.swapaxes(half(ct), 1, 2), half(n)[:, 0, :], m[:, 0, 0]


_MLSTM_SAMPLE_PER_STEP = 1


def _mlstm_sample_kernel(q_ref, k_ref, v_ref, o_ref, ig_ref, lf_ref, m_ref, c_ref, n_ref, gml_ref,
                         y_ref, cn_ref, nn_ref, mn_ref):
    r = lax.broadcasted_iota(jnp.int32, (HEAD_DIM, HEAD_DIM), 0)
    c = lax.broadcasted_iota(jnp.int32, (HEAD_DIM, HEAD_DIM), 1)
    eye = r == c

    def column(row):
        return jnp.sum(jnp.where(eye, jnp.broadcast_to(row, (HEAD_DIM, HEAD_DIM)), 0.0), axis=1, keepdims=True)

    for i in range(q_ref.shape[0]):
        for h in range(N_HEADS):
            q = q_ref[i, h:h + 1, :]
            k = k_ref[i, h:h + 1, :]
            v = v_ref[i, h:h + 1, :]
            ig = ig_ref[i, h:h + 1, 0:1]
            lf = lf_ref[i, h:h + 1, 0:1]
            m_prev = m_ref[i, h:h + 1, 0:1]
            c_prev = c_ref[i, h]
            n_prev = n_ref[i, h:h + 1, :]
            b = lf + m_prev
            m_t = jnp.maximum(b, ig)
            w_intra = jnp.exp(ig - m_t)
            w_inter = jnp.exp(b - m_t)
            scores = jnp.sum(q * k, axis=1, keepdims=True) * w_intra
            qc = jnp.sum(column(q) * c_prev, axis=0, keepdims=True)
            num = w_inter * qc + scores * v
            den = w_inter * jnp.sum(q * n_prev, axis=1, keepdims=True) + scores
            hh = num / jnp.maximum(jnp.abs(den), jnp.exp(-m_t))
            cn_ref[i, h] = w_inter * c_prev + w_intra * (column(k) * v)
            nn_ref[i, h:h + 1, :] = w_inter * n_prev + w_intra * k
            mn_ref[i, h:h + 1, :] = jnp.broadcast_to(m_t, (1, LANES))
            y_ref[i, h:h + 1, :] = _gated_head_norm(hh, o_ref[i, h:h + 1, :], gml_ref[h:h + 1, :])


def _mlstm_sample(mq, mk, mv, mo, ig, lf, state_c, state_n, state_m, g_ml):
    nb = mq.shape[0]
    heads = lambda a: a.reshape(nb, N_HEADS, HEAD_DIM)
    lanes = lambda a: jnp.broadcast_to(a[:, :, None], (nb, N_HEADS, LANES))
    per = int(np.gcd(nb, _MLSTM_SAMPLE_PER_STEP))
    vec = pl.BlockSpec((per, N_HEADS, HEAD_DIM), lambda b: (b, 0, 0))
    sca = pl.BlockSpec((per, N_HEADS, LANES), lambda b: (b, 0, 0))
    mat = pl.BlockSpec((per, N_HEADS, HEAD_DIM, HEAD_DIM), lambda b: (b, 0, 0, 0))
    y, cn, nn, mn = pl.pallas_call(
        _mlstm_sample_kernel,
        grid=(nb // per,),
        in_specs=[vec, vec, vec, vec, sca, sca, sca, mat, vec,
                  pl.BlockSpec((N_HEADS, HEAD_DIM), lambda b: (0, 0))],
        out_specs=(vec, mat, vec, sca),
        out_shape=(jax.ShapeDtypeStruct((nb, N_HEADS, HEAD_DIM), F32),
                   jax.ShapeDtypeStruct((nb, N_HEADS, HEAD_DIM, HEAD_DIM), F32),
                   jax.ShapeDtypeStruct((nb, N_HEADS, HEAD_DIM), F32),
                   jax.ShapeDtypeStruct((nb, N_HEADS, LANES), F32)),
        compiler_params=_params(("arbitrary",)),
        name="mlstm_sample",
    )(heads(mq), heads(mk), heads(mv), heads(mo), lanes(ig), lanes(lf), lanes(state_m), state_c, state_n,
      g_ml.reshape(N_HEADS, HEAD_DIM))
    return y.reshape(nb, WIDTH), cn, nn, mn[:, :, 0]


_R_EXPERT, _R_GROUP = 0, N_EXPERTS


def _outproj_router_kernel(yf_ref, yml_ref, x_ref, wf_ref, wm_ref, gfox_ref, gffn_ref, wr_ref, br_ref,
                           x1_ref, xn_ref, comb_ref, *, prompt):
    yf = yf_ref[...]
    if prompt:
        ms = jnp.mean(yf * yf, axis=0, keepdims=True)
        yfn = (yf * lax.rsqrt(ms + RMS_EPS) * gfox_ref[...]).astype(BF16)
        y = _mm_tn(yfn, wf_ref[...]) + _mm_tn(yml_ref[...].astype(BF16), wm_ref[...])
    else:
        y = _mm_f32(_rms(yf, gfox_ref[...]), wf_ref[...]) + _mm_f32(yml_ref[...], wm_ref[...])
    x1 = x_ref[...] + y
    x1_ref[...] = x1
    xn = _rms(x1, gffn_ref[...])
    xb = xn.astype(BF16)
    xn_ref[...] = xb

    router = _mm(xb, wr_ref[...]) if prompt else _mm_f32(xn, wr_ref[...])
    logits = router + br_ref[...]
    lane = lax.broadcasted_iota(jnp.int32, logits.shape, 1)
    big = jnp.int32(2 * LANES)

    def first_argmax(vals):
        top = jnp.max(vals, axis=1, keepdims=True)
        idx = jnp.min(jnp.where(vals == top, lane, big), axis=1, keepdims=True)
        return top, idx

    is_group = (lane >= _R_GROUP) & (lane < _R_GROUP + N_GROUPS)
    lg = jnp.where(is_group, logits, NEG_INF)
    lg_top, lg_idx = first_argmax(lg)
    gate_g = 1.0 / jnp.sum(jnp.exp(lg - lg_top), axis=1, keepdims=True)
    grp = lg_idx - _R_GROUP
    in_grp = (lane >= grp * EXPERTS_PER_GROUP) & (lane < (grp + 1) * EXPERTS_PER_GROUP)
    le = jnp.where(in_grp, logits, NEG_INF)
    top1, idx1 = first_argmax(le)
    top2, idx2 = first_argmax(jnp.where(lane == idx1, NEG_INF, le))
    e2 = jnp.exp(top2 - top1)
    w1 = gate_g / (1.0 + e2)
    w2 = gate_g * e2 / (1.0 + e2)
    comb_ref[...] = jnp.where(lane == idx1, w1, 0.0) + jnp.where(lane == idx2, w2, 0.0)


def _outproj_router(yf, yml, x, w_out, g_fox, g_ffn, w_rg, b_rg, w_re, b_re, tm, prompt):
    n, d = x.shape
    wdt = BF16 if prompt else F32
    wf = w_out[0:WIDTH].astype(wdt)
    wm = w_out[WIDTH:2 * WIDTH].astype(wdt)
    pad = LANES - N_EXPERTS - N_GROUPS
    wr = jnp.concatenate([w_re, w_rg, jnp.zeros((d, pad), F32)], axis=1).astype(wdt)
    br = jnp.concatenate([b_re, b_rg, jnp.zeros((pad,), F32)]).reshape(1, LANES)
    full = lambda shape: pl.BlockSpec(shape, lambda i: (0,) * len(shape))
    rows = lambda width: pl.BlockSpec((tm, width), lambda i: (i, 0))
    if prompt:
        yf_spec = pl.BlockSpec((WIDTH, tm), lambda i: (0, i))
        gfox = g_fox.reshape(WIDTH, 1)
    else:
        yf_spec = rows(WIDTH)
        gfox = g_fox.reshape(1, WIDTH)
    return pl.pallas_call(
        functools.partial(_outproj_router_kernel, prompt=prompt),
        grid=(n // tm,),
        in_specs=[yf_spec, yf_spec, rows(d), full((WIDTH, d)), full((WIDTH, d)), full(gfox.shape),
                  full((1, d)), full((d, LANES)), full((1, LANES))],
        out_specs=(rows(d), rows(d), rows(LANES)),
        out_shape=(jax.ShapeDtypeStruct((n, d), F32), jax.ShapeDtypeStruct((n, d), BF16),
                   jax.ShapeDtypeStruct((n, LANES), F32)),
        compiler_params=_params(("arbitrary",)),
        name="outproj_router_prompt" if prompt else "outproj_router_sample",
    )(yf, yml, x, wf, wm, gfox, g_ffn.reshape(1, d), wr, br)


_MOE_EXPERTS_PER_STEP = 2


def _moe_kernel(xn_ref, comb_ref, x1_ref, wg_ref, wu_ref, wd_ref, gfin_ref, y_ref, acc_s):
    g = pl.program_id(1)
    per_step = wg_ref.shape[0]

    @pl.when(g == 0)
    def _():
        acc_s[...] = jnp.zeros(acc_s.shape, F32)

    xb = xn_ref[...]
    comb = comb_ref[...]
    lane = lax.broadcasted_iota(jnp.int32, comb.shape, 1)
    gates = [_mm(xb, wg_ref[j]) for j in range(per_step)]
    ups = [_mm(xb, wu_ref[j]) for j in range(per_step)]
    for j in range(per_step):
        he = gates[j] * jax.nn.sigmoid(gates[j]) * ups[j]
        out = _mm(he.astype(BF16), wd_ref[j])
        w_e = jnp.sum(jnp.where(lane == g * per_step + j, comb, 0.0), axis=1, keepdims=True)
        acc_s[...] += w_e * out

    @pl.when(g == pl.num_programs(1) - 1)
    def _():
        y_ref[...] = _rms(x1_ref[...] + acc_s[...], gfin_ref[...])


def _moe(xn, comb, x1, wg, wu, wd, g_final, tm):
    n, d = x1.shape
    de = wg.shape[2]
    eps = _MOE_EXPERTS_PER_STEP
    rows = lambda width: pl.BlockSpec((tm, width), lambda i, e: (i, 0))
    return pl.pallas_call(
        _moe_kernel,
        grid=(n // tm, N_EXPERTS // eps),
        in_specs=[rows(d), rows(LANES), rows(d),
                  pl.BlockSpec((eps, d, de), lambda i, e: (e, 0, 0)),
                  pl.BlockSpec((eps, d, de), lambda i, e: (e, 0, 0)),
                  pl.BlockSpec((eps, de, d), lambda i, e: (e, 0, 0)),
                  pl.BlockSpec((1, d), lambda i, e: (0, 0))],
        out_specs=rows(d),
        out_shape=jax.ShapeDtypeStruct((n, d), F32),
        scratch_shapes=[pltpu.VMEM((tm, d), F32)],
        compiler_params=_params(("arbitrary", "arbitrary")),
        name="moe",
    )(xn, comb, x1, wg, wu, wd, g_final.reshape(1, d))


def kernel(x_prompt, x_sample, cache_k, cache_v, cache_logf, state_conv, state_C, state_n, state_m,
           page_table, g_mix, w_in, b_fox_f, b_ml_i, b_ml_f, w_conv, b_conv, g_fox_out, g_ml_out,
           w_out, g_ffn, w_router_group, b_router_group, w_router_expert, b_router_expert,
           w_exp_gate, w_exp_up, w_exp_down, g_final):
    depth = w_in.shape[0]
    batch, seq, d = x_prompt.shape
    nb, dec_seq, _ = x_sample.shape
    assert depth == 1 and batch == 1 and dec_seq == 1
    l = 0
    gate_bias = _gate_bias(b_fox_f[l], b_ml_i[l], b_ml_f[l])
    wg, wu, wd = (w[l].astype(BF16) for w in (w_exp_gate, w_exp_up, w_exp_down))
    router = (w_router_group[l], b_router_group[l], w_router_expert[l], b_router_expert[l])

    t_attn = min(512, seq)
    (qa, ka, k_p, v_p, vt, gcol, grow, mq, mk, mvt, mot, tail) = _inproj_prompt(
        x_prompt[0], g_mix[l], w_in[l], gate_bias, w_conv[l], b_conv[l], tm=t_attn)
    y_fox_t = _fox_prompt(qa, ka, vt, tq=min(4096, seq))
    y_ml_t, c_p, n_p, m_p = _mlstm_prompt(mq, mk, mvt, mot, gcol, grow, g_ml_out[l])
    x1, xn, comb = _outproj_router(y_fox_t, y_ml_t, x_prompt[0], w_out[l], g_fox_out[l], g_ffn[l], *router,
                                   tm=min(512, seq), prompt=True)
    y_prompt = _moe(xn, comb, x1, wg, wu, wd, g_final, tm=min(1024, seq))

    xs = x_sample[:, 0, :]
    (q_s, k_s, v_s, g_s, mq_s, mk_s, mv_s, mo_s, conv_new) = _inproj_sample(
        xs, g_mix[l], w_in[l], gate_bias, w_conv[l], b_conv[l], jnp.transpose(state_conv[l], (1, 0, 2)))
    logf_s = g_s[:, _G_LOGF:_G_LOGF + N_HEADS]
    y_fox_s = _fox_sample(q_s, k_s, v_s, logf_s, cache_k[l:l + 1], cache_v[l:l + 1], cache_logf[l:l + 1],
                          page_table)
    y_ml_s, c_s, n_s, m_s = _mlstm_sample(mq_s, mk_s, mv_s, mo_s, g_s[:, _G_IG:_G_IG + N_HEADS],
                                          g_s[:, _G_LF:_G_LF + N_HEADS], state_C[l], state_n[l], state_m[l],
                                          g_ml_out[l])
    x1_s, xn_s, comb_s = _outproj_router(y_fox_s, y_ml_s, xs, w_out[l], g_fox_out[l], g_ffn[l], *router,
                                         tm=nb, prompt=False)
    y_sample = _moe(xn_s, comb_s, x1_s, wg, wu, wd, g_final, tm=nb)

    heads = lambda a, n: a.reshape(1, n, -1, N_HEADS, HEAD_DIM)
    return (
        y_prompt[None], y_sample[:, None, :],
        heads(k_p, 1), heads(v_p, 1), gcol[:, _G_LOGF:_G_LOGF + N_HEADS].reshape(1, 1, seq, N_HEADS),
        tail[SUBLANES - (CONV_WIDTH - 1):][None, None],
        c_p[None, None], n_p[None, None], m_p[None, None],
        heads(k_s, nb), heads(v_s, nb), logf_s.reshape(1, nb, 1, N_HEADS),
        jnp.transpose(conv_new, (1, 0, 2))[None],
        c_s[None], n_s[None], m_s[None],
    )
```

```python
import functools

import numpy as np
import jax
import jax.numpy as jnp
from jax import lax
from jax.experimental import pallas as pl
from jax.experimental.pallas import tpu as pltpu

HEAD_DIM = 64
N_HEADS = 8
WIDTH = N_HEADS * HEAD_DIM
CONV_WIDTH = 4
MLSTM_CHUNK = 128
N_GROUPS = 4
EXPERTS_PER_GROUP = 8
N_EXPERTS = N_GROUPS * EXPERTS_PER_GROUP
RMS_EPS = 1e-6
NEG_INF = -1e30
QK_SCALE = HEAD_DIM ** -0.5
LOG2E = 1.4426950408889634

LANES = 128
SUBLANES = 8
VMEM_LIMIT = 56 * 1024 * 1024

F32 = jnp.float32
BF16 = jnp.bfloat16


def _mm(a, b):
    return jnp.dot(a, b, preferred_element_type=F32)


def _mm_f32(a, b):
    return jnp.dot(a, b, preferred_element_type=F32, precision=lax.Precision.HIGHEST)


def _mm_nt(a, b):
    return lax.dot_general(a, b, (((1,), (1,)), ((), ())), preferred_element_type=F32)


def _mm_tn(a, b):
    return lax.dot_general(a, b, (((0,), (0,)), ((), ())), preferred_element_type=F32)


def _split3(a):
    hi = a.astype(BF16)
    r = a - hi.astype(F32)
    mid = r.astype(BF16)
    lo = (r - mid.astype(F32)).astype(BF16)
    return hi, mid, lo


def _mm3_right(a, b01):
    hi, mid, lo = _split3(a)
    return _mm(hi, b01) + _mm(mid, b01) + _mm(lo, b01)


def _mm3_left(a01, b):
    hi, mid, lo = _split3(b)
    return _mm(a01, hi) + _mm(a01, mid) + _mm(a01, lo)


def _log_sigmoid(x):
    return jnp.minimum(x, 0.0) - jnp.log1p(jnp.exp(-jnp.abs(x)))


def _rms(x, g):
    return x * lax.rsqrt(jnp.mean(x * x, axis=-1, keepdims=True) + RMS_EPS) * g


def _params(sem):
    return pltpu.CompilerParams(dimension_semantics=sem, vmem_limit_bytes=VMEM_LIMIT)


_C_Q, _C_K, _C_KRAW, _C_V, _C_MQK, _C_G, _C_END = 0, 1024, 2048, 2560, 3072, 4096, 4224
_T_V, _T_G, _T_MV, _T_MO, _T_END = 0, 512, 544, 1056, 1568
_G_LOGF, _G_IG, _G_LF, _G_CUM = 0, 8, 16, 24
_AUG = HEAD_DIM


def _gate_tile(z, lane_axis):
    idx = lax.broadcasted_iota(jnp.int32, z.shape, lane_axis)
    is_ig = (idx >= _G_IG) & (idx < _G_LF)
    return jnp.where(is_ig, z, _log_sigmoid(z))


def _inproj_prompt_kernel(x_ref, gmix_ref, w_ref, wt_ref, brow_ref, bcol_ref, wconv_ref, bconv_ref,
                          p2q_ref, p2k_ref,
                          qa_ref, ka_ref, k_ref, v_ref, vt_ref, gcol_ref, grow_ref,
                          mq_ref, mk_ref, mvt_ref, mot_ref, tail_ref,
                          conv_s, ccol_s, crow_s, *, tm):
    i = pl.program_id(0)

    @pl.when(i == 0)
    def _():
        conv_s[0:SUBLANES, :] = jnp.zeros((SUBLANES, conv_s.shape[1]), F32)
        ccol_s[...] = jnp.zeros(ccol_s.shape, F32)
        crow_s[...] = jnp.zeros(crow_s.shape, F32)

    xb = _rms(x_ref[...], gmix_ref[...]).astype(BF16)

    lane = lax.broadcasted_iota(jnp.int32, (tm, LANES), 1)
    g = _gate_tile(_mm(xb, w_ref[:, _C_G:_C_END]) + brow_ref[...], 1)
    zt = _mm_nt(wt_ref[...], xb)
    for h in range(N_HEADS):
        vt_ref[h, 0, 0:HEAD_DIM, :] = zt[h * HEAD_DIM:(h + 1) * HEAD_DIM].astype(BF16)
        vt_ref[h, 0, HEAD_DIM:_VT_ROWS, :] = jnp.ones((_VT_ROWS - HEAD_DIM, tm), BF16)
    mvt_ref[...] = zt[_T_MV:_T_MO].astype(BF16)
    mot_ref[...] = zt[_T_MO:_T_END]
    gt = _gate_tile(zt[_T_G:_T_MV] + bcol_ref[...], 0)

    r = lax.broadcasted_iota(jnp.int32, (tm, tm), 0)
    c = lax.broadcasted_iota(jnp.int32, (tm, tm), 1)
    ltri = (c <= r).astype(BF16)
    utri = (r <= c).astype(BF16)
    cs = _mm3_left(ltri, g) + ccol_s[...]
    ccol_s[...] = cs[tm - 1:tm, :]
    cst = _mm3_right(gt[0:8], utri) + crow_s[...][:, 0:1]
    crow_s[...] = jnp.broadcast_to(cst[:, tm - 1:tm], crow_s.shape)
    in_cum = (lane >= _G_CUM) & (lane < _G_CUM + 8)
    gcol_ref[...] = jnp.where(in_cum, pltpu.roll(cs, _G_CUM, 1), g)
    grow_ref[...] = jnp.concatenate([gt[0:24], cst], axis=0)

    hi, mid, lo = _split3(cs * LOG2E)
    caug = jnp.where(lane < 8, hi.astype(F32),
                     jnp.where(lane < 16, pltpu.roll(mid.astype(F32), 8, 1),
                               jnp.where(lane < 24, pltpu.roll(lo.astype(F32), 16, 1),
                                         jnp.where(lane == 24, 1.0, 0.0)))).astype(BF16)
    zq = _mm(xb, w_ref[:, _C_Q:_C_K]) * (QK_SCALE * LOG2E) + _mm(caug, p2q_ref[...])
    zk = _mm(xb, w_ref[:, _C_K:_C_KRAW]) + _mm(caug, p2k_ref[...])
    for h in range(N_HEADS):
        qa_ref[h] = zq[:, h * LANES:(h + 1) * LANES].astype(BF16)
        ka_ref[h] = zk[:, h * LANES:(h + 1) * LANES].astype(BF16)

    k_ref[...] = _mm(xb, w_ref[:, _C_KRAW:_C_V])
    v_ref[...] = _mm(xb, w_ref[:, _C_V:_C_MQK])

    u = _mm(xb, w_ref[:, _C_MQK:_C_G])
    conv_s[SUBLANES:SUBLANES + tm, :] = u
    acc = bconv_ref[...] + u * wconv_ref[CONV_WIDTH - 1:CONV_WIDTH, :]
    for j in range(CONV_WIDTH - 1):
        back = CONV_WIDTH - 1 - j
        acc = acc + conv_s[SUBLANES - back:SUBLANES - back + tm, :] * wconv_ref[j:j + 1, :]
    qk = acc * jax.nn.sigmoid(acc)
    mq_ref[...] = qk[:, 0:WIDTH]
    mk_ref[...] = qk[:, WIDTH:2 * WIDTH] * QK_SCALE
    tail = conv_s[tm:tm + SUBLANES, :]
    tail_ref[...] = tail
    conv_s[0:SUBLANES, :] = tail


def _aug_placement():
    p2q = np.zeros((LANES, N_HEADS * LANES), np.float32)
    p2k = np.zeros((LANES, N_HEADS * LANES), np.float32)
    for h in range(N_HEADS):
        base = h * LANES + _AUG
        for part in range(3):
            p2q[part * 8 + h, base + part] = 1.0
            p2q[24, base + 3 + part] = 1.0
            p2k[24, base + part] = 1.0
            p2k[part * 8 + h, base + 3 + part] = -1.0
    return jnp.asarray(p2q, BF16), jnp.asarray(p2k, BF16)


def _split_w_in(w):
    d = w.shape[0]
    sizes = [WIDTH, WIDTH, WIDTH, N_HEADS, 2 * WIDTH, WIDTH, N_HEADS, N_HEADS, WIDTH]
    pts = np.cumsum([0] + sizes)
    parts = [w[:, pts[j]:pts[j + 1]] for j in range(len(sizes))]
    fq, fk, fv, ff, mqk, mv, mi, mf, mo = parts
    gates = jnp.concatenate([ff, mi, mf, jnp.zeros((d, LANES - 3 * N_HEADS), w.dtype)], axis=1)
    return fq, fk, fv, mqk, mv, mo, gates


def _gate_bias(b_fox_f, b_ml_i, b_ml_f):
    return jnp.concatenate([b_fox_f, b_ml_i, b_ml_f, jnp.zeros((LANES - 3 * N_HEADS,), F32)])


def _pad_heads(w):
    d = w.shape[0]
    w3 = w.reshape(d, N_HEADS, HEAD_DIM)
    return jnp.pad(w3, ((0, 0), (0, 0), (0, LANES - HEAD_DIM))).reshape(d, N_HEADS * LANES)


def _inproj_prompt(x, g_mix, w_in, gate_bias, w_conv, b_conv, tm=256):
    s, d = x.shape
    fq, fk, fv, mqk, mv, mo, gates = _split_w_in(w_in)
    w_all = jnp.concatenate([_pad_heads(fq), _pad_heads(fk), fk, fv, mqk, gates], axis=1).astype(BF16)
    w_t = jnp.concatenate([fv.T, gates[:, 0:32].T, mv.T, mo.T], axis=0).astype(BF16)
    p2q, p2k = _aug_placement()
    nblk = s // tm
    full = lambda shape: pl.BlockSpec(shape, lambda i: (0,) * len(shape))
    rows = lambda width: pl.BlockSpec((tm, width), lambda i: (i, 0))
    out_shape = (
        jax.ShapeDtypeStruct((N_HEADS, s, LANES), BF16),
        jax.ShapeDtypeStruct((N_HEADS, s, LANES), BF16),
        jax.ShapeDtypeStruct((s, WIDTH), F32),
        jax.ShapeDtypeStruct((s, WIDTH), F32),
        jax.ShapeDtypeStruct((N_HEADS, nblk, _VT_ROWS, tm), BF16),
        jax.ShapeDtypeStruct((s, LANES), F32),
        jax.ShapeDtypeStruct((32, s), F32),
        jax.ShapeDtypeStruct((s, WIDTH), F32),
        jax.ShapeDtypeStruct((s, WIDTH), F32),
        jax.ShapeDtypeStruct((WIDTH, s), BF16),
        jax.ShapeDtypeStruct((WIDTH, s), F32),
        jax.ShapeDtypeStruct((SUBLANES, 2 * WIDTH), F32),
    )
    out_specs = (
        pl.BlockSpec((N_HEADS, tm, LANES), lambda i: (0, i, 0)),
        pl.BlockSpec((N_HEADS, tm, LANES), lambda i: (0, i, 0)),
        rows(WIDTH), rows(WIDTH),
        pl.BlockSpec((N_HEADS, 1, _VT_ROWS, tm), lambda i: (0, i, 0, 0)),
        rows(LANES),
        pl.BlockSpec((32, tm), lambda i: (0, i)),
        rows(WIDTH), rows(WIDTH),
        pl.BlockSpec((WIDTH, tm), lambda i: (0, i)), pl.BlockSpec((WIDTH, tm), lambda i: (0, i)),
        full((SUBLANES, 2 * WIDTH)),
    )
    return pl.pallas_call(
        functools.partial(_inproj_prompt_kernel, tm=tm),
        grid=(nblk,),
        in_specs=[rows(d), full((1, d)), full(w_all.shape), full(w_t.shape), full((1, LANES)),
                  full((32, 1)), full((CONV_WIDTH, 2 * WIDTH)), full((1, 2 * WIDTH)),
                  full(p2q.shape), full(p2k.shape)],
        out_specs=out_specs,
        out_shape=out_shape,
        scratch_shapes=[pltpu.VMEM((tm + 2 * SUBLANES, 2 * WIDTH), F32),
                        pltpu.VMEM((1, LANES), F32),
                        pltpu.VMEM((SUBLANES, LANES), F32)],
        compiler_params=_params(("arbitrary",)),
        name="inproj_prompt",
    )(x, g_mix.reshape(1, d), w_all, w_t, gate_bias.reshape(1, LANES), gate_bias[0:32].reshape(32, 1),
      w_conv, b_conv.reshape(1, 2 * WIDTH), p2q, p2k)


_S_Q, _S_K, _S_V, _S_MQK, _S_MV, _S_MO, _S_G, _S_END = 0, 512, 1024, 1536, 2560, 3072, 3584, 3712


def _inproj_sample_kernel(x_ref, gmix_ref, w_ref, brow_ref, wconv_ref, bconv_ref, cstate_ref,
                          q_ref, k_ref, v_ref, g_ref, mq_ref, mk_ref, mv_ref, mo_ref, cnew_ref):
    xn = _rms(x_ref[...], gmix_ref[...])
    q_ref[...] = _mm_f32(xn, w_ref[:, _S_Q:_S_K])
    k_ref[...] = _mm_f32(xn, w_ref[:, _S_K:_S_V])
    v_ref[...] = _mm_f32(xn, w_ref[:, _S_V:_S_MQK])
    mv_ref[...] = _mm_f32(xn, w_ref[:, _S_MV:_S_MO])
    mo_ref[...] = _mm_f32(xn, w_ref[:, _S_MO:_S_G])
    g_ref[...] = _gate_tile(_mm_f32(xn, w_ref[:, _S_G:_S_END]) + brow_ref[...], 1)
    u = _mm_f32(xn, w_ref[:, _S_MQK:_S_MV])
    acc = bconv_ref[...] + u * wconv_ref[CONV_WIDTH - 1:CONV_WIDTH, :]
    for j in range(CONV_WIDTH - 1):
        acc = acc + cstate_ref[j] * wconv_ref[j:j + 1, :]
    qk = acc * jax.nn.sigmoid(acc)
    mq_ref[...] = qk[:, 0:WIDTH]
    mk_ref[...] = qk[:, WIDTH:2 * WIDTH] * QK_SCALE
    for j in range(CONV_WIDTH - 2):
        cnew_ref[j] = cstate_ref[j + 1]
    cnew_ref[CONV_WIDTH - 2] = u


def _inproj_sample(x, g_mix, w_in, gate_bias, w_conv, b_conv, conv_state_t):
    n, d = x.shape
    fq, fk, fv, mqk, mv, mo, gates = _split_w_in(w_in)
    w_all = jnp.concatenate([fq, fk, fv, mqk, mv, mo, gates], axis=1)
    wide = jax.ShapeDtypeStruct((n, WIDTH), F32)
    out_shape = (wide, wide, wide, jax.ShapeDtypeStruct((n, LANES), F32), wide, wide, wide, wide,
                 jax.ShapeDtypeStruct((CONV_WIDTH - 1, n, 2 * WIDTH), F32))
    return pl.pallas_call(
        _inproj_sample_kernel,
        out_shape=out_shape,
        compiler_params=pltpu.CompilerParams(vmem_limit_bytes=VMEM_LIMIT),
        name="inproj_sample",
    )(x, g_mix.reshape(1, d), w_all, gate_bias.reshape(1, LANES), w_conv, b_conv.reshape(1, 2 * WIDTH),
      conv_state_t)


_ATTN_SUB = 256
_ATTN_AHEAD = 16
_ATTN_CHUNKS = 1
_VT_ROWS = HEAD_DIM + 16


def _fox_prompt_kernel(qa_ref, ka_ref, vt_ref, o_ref, *, tq, tk):
    qi = pl.program_id(1)
    nsub = tq // _ATTN_SUB
    per_q = tq // tk
    qs = [qa_ref[0, c * _ATTN_SUB:(c + 1) * _ATTN_SUB, :] for c in range(nsub)]

    def chunks(rows):
        step = max(rows // _ATTN_CHUNKS, SUBLANES * 2)
        return [(r0, min(r0 + step, rows)) for r0 in range(0, rows, step)]

    def col_max(s, span):
        return jnp.max(s[span[0]:span[1]], axis=0, keepdims=True)

    def finish(state, m_new, p, vblk):
        m, l, acc = state
        alpha = jnp.exp2(m - m_new)
        pv = _mm(vblk, p)
        return m_new, alpha * l + pv[HEAD_DIM:HEAD_DIM + 1], alpha * acc + pv[0:HEAD_DIM]

    def run_block(states, k_start, v_idx, plan):
        vblk = vt_ref[0, v_idx]

        def logits_of(c, rows, k_lo):
            s = _mm_nt(ka_ref[0, pl.ds(k_start, rows), :], qs[c])
            if k_lo is not None:
                kpos = lax.broadcasted_iota(jnp.int32, (rows, _ATTN_SUB), 0) + k_lo
                qpos = lax.broadcasted_iota(jnp.int32, (rows, _ATTN_SUB), 1) + c * _ATTN_SUB
                s = jnp.where(kpos <= qpos, s, NEG_INF)
            return s

        logits, maxes = {}, {}
        for i in range(min(_ATTN_AHEAD, len(plan))):
            c, rows, k_lo = plan[i]
            logits[i] = logits_of(c, rows, k_lo)
            maxes[i] = [col_max(logits[i], sp) for sp in chunks(rows)]
        for i, (c, rows, _) in enumerate(plan):
            m_new = functools.reduce(jnp.maximum, maxes.pop(i), states[c][0])
            nxt = i + _ATTN_AHEAD
            nxt_spans = []
            if nxt < len(plan):
                logits[nxt] = logits_of(*plan[nxt])
                maxes[nxt] = []
                nxt_spans = chunks(plan[nxt][1])
            s_cur = logits.pop(i)
            p_parts = []
            for n, sp in enumerate(chunks(rows)):
                p_parts.append(jnp.exp2(s_cur[sp[0]:sp[1]] - m_new).astype(BF16))
                if n < len(nxt_spans):
                    maxes[nxt].append(col_max(logits[nxt], nxt_spans[n]))
            for sp in nxt_spans[len(p_parts):]:
                maxes[nxt].append(col_max(logits[nxt], sp))
            states[c] = finish(states[c], m_new, jnp.concatenate(p_parts, axis=0), vblk[:, 0:rows])
        return states

    def body(j, carry):
        full = [(c, tk, None) for c in range(nsub)]
        return tuple(run_block(list(carry), pl.multiple_of(j * tk, tk), j, full))

    init = tuple((jnp.full((1, _ATTN_SUB), NEG_INF, F32), jnp.zeros((1, _ATTN_SUB), F32),
                  jnp.zeros((HEAD_DIM, _ATTN_SUB), F32)) for _ in range(nsub))
    states = list(lax.fori_loop(0, qi * per_q, body, init))

    for d in range(per_q):
        k_lo = d * tk
        plan = []
        for c in range(nsub):
            q_lo, q_hi = c * _ATTN_SUB, (c + 1) * _ATTN_SUB
            rows = min(q_hi - k_lo, tk)
            if rows > 0:
                plan.append((c, rows, k_lo if k_lo + rows > q_lo else None))
        states = run_block(states, pl.multiple_of(qi * tq + k_lo, tk), qi * per_q + d, plan)
    for c in range(nsub):
        m, l, acc = states[c]
        o_ref[:, c * _ATTN_SUB:(c + 1) * _ATTN_SUB] = acc / l


def _fox_prompt(qa, ka, vt, tq):
    _, s, _ = qa.shape
    _, nkv, _, tk = vt.shape
    assert tq % tk == 0 and tq % _ATTN_SUB == 0 and tk % _ATTN_SUB == 0
    return pl.pallas_call(
        functools.partial(_fox_prompt_kernel, tq=tq, tk=tk),
        grid=(N_HEADS, s // tq),
        in_specs=[pl.BlockSpec((1, tq, LANES), lambda h, i: (h, i, 0)),
                  pl.BlockSpec((1, s, LANES), lambda h, i: (h, 0, 0)),
                  pl.BlockSpec((1, nkv, _VT_ROWS, tk), lambda h, i: (h, 0, 0, 0))],
        out_specs=pl.BlockSpec((HEAD_DIM, tq), lambda h, i: (h, i)),
        out_shape=jax.ShapeDtypeStruct((WIDTH, s), F32),
        compiler_params=_params(("arbitrary", "arbitrary")),
        name="fox_prompt",
    )(qa, ka, vt)


_PAGES_PER_STEP = 16


def _fox_sample_kernel(pt_ref, qb_ref, vnb_ref, q_ref, kn_ref, lfn_ref, tsuf_ref, *rest, page, npp):
    del pt_ref
    k_refs = rest[0:npp]
    v_refs = rest[npp:2 * npp]
    lf_refs = rest[2 * npp:3 * npp]
    o_ref = rest[3 * npp]
    qs_s, m_s, l_s, acc_s, carry_s = rest[3 * npp + 1:]
    g = pl.program_id(1)

    @pl.when(g == 0)
    def _():
        qs_s[...] = qb_ref[0] * QK_SCALE
        s_self = jnp.sum(q_ref[0] * QK_SCALE * kn_ref[0], axis=1, keepdims=True)
        m_s[...] = jnp.broadcast_to(s_self, m_s.shape)
        l_s[...] = jnp.ones(l_s.shape, F32)
        lane = lax.broadcasted_iota(jnp.int32, acc_s.shape, 2)
        acc_s[...] = jnp.where(lane == 0, vnb_ref[0], 0.0)
        carry_s[...] = lfn_ref[0]

    lf_all = jnp.concatenate([lf_refs[r][...] for r in range(npp)], axis=0)
    suffix = _mm3_right(lf_all, tsuf_ref[...])
    page_sum = jnp.sum(lf_all, axis=1, keepdims=True)
    carry = carry_s[...][:, 0:1]
    scores = []
    for r in range(npp):
        rows = slice(r * N_HEADS, (r + 1) * N_HEADS)
        qk = jnp.concatenate(
            [jnp.sum(k_refs[r][h] * qs_s[h], axis=0, keepdims=True) for h in range(N_HEADS)], axis=0)
        scores.append(qk + suffix[rows] + carry)
        carry = carry + page_sum[rows]
    carry_s[...] = jnp.broadcast_to(carry, carry_s.shape)

    m = m_s[...][:, 0:1]
    m_new = m
    for s in scores:
        m_new = jnp.maximum(m_new, jnp.max(s, axis=1, keepdims=True))
    alpha = jnp.exp(m - m_new)
    probs = [jnp.exp(s - m_new) for s in scores]
    l_new = alpha * l_s[...][:, 0:1]
    for p in probs:
        l_new = l_new + jnp.sum(p, axis=1, keepdims=True)
    l_s[...] = jnp.broadcast_to(l_new, l_s.shape)
    m_s[...] = jnp.broadcast_to(m_new, m_s.shape)
    for h in range(N_HEADS):
        upd = alpha[h:h + 1, :] * acc_s[h]
        for r in range(npp):
            upd = upd + probs[r][h:h + 1, :] * v_refs[r][h]
        acc_s[h] = upd

    @pl.when(g == pl.num_programs(1) - 1)
    def _():
        rr = lax.broadcasted_iota(jnp.int32, (HEAD_DIM, HEAD_DIM), 0)
        cc = lax.broadcasted_iota(jnp.int32, (HEAD_DIM, HEAD_DIM), 1)
        for h in range(N_HEADS):
            col = jnp.sum(acc_s[h], axis=1, keepdims=True) / l_s[...][h:h + 1, 0:1]
            row = jnp.sum(jnp.where(rr == cc, jnp.broadcast_to(col, (HEAD_DIM, HEAD_DIM)), 0.0),
                          axis=0, keepdims=True)
            o_ref[0, h:h + 1, :] = row


def _fox_sample(q, k_new, v_new, logf_new, cache_k, cache_v, cache_logf, page_table):
    nb, n_pages = page_table.shape
    page = cache_k.shape[2]
    npp = _PAGES_PER_STEP
    assert page == LANES and n_pages % npp == 0
    kt = jnp.transpose(cache_k, (0, 1, 3, 4, 2))
    vt = jnp.transpose(cache_v, (0, 1, 3, 4, 2))
    lft = jnp.transpose(cache_logf, (0, 1, 3, 2))
    q3, kn3, vn3 = (a.reshape(nb, N_HEADS, HEAD_DIM) for a in (q, k_new, v_new))
    lanes = lambda a: jnp.broadcast_to(a[..., None], a.shape + (LANES,))
    tsuf = jnp.asarray(np.tril(np.ones((page, page), np.float32), -1), BF16)

    def page_map(r):
        return lambda b, g, pt: (0, pt[b, n_pages - 1 - (g * npp + r)], 0, 0, 0)

    def lf_map(r):
        return lambda b, g, pt: (0, pt[b, n_pages - 1 - (g * npp + r)], 0, 0)

    per_b = pl.BlockSpec((1, N_HEADS, HEAD_DIM), lambda b, g, pt: (b, 0, 0))
    per_b_lanes = pl.BlockSpec((1, N_HEADS, HEAD_DIM, LANES), lambda b, g, pt: (b, 0, 0, 0))
    kv_specs = [pl.BlockSpec((None, None, N_HEADS, HEAD_DIM, page), page_map(r)) for r in range(npp)]
    lf_specs = [pl.BlockSpec((None, None, N_HEADS, page), lf_map(r)) for r in range(npp)]
    stat = pltpu.VMEM((N_HEADS, LANES), F32)
    grid_spec = pltpu.PrefetchScalarGridSpec(
        num_scalar_prefetch=1,
        grid=(nb, n_pages // npp),
        in_specs=[per_b_lanes, per_b_lanes, per_b, per_b,
                  pl.BlockSpec((1, N_HEADS, LANES), lambda b, g, pt: (b, 0, 0)),
                  pl.BlockSpec((page, page), lambda b, g, pt: (0, 0))] + kv_specs + kv_specs + lf_specs,
        out_specs=per_b,
        scratch_shapes=[pltpu.VMEM((N_HEADS, HEAD_DIM, LANES), F32), stat, stat,
                        pltpu.VMEM((N_HEADS, HEAD_DIM, page), F32), stat],
    )
    out = pl.pallas_call(
        functools.partial(_fox_sample_kernel, page=page, npp=npp),
        grid_spec=grid_spec,
        out_shape=jax.ShapeDtypeStruct((nb, N_HEADS, HEAD_DIM), F32),
        compiler_params=_params(("arbitrary", "arbitrary")),
        name="fox_sample",
    )(page_table, lanes(q3), lanes(vn3), q3, kn3, lanes(logf_new), tsuf,
      *([kt] * npp), *([vt] * npp), *([lft] * npp))
    return out.reshape(nb, WIDTH)


def _gated_head_norm(h, o_pre, g):
    hg = h * jax.nn.sigmoid(o_pre)
    return hg * lax.rsqrt(jnp.mean(hg * hg, axis=-1, keepdims=True) + RMS_EPS) * g


def _mlstm_prompt_kernel(q_ref, k_ref, vt_ref, ot_ref, gcol_ref, grow_ref, gml_ref,
                         yt_ref, c_ref, n_ref, m_ref, *, chunk):
    @pl.when(pl.program_id(0) == 0)
    def _():
        c_ref[...] = jnp.zeros(c_ref.shape, F32)
        n_ref[...] = jnp.zeros(n_ref.shape, F32)
        m_ref[...] = jnp.zeros(m_ref.shape, F32)

    r = lax.broadcasted_iota(jnp.int32, (chunk, chunk), 0)
    c = lax.broadcasted_iota(jnp.int32, (chunk, chunk), 1)
    causal_t = r <= c
    gcol = gcol_ref[...]
    grow = grow_ref[...]
    a_col_all = _mm3_left((c <= r).astype(BF16), gcol)
    a_row_all = _mm3_right(grow[_G_LF:_G_LF + 8], causal_t.astype(BF16))
    lane = lax.broadcasted_iota(jnp.int32, (1, LANES), 1)

    heads = range(N_HEADS)
    own = [(lane >= (h % 2) * HEAD_DIM) & (lane < (h % 2 + 1) * HEAD_DIM) for h in heads]
    pair = [slice((h // 2) * LANES, (h // 2 + 1) * LANES) for h in heads]
    rows = [slice(h * HEAD_DIM, (h + 1) * HEAD_DIM) for h in heads]
    kb = [k_ref[:, pair[h]].astype(BF16) for h in range(0, N_HEADS, 2)]
    qb = [jnp.where(own[h], q_ref[:, pair[h]], 0.0).astype(BF16) for h in heads]
    c_prev = [c_ref[h] for h in heads]
    n_prev = [n_ref[h] for h in heads]
    m_prev = [m_ref[h][:, 0:1] for h in heads]
    st = [_mm_nt(kb[h // 2], qb[h]) for h in heads]
    ctq = [_mm_nt(c_prev[h].astype(BF16), qb[h]) for h in heads]
    qn = [_mm_nt(n_prev[h].astype(BF16), qb[h])[0:1] for h in heads]

    m_t, w_intra, w_inter, decay, m_new, vw, w_rows = [], [], [], [], [], [], []
    for h in heads:
        a_r = a_row_all[h:h + 1, :]
        ig_r = grow[_G_IG + h:_G_IG + h + 1, :]
        key_term = gcol[:, _G_IG + h:_G_IG + h + 1] - a_col_all[:, _G_LF + h:_G_LF + h + 1]
        d = jnp.where(causal_t, a_r + key_term, NEG_INF)
        b = a_r + m_prev[h]
        m_t.append(jnp.maximum(b, jnp.max(d, axis=0, keepdims=True)))
        w_intra.append(jnp.exp(d - m_t[h]))
        w_inter.append(jnp.exp(b - m_t[h]))
        m_new.append(m_t[h][:, chunk - 1:chunk])
        a_last = a_r[:, chunk - 1:chunk]
        decay.append(jnp.exp(a_last + m_prev[h] - m_new[h]))
        w_write = jnp.exp(a_last - a_r + ig_r - m_new[h])
        w_rows.append(jnp.broadcast_to(w_write, (SUBLANES, chunk)).astype(BF16))
        vw.append((vt_ref[rows[h], :].astype(F32) * w_write).astype(BF16))
    scores = [st[h] * w_intra[h] for h in heads]
    sv = [_mm(vt_ref[rows[h], :], scores[h].astype(BF16)) for h in heads]
    c_add = [_mm(vw[h], kb[h // 2]) for h in heads]
    n_add = [_mm(w_rows[h], kb[h // 2]) for h in heads]
    for h in heads:
        num = w_inter[h] * ctq[h] + sv[h]
        den = w_inter[h] * qn[h] + jnp.sum(scores[h], axis=0, keepdims=True)
        hh = num / jnp.maximum(jnp.abs(den), jnp.exp(-m_t[h]))
        c_ref[h] = decay[h] * c_prev[h] + jnp.where(own[h], c_add[h], 0.0)
        n_ref[h] = decay[h] * n_prev[h] + jnp.where(own[h], n_add[h], 0.0)
        m_ref[h] = jnp.broadcast_to(m_new[h], (1, LANES))
        hg = hh * jax.nn.sigmoid(ot_ref[rows[h], :])
        yt_ref[rows[h], :] = hg * lax.rsqrt(jnp.mean(hg * hg, axis=0, keepdims=True) + RMS_EPS) \
            * gml_ref[rows[h], :]


def _mlstm_prompt(mq, mk, mvt, mot, gcol, grow, g_ml):
    s = mq.shape[0]
    chunk = int(np.gcd(s, MLSTM_CHUNK))
    tok_rows = pl.BlockSpec((chunk, WIDTH), lambda i: (i, 0))
    tok_lanes = pl.BlockSpec((WIDTH, chunk), lambda i: (0, i))
    state = lambda shape: pl.BlockSpec(shape, lambda i: (0,) * len(shape))
    yt, ct, n, m = pl.pallas_call(
        functools.partial(_mlstm_prompt_kernel, chunk=chunk),
        grid=(s // chunk,),
        in_specs=[tok_rows, tok_rows, tok_lanes, tok_lanes, pl.BlockSpec((chunk, LANES), lambda i: (i, 0)),
                  pl.BlockSpec((32, chunk), lambda i: (0, i)), state((WIDTH, 1))],
        out_specs=(tok_lanes, state((N_HEADS, HEAD_DIM, LANES)), state((N_HEADS, SUBLANES, LANES)),
                   state((N_HEADS, 1, LANES))),
        out_shape=(jax.ShapeDtypeStruct((WIDTH, s), F32),
                   jax.ShapeDtypeStruct((N_HEADS, HEAD_DIM, LANES), F32),
                   jax.ShapeDtypeStruct((N_HEADS, SUBLANES, LANES), F32),
                   jax.ShapeDtypeStruct((N_HEADS, 1, LANES), F32)),
        compiler_params=_params(("arbitrary",)),
        name="mlstm_prompt",
    )(mq, mk, mvt, mot, gcol, grow, g_ml.reshape(WIDTH, 1))
    half = lambda a: jnp.stack([a[h, ..., (h % 2) * HEAD_DIM:(h % 2 + 1) * HEAD_DIM] for h in range(N_HEADS)])
    return yt, jnp.swapaxes(half(ct), 1, 2), half(n)[:, 0, :], m[:, 0, 0]


_MLSTM_SAMPLE_PER_STEP = 1


def _mlstm_sample_kernel(q_ref, k_ref, v_ref, o_ref, ig_ref, lf_ref, m_ref, c_ref, n_ref, gml_ref,
                         y_ref, cn_ref, nn_ref, mn_ref):
    r = lax.broadcasted_iota(jnp.int32, (HEAD_DIM, HEAD_DIM), 0)
    c = lax.broadcasted_iota(jnp.int32, (HEAD_DIM, HEAD_DIM), 1)
    eye = r == c

    def column(row):
        return jnp.sum(jnp.where(eye, jnp.broadcast_to(row, (HEAD_DIM, HEAD_DIM)), 0.0), axis=1, keepdims=True)

    for i in range(q_ref.shape[0]):
        for h in range(N_HEADS):
            q = q_ref[i, h:h + 1, :]
            k = k_ref[i, h:h + 1, :]
            v = v_ref[i, h:h + 1, :]
            ig = ig_ref[i, h:h + 1, 0:1]
            lf = lf_ref[i, h:h + 1, 0:1]
            m_prev = m_ref[i, h:h + 1, 0:1]
            c_prev = c_ref[i, h]
            n_prev = n_ref[i, h:h + 1, :]
            b = lf + m_prev
            m_t = jnp.maximum(b, ig)
            w_intra = jnp.exp(ig - m_t)
            w_inter = jnp.exp(b - m_t)
            scores = jnp.sum(q * k, axis=1, keepdims=True) * w_intra
            qc = jnp.sum(column(q) * c_prev, axis=0, keepdims=True)
            num = w_inter * qc + scores * v
            den = w_inter * jnp.sum(q * n_prev, axis=1, keepdims=True) + scores
            hh = num / jnp.maximum(jnp.abs(den), jnp.exp(-m_t))
            cn_ref[i, h] = w_inter * c_prev + w_intra * (column(k) * v)
            nn_ref[i, h:h + 1, :] = w_inter * n_prev + w_intra * k
            mn_ref[i, h:h + 1, :] = jnp.broadcast_to(m_t, (1, LANES))
            y_ref[i, h:h + 1, :] = _gated_head_norm(hh, o_ref[i, h:h + 1, :], gml_ref[h:h + 1, :])


def _mlstm_sample(mq, mk, mv, mo, ig, lf, state_c, state_n, state_m, g_ml):
    nb = mq.shape[0]
    heads = lambda a: a.reshape(nb, N_HEADS, HEAD_DIM)
    lanes = lambda a: jnp.broadcast_to(a[:, :, None], (nb, N_HEADS, LANES))
    per = int(np.gcd(nb, _MLSTM_SAMPLE_PER_STEP))
    vec = pl.BlockSpec((per, N_HEADS, HEAD_DIM), lambda b: (b, 0, 0))
    sca = pl.BlockSpec((per, N_HEADS, LANES), lambda b: (b, 0, 0))
    mat = pl.BlockSpec((per, N_HEADS, HEAD_DIM, HEAD_DIM), lambda b: (b, 0, 0, 0))
    y, cn, nn, mn = pl.pallas_call(
        _mlstm_sample_kernel,
        grid=(nb // per,),
        in_specs=[vec, vec, vec, vec, sca, sca, sca, mat, vec,
                  pl.BlockSpec((N_HEADS, HEAD_DIM), lambda b: (0, 0))],
        out_specs=(vec, mat, vec, sca),
        out_shape=(jax.ShapeDtypeStruct((nb, N_HEADS, HEAD_DIM), F32),
                   jax.ShapeDtypeStruct((nb, N_HEADS, HEAD_DIM, HEAD_DIM), F32),
                   jax.ShapeDtypeStruct((nb, N_HEADS, HEAD_DIM), F32),
                   jax.ShapeDtypeStruct((nb, N_HEADS, LANES), F32)),
        compiler_params=_params(("arbitrary",)),
        name="mlstm_sample",
    )(heads(mq), heads(mk), heads(mv), heads(mo), lanes(ig), lanes(lf), lanes(state_m), state_c, state_n,
      g_ml.reshape(N_HEADS, HEAD_DIM))
    return y.reshape(nb, WIDTH), cn, nn, mn[:, :, 0]


_R_EXPERT, _R_GROUP = 0, N_EXPERTS


def _outproj_router_kernel(yf_ref, yml_ref, x_ref, wf_ref, wm_ref, gfox_ref, gffn_ref, wr_ref, br_ref,
                           x1_ref, xn_ref, comb_ref, *, prompt):
    yf = yf_ref[...]
    if prompt:
        ms = jnp.mean(yf * yf, axis=0, keepdims=True)
        yfn = (yf * lax.rsqrt(ms + RMS_EPS) * gfox_ref[...]).astype(BF16)
        y = _mm_tn(yfn, wf_ref[...]) + _mm_tn(yml_ref[...].astype(BF16), wm_ref[...])
    else:
        y = _mm_f32(_rms(yf, gfox_ref[...]), wf_ref[...]) + _mm_f32(yml_ref[...], wm_ref[...])
    x1 = x_ref[...] + y
    x1_ref[...] = x1
    xn = _rms(x1, gffn_ref[...])
    xb = xn.astype(BF16)
    xn_ref[...] = xb

    router = _mm(xb, wr_ref[...]) if prompt else _mm_f32(xn, wr_ref[...])
    logits = router + br_ref[...]
    lane = lax.broadcasted_iota(jnp.int32, logits.shape, 1)
    big = jnp.int32(2 * LANES)

    def first_argmax(vals):
        top = jnp.max(vals, axis=1, keepdims=True)
        idx = jnp.min(jnp.where(vals == top, lane, big), axis=1, keepdims=True)
        return top, idx

    is_group = (lane >= _R_GROUP) & (lane < _R_GROUP + N_GROUPS)
    lg = jnp.where(is_group, logits, NEG_INF)
    lg_top, lg_idx = first_argmax(lg)
    gate_g = 1.0 / jnp.sum(jnp.exp(lg - lg_top), axis=1, keepdims=True)
    grp = lg_idx - _R_GROUP
    in_grp = (lane >= grp * EXPERTS_PER_GROUP) & (lane < (grp + 1) * EXPERTS_PER_GROUP)
    le = jnp.where(in_grp, logits, NEG_INF)
    top1, idx1 = first_argmax(le)
    top2, idx2 = first_argmax(jnp.where(lane == idx1, NEG_INF, le))
    e2 = jnp.exp(top2 - top1)
    w1 = gate_g / (1.0 + e2)
    w2 = gate_g * e2 / (1.0 + e2)
    comb_ref[...] = jnp.where(lane == idx1, w1, 0.0) + jnp.where(lane == idx2, w2, 0.0)


def _outproj_router(yf, yml, x, w_out, g_fox, g_ffn, w_rg, b_rg, w_re, b_re, tm, prompt):
    n, d = x.shape
    wdt = BF16 if prompt else F32
    wf = w_out[0:WIDTH].astype(wdt)
    wm = w_out[WIDTH:2 * WIDTH].astype(wdt)
    pad = LANES - N_EXPERTS - N_GROUPS
    wr = jnp.concatenate([w_re, w_rg, jnp.zeros((d, pad), F32)], axis=1).astype(wdt)
    br = jnp.concatenate([b_re, b_rg, jnp.zeros((pad,), F32)]).reshape(1, LANES)
    full = lambda shape: pl.BlockSpec(shape, lambda i: (0,) * len(shape))
    rows = lambda width: pl.BlockSpec((tm, width), lambda i: (i, 0))
    if prompt:
        yf_spec = pl.BlockSpec((WIDTH, tm), lambda i: (0, i))
        gfox = g_fox.reshape(WIDTH, 1)
    else:
        yf_spec = rows(WIDTH)
        gfox = g_fox.reshape(1, WIDTH)
    return pl.pallas_call(
        functools.partial(_outproj_router_kernel, prompt=prompt),
        grid=(n // tm,),
        in_specs=[yf_spec, yf_spec, rows(d), full((WIDTH, d)), full((WIDTH, d)), full(gfox.shape),
                  full((1, d)), full((d, LANES)), full((1, LANES))],
        out_specs=(rows(d), rows(d), rows(LANES)),
        out_shape=(jax.ShapeDtypeStruct((n, d), F32), jax.ShapeDtypeStruct((n, d), BF16),
                   jax.ShapeDtypeStruct((n, LANES), F32)),
        compiler_params=_params(("arbitrary",)),
        name="outproj_router_prompt" if prompt else "outproj_router_sample",
    )(yf, yml, x, wf, wm, gfox, g_ffn.reshape(1, d), wr, br)


_MOE_EXPERTS_PER_STEP = 4


def _moe_kernel(xn_ref, comb_ref, x1_ref, wg_ref, wu_ref, wd_ref, gfin_ref, y_ref, acc_s):
    g = pl.program_id(1)
    per_step = wg_ref.shape[0]

    @pl.when(g == 0)
    def _():
        acc_s[...] = jnp.zeros(acc_s.shape, F32)

    xb = xn_ref[...]
    comb = comb_ref[...]
    lane = lax.broadcasted_iota(jnp.int32, comb.shape, 1)
    gates = [_mm(xb, wg_ref[j]) for j in range(per_step)]
    ups = [_mm(xb, wu_ref[j]) for j in range(per_step)]
    for j in range(per_step):
        he = gates[j] * jax.nn.sigmoid(gates[j]) * ups[j]
        out = _mm(he.astype(BF16), wd_ref[j])
        w_e = jnp.sum(jnp.where(lane == g * per_step + j, comb, 0.0), axis=1, keepdims=True)
        acc_s[...] += w_e * out

    @pl.when(g == pl.num_programs(1) - 1)
    def _():
        y_ref[...] = _rms(x1_ref[...] + acc_s[...], gfin_ref[...])


def _moe(xn, comb, x1, wg, wu, wd, g_final, tm):
    n, d = x1.shape
    de = wg.shape[2]
    eps = _MOE_EXPERTS_PER_STEP
    rows = lambda width: pl.BlockSpec((tm, width), lambda i, e: (i, 0))
    return pl.pallas_call(
        _moe_kernel,
        grid=(n // tm, N_EXPERTS // eps),
        in_specs=[rows(d), rows(LANES), rows(d),
                  pl.BlockSpec((eps, d, de), lambda i, e: (e, 0, 0)),
                  pl.BlockSpec((eps, d, de), lambda i, e: (e, 0, 0)),
                  pl.BlockSpec((eps, de, d), lambda i, e: (e, 0, 0)),
                  pl.BlockSpec((1, d), lambda i, e: (0, 0))],
        out_specs=rows(d),
        out_shape=jax.ShapeDtypeStruct((n, d), F32),
        scratch_shapes=[pltpu.VMEM((tm, d), F32)],
        compiler_params=_params(("arbitrary", "arbitrary")),
        name="moe",
    )(xn, comb, x1, wg, wu, wd, g_final.reshape(1, d))


def kernel(x_prompt, x_sample, cache_k, cache_v, cache_logf, state_conv, state_C, state_n, state_m,
           page_table, g_mix, w_in, b_fox_f, b_ml_i, b_ml_f, w_conv, b_conv, g_fox_out, g_ml_out,
           w_out, g_ffn, w_router_group, b_router_group, w_router_expert, b_router_expert,
           w_exp_gate, w_exp_up, w_exp_down, g_final):
    depth = w_in.shape[0]
    batch, seq, d = x_prompt.shape
    nb, dec_seq, _ = x_sample.shape
    assert depth == 1 and batch == 1 and dec_seq == 1
    l = 0
    gate_bias = _gate_bias(b_fox_f[l], b_ml_i[l], b_ml_f[l])
    wg, wu, wd = (w[l].astype(BF16) for w in (w_exp_gate, w_exp_up, w_exp_down))
    router = (w_router_group[l], b_router_group[l], w_router_expert[l], b_router_expert[l])

    t_attn = min(512, seq)
    (qa, ka, k_p, v_p, vt, gcol, grow, mq, mk, mvt, mot, tail) = _inproj_prompt(
        x_prompt[0], g_mix[l], w_in[l], gate_bias, w_conv[l], b_conv[l], tm=t_attn)
    y_fox_t = _fox_prompt(qa, ka, vt, tq=min(4096, seq))
    y_ml_t, c_p, n_p, m_p = _mlstm_prompt(mq, mk, mvt, mot, gcol, grow, g_ml_out[l])
    x1, xn, comb = _outproj_router(y_fox_t, y_ml_t, x_prompt[0], w_out[l], g_fox_out[l], g_ffn[l], *router,
                                   tm=min(512, seq), prompt=True)
    y_prompt = _moe(xn, comb, x1, wg, wu, wd, g_final, tm=min(1024, seq))

    xs = x_sample[:, 0, :]
    (q_s, k_s, v_s, g_s, mq_s, mk_s, mv_s, mo_s, conv_new) = _inproj_sample(
        xs, g_mix[l], w_in[l], gate_bias, w_conv[l], b_conv[l], jnp.transpose(state_conv[l], (1, 0, 2)))
    logf_s = g_s[:, _G_LOGF:_G_LOGF + N_HEADS]
    y_fox_s = _fox_sample(q_s, k_s, v_s, logf_s, cache_k[l:l + 1], cache_v[l:l + 1], cache_logf[l:l + 1],
                          page_table)
    y_ml_s, c_s, n_s, m_s = _mlstm_sample(mq_s, mk_s, mv_s, mo_s, g_s[:, _G_IG:_G_IG + N_HEADS],
                                          g_s[:, _G_LF:_G_LF + N_HEADS], state_C[l], state_n[l], state_m[l],
                                          g_ml_out[l])
    x1_s, xn_s, comb_s = _outproj_router(y_fox_s, y_ml_s, xs, w_out[l], g_fox_out[l], g_ffn[l], *router,
                                         tm=nb, prompt=False)
    y_sample = _moe(xn_s, comb_s, x1_s, wg, wu, wd, g_final, tm=nb)

    heads = lambda a, n: a.reshape(1, n, -1, N_HEADS, HEAD_DIM)
    return (
        y_prompt[None], y_sample[:, None, :],
        heads(k_p, 1), heads(v_p, 1), gcol[:, _G_LOGF:_G_LOGF + N_HEADS].reshape(1, 1, seq, N_HEADS),
        tail[SUBLANES - (CONV_WIDTH - 1):][None, None],
        c_p[None, None], n_p[None, None], m_p[None, None],
        heads(k_s, nb), heads(v_s, nb), logf_s.reshape(1, nb, 1, N_HEADS),
        jnp.transpose(conv_new, (1, 0, 2))[None],
        c_s[None], n_s[None], m_s[None],
    )
```

```python
import functools

import numpy as np
import jax
import jax.numpy as jnp
from jax import lax
from jax.experimental import pallas as pl
from jax.experimental.pallas import tpu as pltpu

HEAD_DIM = 64
N_HEADS = 8
WIDTH = N_HEADS * HEAD_DIM
CONV_WIDTH = 4
MLSTM_CHUNK = 128
N_GROUPS = 4
EXPERTS_PER_GROUP = 8
N_EXPERTS = N_GROUPS * EXPERTS_PER_GROUP
RMS_EPS = 1e-6
NEG_INF = -1e30
QK_SCALE = HEAD_DIM ** -0.5
LOG2E = 1.4426950408889634

LANES = 128
SUBLANES = 8
VMEM_LIMIT = 56 * 1024 * 1024

F32 = jnp.float32
BF16 = jnp.bfloat16


def _mm(a, b):
    return jnp.dot(a, b, preferred_element_type=F32)


def _mm_f32(a, b):
    return jnp.dot(a, b, preferred_element_type=F32, precision=lax.Precision.HIGHEST)


def _mm_nt(a, b):
    return lax.dot_general(a, b, (((1,), (1,)), ((), ())), preferred_element_type=F32)


def _mm_tn(a, b):
    return lax.dot_general(a, b, (((0,), (0,)), ((), ())), preferred_element_type=F32)


def _split3(a):
    hi = a.astype(BF16)
    r = a - hi.astype(F32)
    mid = r.astype(BF16)
    lo = (r - mid.astype(F32)).astype(BF16)
    return hi, mid, lo


def _mm3_right(a, b01):
    hi, mid, lo = _split3(a)
    return _mm(hi, b01) + _mm(mid, b01) + _mm(lo, b01)


def _mm3_left(a01, b):
    hi, mid, lo = _split3(b)
    return _mm(a01, hi) + _mm(a01, mid) + _mm(a01, lo)


def _log_sigmoid(x):
    return jnp.minimum(x, 0.0) - jnp.log1p(jnp.exp(-jnp.abs(x)))


def _rms(x, g):
    return x * lax.rsqrt(jnp.mean(x * x, axis=-1, keepdims=True) + RMS_EPS) * g


def _params(sem):
    return pltpu.CompilerParams(dimension_semantics=sem, vmem_limit_bytes=VMEM_LIMIT)


_C_Q, _C_K, _C_KRAW, _C_V, _C_MQK, _C_G, _C_END = 0, 1024, 2048, 2560, 3072, 4096, 4224
_T_V, _T_G, _T_MV, _T_MO, _T_END = 0, 512, 544, 1056, 1568
_G_LOGF, _G_IG, _G_LF, _G_CUM = 0, 8, 16, 24
_AUG = HEAD_DIM


def _gate_tile(z, lane_axis):
    idx = lax.broadcasted_iota(jnp.int32, z.shape, lane_axis)
    is_ig = (idx >= _G_IG) & (idx < _G_LF)
    return jnp.where(is_ig, z, _log_sigmoid(z))


def _inproj_prompt_kernel(x_ref, gmix_ref, w_ref, wt_ref, brow_ref, bcol_ref, wconv_ref, bconv_ref,
                          p2q_ref, p2k_ref,
                          qa_ref, ka_ref, k_ref, v_ref, vt_ref, gcol_ref, grow_ref,
                          mq_ref, mk_ref, mvt_ref, mot_ref, tail_ref,
                          conv_s, ccol_s, crow_s, *, tm):
    i = pl.program_id(0)

    @pl.when(i == 0)
    def _():
        conv_s[0:SUBLANES, :] = jnp.zeros((SUBLANES, conv_s.shape[1]), F32)
        ccol_s[...] = jnp.zeros(ccol_s.shape, F32)
        crow_s[...] = jnp.zeros(crow_s.shape, F32)

    xb = _rms(x_ref[...], gmix_ref[...]).astype(BF16)

    lane = lax.broadcasted_iota(jnp.int32, (tm, LANES), 1)
    g = _gate_tile(_mm(xb, w_ref[:, _C_G:_C_END]) + brow_ref[...], 1)
    zt = _mm_nt(wt_ref[...], xb)
    for h in range(N_HEADS):
        vt_ref[h, 0, 0:HEAD_DIM, :] = zt[h * HEAD_DIM:(h + 1) * HEAD_DIM].astype(BF16)
        vt_ref[h, 0, HEAD_DIM:_VT_ROWS, :] = jnp.ones((_VT_ROWS - HEAD_DIM, tm), BF16)
    mvt_ref[...] = zt[_T_MV:_T_MO].astype(BF16)
    mot_ref[...] = zt[_T_MO:_T_END]
    gt = _gate_tile(zt[_T_G:_T_MV] + bcol_ref[...], 0)

    r = lax.broadcasted_iota(jnp.int32, (tm, tm), 0)
    c = lax.broadcasted_iota(jnp.int32, (tm, tm), 1)
    ltri = (c <= r).astype(BF16)
    utri = (r <= c).astype(BF16)
    cs = _mm3_left(ltri, g) + ccol_s[...]
    ccol_s[...] = cs[tm - 1:tm, :]
    cst = _mm3_right(gt[0:8], utri) + crow_s[...][:, 0:1]
    crow_s[...] = jnp.broadcast_to(cst[:, tm - 1:tm], crow_s.shape)
    in_cum = (lane >= _G_CUM) & (lane < _G_CUM + 8)
    gcol_ref[...] = jnp.where(in_cum, pltpu.roll(cs, _G_CUM, 1), g)
    grow_ref[...] = jnp.concatenate([gt[0:24], cst], axis=0)

    hi, mid, lo = _split3(cs * LOG2E)
    caug = jnp.where(lane < 8, hi.astype(F32),
                     jnp.where(lane < 16, pltpu.roll(mid.astype(F32), 8, 1),
                               jnp.where(lane < 24, pltpu.roll(lo.astype(F32), 16, 1),
                                         jnp.where(lane == 24, 1.0, 0.0)))).astype(BF16)
    zq = _mm(xb, w_ref[:, _C_Q:_C_K]) * (QK_SCALE * LOG2E) + _mm(caug, p2q_ref[...])
    zk = _mm(xb, w_ref[:, _C_K:_C_KRAW]) + _mm(caug, p2k_ref[...])
    for h in range(N_HEADS):
        qa_ref[h] = zq[:, h * LANES:(h + 1) * LANES].astype(BF16)
        ka_ref[h] = zk[:, h * LANES:(h + 1) * LANES].astype(BF16)

    for h in range(N_HEADS):
        k_ref[:, h * HEAD_DIM:(h + 1) * HEAD_DIM] = zk[:, h * LANES:h * LANES + HEAD_DIM]
    v_ref[...] = _mm(xb, w_ref[:, _C_V:_C_MQK])

    u = _mm(xb, w_ref[:, _C_MQK:_C_G])
    conv_s[SUBLANES:SUBLANES + tm, :] = u
    acc = bconv_ref[...] + u * wconv_ref[CONV_WIDTH - 1:CONV_WIDTH, :]
    for j in range(CONV_WIDTH - 1):
        back = CONV_WIDTH - 1 - j
        acc = acc + conv_s[SUBLANES - back:SUBLANES - back + tm, :] * wconv_ref[j:j + 1, :]
    qk = acc * jax.nn.sigmoid(acc)
    mq_ref[...] = qk[:, 0:WIDTH]
    mk_ref[...] = qk[:, WIDTH:2 * WIDTH] * QK_SCALE
    tail = conv_s[tm:tm + SUBLANES, :]
    tail_ref[...] = tail
    conv_s[0:SUBLANES, :] = tail


def _aug_placement():
    p2q = np.zeros((LANES, N_HEADS * LANES), np.float32)
    p2k = np.zeros((LANES, N_HEADS * LANES), np.float32)
    for h in range(N_HEADS):
        base = h * LANES + _AUG
        for part in range(3):
            p2q[part * 8 + h, base + part] = 1.0
            p2q[24, base + 3 + part] = 1.0
            p2k[24, base + part] = 1.0
            p2k[part * 8 + h, base + 3 + part] = -1.0
    return jnp.asarray(p2q, BF16), jnp.asarray(p2k, BF16)


def _split_w_in(w):
    d = w.shape[0]
    sizes = [WIDTH, WIDTH, WIDTH, N_HEADS, 2 * WIDTH, WIDTH, N_HEADS, N_HEADS, WIDTH]
    pts = np.cumsum([0] + sizes)
    parts = [w[:, pts[j]:pts[j + 1]] for j in range(len(sizes))]
    fq, fk, fv, ff, mqk, mv, mi, mf, mo = parts
    gates = jnp.concatenate([ff, mi, mf, jnp.zeros((d, LANES - 3 * N_HEADS), w.dtype)], axis=1)
    return fq, fk, fv, mqk, mv, mo, gates


def _gate_bias(b_fox_f, b_ml_i, b_ml_f):
    return jnp.concatenate([b_fox_f, b_ml_i, b_ml_f, jnp.zeros((LANES - 3 * N_HEADS,), F32)])


def _pad_heads(w):
    d = w.shape[0]
    w3 = w.reshape(d, N_HEADS, HEAD_DIM)
    return jnp.pad(w3, ((0, 0), (0, 0), (0, LANES - HEAD_DIM))).reshape(d, N_HEADS * LANES)


def _inproj_prompt(x, g_mix, w_in, gate_bias, w_conv, b_conv, tm=256):
    s, d = x.shape
    fq, fk, fv, mqk, mv, mo, gates = _split_w_in(w_in)
    w_all = jnp.concatenate([_pad_heads(fq), _pad_heads(fk), fk, fv, mqk, gates], axis=1).astype(BF16)
    w_t = jnp.concatenate([fv.T, gates[:, 0:32].T, mv.T, mo.T], axis=0).astype(BF16)
    p2q, p2k = _aug_placement()
    nblk = s // tm
    full = lambda shape: pl.BlockSpec(shape, lambda i: (0,) * len(shape))
    rows = lambda width: pl.BlockSpec((tm, width), lambda i: (i, 0))
    out_shape = (
        jax.ShapeDtypeStruct((N_HEADS, s, LANES), BF16),
        jax.ShapeDtypeStruct((N_HEADS, s, LANES), BF16),
        jax.ShapeDtypeStruct((s, WIDTH), F32),
        jax.ShapeDtypeStruct((s, WIDTH), F32),
        jax.ShapeDtypeStruct((N_HEADS, nblk, _VT_ROWS, tm), BF16),
        jax.ShapeDtypeStruct((s, LANES), F32),
        jax.ShapeDtypeStruct((32, s), F32),
        jax.ShapeDtypeStruct((s, WIDTH), F32),
        jax.ShapeDtypeStruct((s, WIDTH), F32),
        jax.ShapeDtypeStruct((WIDTH, s), BF16),
        jax.ShapeDtypeStruct((WIDTH, s), F32),
        jax.ShapeDtypeStruct((SUBLANES, 2 * WIDTH), F32),
    )
    out_specs = (
        pl.BlockSpec((N_HEADS, tm, LANES), lambda i: (0, i, 0)),
        pl.BlockSpec((N_HEADS, tm, LANES), lambda i: (0, i, 0)),
        rows(WIDTH), rows(WIDTH),
        pl.BlockSpec((N_HEADS, 1, _VT_ROWS, tm), lambda i: (0, i, 0, 0)),
        rows(LANES),
        pl.BlockSpec((32, tm), lambda i: (0, i)),
        rows(WIDTH), rows(WIDTH),
        pl.BlockSpec((WIDTH, tm), lambda i: (0, i)), pl.BlockSpec((WIDTH, tm), lambda i: (0, i)),
        full((SUBLANES, 2 * WIDTH)),
    )
    return pl.pallas_call(
        functools.partial(_inproj_prompt_kernel, tm=tm),
        grid=(nblk,),
        in_specs=[rows(d), full((1, d)), full(w_all.shape), full(w_t.shape), full((1, LANES)),
                  full((32, 1)), full((CONV_WIDTH, 2 * WIDTH)), full((1, 2 * WIDTH)),
                  full(p2q.shape), full(p2k.shape)],
        out_specs=out_specs,
        out_shape=out_shape,
        scratch_shapes=[pltpu.VMEM((tm + 2 * SUBLANES, 2 * WIDTH), F32),
                        pltpu.VMEM((1, LANES), F32),
                        pltpu.VMEM((SUBLANES, LANES), F32)],
        compiler_params=_params(("arbitrary",)),
        name="inproj_prompt",
    )(x, g_mix.reshape(1, d), w_all, w_t, gate_bias.reshape(1, LANES), gate_bias[0:32].reshape(32, 1),
      w_conv, b_conv.reshape(1, 2 * WIDTH), p2q, p2k)


_S_Q, _S_K, _S_V, _S_MQK, _S_MV, _S_MO, _S_G, _S_END = 0, 512, 1024, 1536, 2560, 3072, 3584, 3712


def _inproj_sample_kernel(x_ref, gmix_ref, w_ref, brow_ref, wconv_ref, bconv_ref, cstate_ref,
                          q_ref, k_ref, v_ref, g_ref, mq_ref, mk_ref, mv_ref, mo_ref, cnew_ref):
    xn = _rms(x_ref[...], gmix_ref[...])
    q_ref[...] = _mm_f32(xn, w_ref[:, _S_Q:_S_K])
    k_ref[...] = _mm_f32(xn, w_ref[:, _S_K:_S_V])
    v_ref[...] = _mm_f32(xn, w_ref[:, _S_V:_S_MQK])
    mv_ref[...] = _mm_f32(xn, w_ref[:, _S_MV:_S_MO])
    mo_ref[...] = _mm_f32(xn, w_ref[:, _S_MO:_S_G])
    g_ref[...] = _gate_tile(_mm_f32(xn, w_ref[:, _S_G:_S_END]) + brow_ref[...], 1)
    u = _mm_f32(xn, w_ref[:, _S_MQK:_S_MV])
    acc = bconv_ref[...] + u * wconv_ref[CONV_WIDTH - 1:CONV_WIDTH, :]
    for j in range(CONV_WIDTH - 1):
        acc = acc + cstate_ref[j] * wconv_ref[j:j + 1, :]
    qk = acc * jax.nn.sigmoid(acc)
    mq_ref[...] = qk[:, 0:WIDTH]
    mk_ref[...] = qk[:, WIDTH:2 * WIDTH] * QK_SCALE
    for j in range(CONV_WIDTH - 2):
        cnew_ref[j] = cstate_ref[j + 1]
    cnew_ref[CONV_WIDTH - 2] = u


def _inproj_sample(x, g_mix, w_in, gate_bias, w_conv, b_conv, conv_state_t):
    n, d = x.shape
    fq, fk, fv, mqk, mv, mo, gates = _split_w_in(w_in)
    w_all = jnp.concatenate([fq, fk, fv, mqk, mv, mo, gates], axis=1)
    wide = jax.ShapeDtypeStruct((n, WIDTH), F32)
    out_shape = (wide, wide, wide, jax.ShapeDtypeStruct((n, LANES), F32), wide, wide, wide, wide,
                 jax.ShapeDtypeStruct((CONV_WIDTH - 1, n, 2 * WIDTH), F32))
    return pl.pallas_call(
        _inproj_sample_kernel,
        out_shape=out_shape,
        compiler_params=pltpu.CompilerParams(vmem_limit_bytes=VMEM_LIMIT),
        name="inproj_sample",
    )(x, g_mix.reshape(1, d), w_all, gate_bias.reshape(1, LANES), w_conv, b_conv.reshape(1, 2 * WIDTH),
      conv_state_t)


_ATTN_SUB = 256
_ATTN_AHEAD = 16
_ATTN_CHUNKS = 1
_VT_ROWS = HEAD_DIM + 16


def _fox_prompt_kernel(qa_ref, ka_ref, vt_ref, o_ref, *, tq, tk):
    qi = pl.program_id(1)
    nsub = tq // _ATTN_SUB
    per_q = tq // tk
    qs = [qa_ref[0, c * _ATTN_SUB:(c + 1) * _ATTN_SUB, :] for c in range(nsub)]

    def chunks(rows):
        step = max(rows // _ATTN_CHUNKS, SUBLANES * 2)
        return [(r0, min(r0 + step, rows)) for r0 in range(0, rows, step)]

    def col_max(s, span):
        return jnp.max(s[span[0]:span[1]], axis=0, keepdims=True)

    def finish(state, m_new, p, vblk):
        m, l, acc = state
        alpha = jnp.exp2(m - m_new)
        pv = _mm(vblk, p)
        return m_new, alpha * l + pv[HEAD_DIM:HEAD_DIM + 1], alpha * acc + pv[0:HEAD_DIM]

    def run_block(states, k_start, v_idx, plan):
        vblk = vt_ref[0, v_idx]

        def logits_of(c, rows, k_lo):
            s = _mm_nt(ka_ref[0, pl.ds(k_start, rows), :], qs[c])
            if k_lo is not None:
                kpos = lax.broadcasted_iota(jnp.int32, (rows, _ATTN_SUB), 0) + k_lo
                qpos = lax.broadcasted_iota(jnp.int32, (rows, _ATTN_SUB), 1) + c * _ATTN_SUB
                s = jnp.where(kpos <= qpos, s, NEG_INF)
            return s

        logits, maxes = {}, {}
        for i in range(min(_ATTN_AHEAD, len(plan))):
            c, rows, k_lo = plan[i]
            logits[i] = logits_of(c, rows, k_lo)
            maxes[i] = [col_max(logits[i], sp) for sp in chunks(rows)]
        for i, (c, rows, _) in enumerate(plan):
            m_new = functools.reduce(jnp.maximum, maxes.pop(i), states[c][0])
            nxt = i + _ATTN_AHEAD
            nxt_spans = []
            if nxt < len(plan):
                logits[nxt] = logits_of(*plan[nxt])
                maxes[nxt] = []
                nxt_spans = chunks(plan[nxt][1])
            s_cur = logits.pop(i)
            p_parts = []
            for n, sp in enumerate(chunks(rows)):
                p_parts.append(jnp.exp2(s_cur[sp[0]:sp[1]] - m_new).astype(BF16))
                if n < len(nxt_spans):
                    maxes[nxt].append(col_max(logits[nxt], nxt_spans[n]))
            for sp in nxt_spans[len(p_parts):]:
                maxes[nxt].append(col_max(logits[nxt], sp))
            states[c] = finish(states[c], m_new, jnp.concatenate(p_parts, axis=0), vblk[:, 0:rows])
        return states

    def body(j, carry):
        full = [(c, tk, None) for c in range(nsub)]
        return tuple(run_block(list(carry), pl.multiple_of(j * tk, tk), j, full))

    init = tuple((jnp.full((1, _ATTN_SUB), NEG_INF, F32), jnp.zeros((1, _ATTN_SUB), F32),
                  jnp.zeros((HEAD_DIM, _ATTN_SUB), F32)) for _ in range(nsub))
    states = list(lax.fori_loop(0, qi * per_q, body, init))

    for d in range(per_q):
        k_lo = d * tk
        plan = []
        for c in range(nsub):
            q_lo, q_hi = c * _ATTN_SUB, (c + 1) * _ATTN_SUB
            rows = min(q_hi - k_lo, tk)
            if rows > 0:
                plan.append((c, rows, k_lo if k_lo + rows > q_lo else None))
        states = run_block(states, pl.multiple_of(qi * tq + k_lo, tk), qi * per_q + d, plan)
    for c in range(nsub):
        m, l, acc = states[c]
        o_ref[:, c * _ATTN_SUB:(c + 1) * _ATTN_SUB] = acc / l


def _fox_prompt(qa, ka, vt, tq):
    _, s, _ = qa.shape
    _, nkv, _, tk = vt.shape
    assert tq % tk == 0 and tq % _ATTN_SUB == 0 and tk % _ATTN_SUB == 0
    return pl.pallas_call(
        functools.partial(_fox_prompt_kernel, tq=tq, tk=tk),
        grid=(N_HEADS, s // tq),
        in_specs=[pl.BlockSpec((1, tq, LANES), lambda h, i: (h, i, 0)),
                  pl.BlockSpec((1, s, LANES), lambda h, i: (h, 0, 0)),
                  pl.BlockSpec((1, nkv, _VT_ROWS, tk), lambda h, i: (h, 0, 0, 0))],
        out_specs=pl.BlockSpec((HEAD_DIM, tq), lambda h, i: (h, i)),
        out_shape=jax.ShapeDtypeStruct((WIDTH, s), F32),
        compiler_params=_params(("arbitrary", "arbitrary")),
        name="fox_prompt",
    )(qa, ka, vt)


_PAGES_PER_STEP = 16


def _fox_sample_kernel(pt_ref, qb_ref, vnb_ref, q_ref, kn_ref, lfn_ref, tsuf_ref, *rest, page, npp):
    del pt_ref
    k_refs = rest[0:npp]
    v_refs = rest[npp:2 * npp]
    lf_refs = rest[2 * npp:3 * npp]
    o_ref = rest[3 * npp]
    qs_s, m_s, l_s, acc_s, carry_s = rest[3 * npp + 1:]
    g = pl.program_id(1)

    @pl.when(g == 0)
    def _():
        qs_s[...] = qb_ref[0] * QK_SCALE
        s_self = jnp.sum(q_ref[0] * QK_SCALE * kn_ref[0], axis=1, keepdims=True)
        m_s[...] = jnp.broadcast_to(s_self, m_s.shape)
        l_s[...] = jnp.ones(l_s.shape, F32)
        lane = lax.broadcasted_iota(jnp.int32, acc_s.shape, 2)
        acc_s[...] = jnp.where(lane == 0, vnb_ref[0], 0.0)
        carry_s[...] = lfn_ref[0]

    lf_all = jnp.concatenate([lf_refs[r][...] for r in range(npp)], axis=0)
    suffix = _mm3_right(lf_all, tsuf_ref[...])
    page_sum = jnp.sum(lf_all, axis=1, keepdims=True)
    carry = carry_s[...][:, 0:1]
    scores = []
    for r in range(npp):
        rows = slice(r * N_HEADS, (r + 1) * N_HEADS)
        qk = jnp.concatenate(
            [jnp.sum(k_refs[r][h] * qs_s[h], axis=0, keepdims=True) for h in range(N_HEADS)], axis=0)
        scores.append(qk + suffix[rows] + carry)
        carry = carry + page_sum[rows]
    carry_s[...] = jnp.broadcast_to(carry, carry_s.shape)

    m = m_s[...][:, 0:1]
    m_new = m
    for s in scores:
        m_new = jnp.maximum(m_new, jnp.max(s, axis=1, keepdims=True))
    alpha = jnp.exp(m - m_new)
    probs = [jnp.exp(s - m_new) for s in scores]
    l_new = alpha * l_s[...][:, 0:1]
    for p in probs:
        l_new = l_new + jnp.sum(p, axis=1, keepdims=True)
    l_s[...] = jnp.broadcast_to(l_new, l_s.shape)
    m_s[...] = jnp.broadcast_to(m_new, m_s.shape)
    for h in range(N_HEADS):
        upd = alpha[h:h + 1, :] * acc_s[h]
        for r in range(npp):
            upd = upd + probs[r][h:h + 1, :] * v_refs[r][h]
        acc_s[h] = upd

    @pl.when(g == pl.num_programs(1) - 1)
    def _():
        rr = lax.broadcasted_iota(jnp.int32, (HEAD_DIM, HEAD_DIM), 0)
        cc = lax.broadcasted_iota(jnp.int32, (HEAD_DIM, HEAD_DIM), 1)
        for h in range(N_HEADS):
            col = jnp.sum(acc_s[h], axis=1, keepdims=True) / l_s[...][h:h + 1, 0:1]
            row = jnp.sum(jnp.where(rr == cc, jnp.broadcast_to(col, (HEAD_DIM, HEAD_DIM)), 0.0),
                          axis=0, keepdims=True)
            o_ref[0, h:h + 1, :] = row


def _fox_sample(q, k_new, v_new, logf_new, cache_k, cache_v, cache_logf, page_table):
    nb, n_pages = page_table.shape
    page = cache_k.shape[2]
    npp = _PAGES_PER_STEP
    assert page == LANES and n_pages % npp == 0
    kt = jnp.transpose(cache_k, (0, 1, 3, 4, 2))
    vt = jnp.transpose(cache_v, (0, 1, 3, 4, 2))
    lft = jnp.transpose(cache_logf, (0, 1, 3, 2))
    q3, kn3, vn3 = (a.reshape(nb, N_HEADS, HEAD_DIM) for a in (q, k_new, v_new))
    lanes = lambda a: jnp.broadcast_to(a[..., None], a.shape + (LANES,))
    tsuf = jnp.asarray(np.tril(np.ones((page, page), np.float32), -1), BF16)

    def page_map(r):
        return lambda b, g, pt: (0, pt[b, n_pages - 1 - (g * npp + r)], 0, 0, 0)

    def lf_map(r):
        return lambda b, g, pt: (0, pt[b, n_pages - 1 - (g * npp + r)], 0, 0)

    per_b = pl.BlockSpec((1, N_HEADS, HEAD_DIM), lambda b, g, pt: (b, 0, 0))
    per_b_lanes = pl.BlockSpec((1, N_HEADS, HEAD_DIM, LANES), lambda b, g, pt: (b, 0, 0, 0))
    kv_specs = [pl.BlockSpec((None, None, N_HEADS, HEAD_DIM, page), page_map(r)) for r in range(npp)]
    lf_specs = [pl.BlockSpec((None, None, N_HEADS, page), lf_map(r)) for r in range(npp)]
    stat = pltpu.VMEM((N_HEADS, LANES), F32)
    grid_spec = pltpu.PrefetchScalarGridSpec(
        num_scalar_prefetch=1,
        grid=(nb, n_pages // npp),
        in_specs=[per_b_lanes, per_b_lanes, per_b, per_b,
                  pl.BlockSpec((1, N_HEADS, LANES), lambda b, g, pt: (b, 0, 0)),
                  pl.BlockSpec((page, page), lambda b, g, pt: (0, 0))] + kv_specs + kv_specs + lf_specs,
        out_specs=per_b,
        scratch_shapes=[pltpu.VMEM((N_HEADS, HEAD_DIM, LANES), F32), stat, stat,
                        pltpu.VMEM((N_HEADS, HEAD_DIM, page), F32), stat],
    )
    out = pl.pallas_call(
        functools.partial(_fox_sample_kernel, page=page, npp=npp),
        grid_spec=grid_spec,
        out_shape=jax.ShapeDtypeStruct((nb, N_HEADS, HEAD_DIM), F32),
        compiler_params=_params(("arbitrary", "arbitrary")),
        name="fox_sample",
    )(page_table, lanes(q3), lanes(vn3), q3, kn3, lanes(logf_new), tsuf,
      *([kt] * npp), *([vt] * npp), *([lft] * npp))
    return out.reshape(nb, WIDTH)


def _gated_head_norm(h, o_pre, g):
    hg = h * jax.nn.sigmoid(o_pre)
    return hg * lax.rsqrt(jnp.mean(hg * hg, axis=-1, keepdims=True) + RMS_EPS) * g


def _mlstm_prompt_kernel(q_ref, k_ref, vt_ref, ot_ref, gcol_ref, grow_ref, gml_ref,
                         yt_ref, c_ref, n_ref, m_ref, *, chunk):
    @pl.when(pl.program_id(0) == 0)
    def _():
        c_ref[...] = jnp.zeros(c_ref.shape, F32)
        n_ref[...] = jnp.zeros(n_ref.shape, F32)
        m_ref[...] = jnp.zeros(m_ref.shape, F32)

    r = lax.broadcasted_iota(jnp.int32, (chunk, chunk), 0)
    c = lax.broadcasted_iota(jnp.int32, (chunk, chunk), 1)
    causal_t = r <= c
    gcol = gcol_ref[...]
    grow = grow_ref[...]
    a_col_all = _mm3_left((c <= r).astype(BF16), gcol)
    a_row_all = _mm3_right(grow[_G_LF:_G_LF + 8], causal_t.astype(BF16))
    lane = lax.broadcasted_iota(jnp.int32, (1, LANES), 1)

    heads = range(N_HEADS)
    own = [(lane >= (h % 2) * HEAD_DIM) & (lane < (h % 2 + 1) * HEAD_DIM) for h in heads]
    pair = [slice((h // 2) * LANES, (h // 2 + 1) * LANES) for h in heads]
    rows = [slice(h * HEAD_DIM, (h + 1) * HEAD_DIM) for h in heads]
    kb = [k_ref[:, pair[h]].astype(BF16) for h in range(0, N_HEADS, 2)]
    qb = [jnp.where(own[h], q_ref[:, pair[h]], 0.0).astype(BF16) for h in heads]
    c_prev = [c_ref[h] for h in heads]
    n_prev = [n_ref[h] for h in heads]
    m_prev = [m_ref[h][:, 0:1] for h in heads]
    st = [_mm_nt(kb[h // 2], qb[h]) for h in heads]
    ctq = [_mm_nt(c_prev[h].astype(BF16), qb[h]) for h in heads]
    qn = [_mm_nt(n_prev[h].astype(BF16), qb[h])[0:1] for h in heads]

    m_t, w_intra, w_inter, decay, m_new, vw, w_rows = [], [], [], [], [], [], []
    for h in heads:
        a_r = a_row_all[h:h + 1, :]
        ig_r = grow[_G_IG + h:_G_IG + h + 1, :]
        key_term = gcol[:, _G_IG + h:_G_IG + h + 1] - a_col_all[:, _G_LF + h:_G_LF + h + 1]
        d = jnp.where(causal_t, a_r + key_term, NEG_INF)
        b = a_r + m_prev[h]
        m_t.append(jnp.maximum(b, jnp.max(d, axis=0, keepdims=True)))
        w_intra.append(jnp.exp(d - m_t[h]))
        w_inter.append(jnp.exp(b - m_t[h]))
        m_new.append(m_t[h][:, chunk - 1:chunk])
        a_last = a_r[:, chunk - 1:chunk]
        decay.append(jnp.exp(a_last + m_prev[h] - m_new[h]))
        w_write = jnp.exp(a_last - a_r + ig_r - m_new[h])
        w_rows.append(jnp.broadcast_to(w_write, (SUBLANES, chunk)).astype(BF16))
        vw.append((vt_ref[rows[h], :].astype(F32) * w_write).astype(BF16))
    scores = [st[h] * w_intra[h] for h in heads]
    sv = [_mm(vt_ref[rows[h], :], scores[h].astype(BF16)) for h in heads]
    c_add = [_mm(vw[h], kb[h // 2]) for h in heads]
    n_add = [_mm(w_rows[h], kb[h // 2]) for h in heads]
    for h in heads:
        num = w_inter[h] * ctq[h] + sv[h]
        den = w_inter[h] * qn[h] + jnp.sum(scores[h], axis=0, keepdims=True)
        hh = num / jnp.maximum(jnp.abs(den), jnp.exp(-m_t[h]))
        c_ref[h] = decay[h] * c_prev[h] + jnp.where(own[h], c_add[h], 0.0)
        n_ref[h] = decay[h] * n_prev[h] + jnp.where(own[h], n_add[h], 0.0)
        m_ref[h] = jnp.broadcast_to(m_new[h], (1, LANES))
        hg = hh * jax.nn.sigmoid(ot_ref[rows[h], :])
        yt_ref[rows[h], :] = hg * lax.rsqrt(jnp.mean(hg * hg, axis=0, keepdims=True) + RMS_EPS) \
            * gml_ref[rows[h], :]


def _mlstm_prompt(mq, mk, mvt, mot, gcol, grow, g_ml):
    s = mq.shape[0]
    chunk = int(np.gcd(s, MLSTM_CHUNK))
    tok_rows = pl.BlockSpec((chunk, WIDTH), lambda i: (i, 0))
    tok_lanes = pl.BlockSpec((WIDTH, chunk), lambda i: (0, i))
    state = lambda shape: pl.BlockSpec(shape, lambda i: (0,) * len(shape))
    yt, ct, n, m = pl.pallas_call(
        functools.partial(_mlstm_prompt_kernel, chunk=chunk),
        grid=(s // chunk,),
        in_specs=[tok_rows, tok_rows, tok_lanes, tok_lanes, pl.BlockSpec((chunk, LANES), lambda i: (i, 0)),
                  pl.BlockSpec((32, chunk), lambda i: (0, i)), state((WIDTH, 1))],
        out_specs=(tok_lanes, state((N_HEADS, HEAD_DIM, LANES)), state((N_HEADS, SUBLANES, LANES)),
                   state((N_HEADS, 1, LANES))),
        out_shape=(jax.ShapeDtypeStruct((WIDTH, s), F32),
                   jax.ShapeDtypeStruct((N_HEADS, HEAD_DIM, LANES), F32),
                   jax.ShapeDtypeStruct((N_HEADS, SUBLANES, LANES), F32),
                   jax.ShapeDtypeStruct((N_HEADS, 1, LANES), F32)),
        compiler_params=_params(("arbitrary",)),
        name="mlstm_prompt",
    )(mq, mk, mvt, mot, gcol, grow, g_ml.reshape(WIDTH, 1))
    half = lambda a: jnp.stack([a[h, ..., (h % 2) * HEAD_DIM:(h % 2 + 1) * HEAD_DIM] for h in range(N_HEADS)])
    return yt, jnp.swapaxes(half(ct), 1, 2), half(n)[:, 0, :], m[:, 0, 0]


_MLSTM_SAMPLE_PER_STEP = 1


def _mlstm_sample_kernel(q_ref, k_ref, v_ref, o_ref, ig_ref, lf_ref, m_ref, c_ref, n_ref, gml_ref,
                         y_ref, cn_ref, nn_ref, mn_ref):
    r = lax.broadcasted_iota(jnp.int32, (HEAD_DIM, HEAD_DIM), 0)
    c = lax.broadcasted_iota(jnp.int32, (HEAD_DIM, HEAD_DIM), 1)
    eye = r == c

    def column(row):
        return jnp.sum(jnp.where(eye, jnp.broadcast_to(row, (HEAD_DIM, HEAD_DIM)), 0.0), axis=1, keepdims=True)

    for i in range(q_ref.shape[0]):
        for h in range(N_HEADS):
            q = q_ref[i, h:h + 1, :]
            k = k_ref[i, h:h + 1, :]
            v = v_ref[i, h:h + 1, :]
            ig = ig_ref[i, h:h + 1, 0:1]
            lf = lf_ref[i, h:h + 1, 0:1]
            m_prev = m_ref[i, h:h + 1, 0:1]
            c_prev = c_ref[i, h]
            n_prev = n_ref[i, h:h + 1, :]
            b = lf + m_prev
            m_t = jnp.maximum(b, ig)
            w_intra = jnp.exp(ig - m_t)
            w_inter = jnp.exp(b - m_t)
            scores = jnp.sum(q * k, axis=1, keepdims=True) * w_intra
            qc = jnp.sum(column(q) * c_prev, axis=0, keepdims=True)
            num = w_inter * qc + scores * v
            den = w_inter * jnp.sum(q * n_prev, axis=1, keepdims=True) + scores
            hh = num / jnp.maximum(jnp.abs(den), jnp.exp(-m_t))
            cn_ref[i, h] = w_inter * c_prev + w_intra * (column(k) * v)
            nn_ref[i, h:h + 1, :] = w_inter * n_prev + w_intra * k
            mn_ref[i, h:h + 1, :] = jnp.broadcast_to(m_t, (1, LANES))
            y_ref[i, h:h + 1, :] = _gated_head_norm(hh, o_ref[i, h:h + 1, :], gml_ref[h:h + 1, :])


def _mlstm_sample(mq, mk, mv, mo, ig, lf, state_c, state_n, state_m, g_ml):
    nb = mq.shape[0]
    heads = lambda a: a.reshape(nb, N_HEADS, HEAD_DIM)
    lanes = lambda a: jnp.broadcast_to(a[:, :, None], (nb, N_HEADS, LANES))
    per = int(np.gcd(nb, _MLSTM_SAMPLE_PER_STEP))
    vec = pl.BlockSpec((per, N_HEADS, HEAD_DIM), lambda b: (b, 0, 0))
    sca = pl.BlockSpec((per, N_HEADS, LANES), lambda b: (b, 0, 0))
    mat = pl.BlockSpec((per, N_HEADS, HEAD_DIM, HEAD_DIM), lambda b: (b, 0, 0, 0))
    y, cn, nn, mn = pl.pallas_call(
        _mlstm_sample_kernel,
        grid=(nb // per,),
        in_specs=[vec, vec, vec, vec, sca, sca, sca, mat, vec,
                  pl.BlockSpec((N_HEADS, HEAD_DIM), lambda b: (0, 0))],
        out_specs=(vec, mat, vec, sca),
        out_shape=(jax.ShapeDtypeStruct((nb, N_HEADS, HEAD_DIM), F32),
                   jax.ShapeDtypeStruct((nb, N_HEADS, HEAD_DIM, HEAD_DIM), F32),
                   jax.ShapeDtypeStruct((nb, N_HEADS, HEAD_DIM), F32),
                   jax.ShapeDtypeStruct((nb, N_HEADS, LANES), F32)),
        compiler_params=_params(("arbitrary",)),
        name="mlstm_sample",
    )(heads(mq), heads(mk), heads(mv), heads(mo), lanes(ig), lanes(lf), lanes(state_m), state_c, state_n,
      g_ml.reshape(N_HEADS, HEAD_DIM))
    return y.reshape(nb, WIDTH), cn, nn, mn[:, :, 0]


_R_EXPERT, _R_GROUP = 0, N_EXPERTS


def _outproj_router_kernel(yf_ref, yml_ref, x_ref, wf_ref, wm_ref, gfox_ref, gffn_ref, wr_ref, br_ref,
                           x1_ref, xn_ref, comb_ref, *, prompt):
    yf = yf_ref[...]
    if prompt:
        ms = jnp.mean(yf * yf, axis=0, keepdims=True)
        yfn = (yf * lax.rsqrt(ms + RMS_EPS) * gfox_ref[...]).astype(BF16)
        y = _mm_tn(yfn, wf_ref[...]) + _mm_tn(yml_ref[...].astype(BF16), wm_ref[...])
    else:
        y = _mm_f32(_rms(yf, gfox_ref[...]), wf_ref[...]) + _mm_f32(yml_ref[...], wm_ref[...])
    x1 = x_ref[...] + y
    x1_ref[...] = x1
    xn = _rms(x1, gffn_ref[...])
    xb = xn.astype(BF16)
    xn_ref[...] = xb

    router = _mm(xb, wr_ref[...]) if prompt else _mm_f32(xn, wr_ref[...])
    logits = router + br_ref[...]
    lane = lax.broadcasted_iota(jnp.int32, logits.shape, 1)
    big = jnp.int32(2 * LANES)

    def first_argmax(vals):
        top = jnp.max(vals, axis=1, keepdims=True)
        idx = jnp.min(jnp.where(vals == top, lane, big), axis=1, keepdims=True)
        return top, idx

    is_group = (lane >= _R_GROUP) & (lane < _R_GROUP + N_GROUPS)
    lg = jnp.where(is_group, logits, NEG_INF)
    lg_top, lg_idx = first_argmax(lg)
    gate_g = 1.0 / jnp.sum(jnp.exp(lg - lg_top), axis=1, keepdims=True)
    grp = lg_idx - _R_GROUP
    in_grp = (lane >= grp * EXPERTS_PER_GROUP) & (lane < (grp + 1) * EXPERTS_PER_GROUP)
    le = jnp.where(in_grp, logits, NEG_INF)
    top1, idx1 = first_argmax(le)
    top2, idx2 = first_argmax(jnp.where(lane == idx1, NEG_INF, le))
    e2 = jnp.exp(top2 - top1)
    w1 = gate_g / (1.0 + e2)
    w2 = gate_g * e2 / (1.0 + e2)
    comb_ref[...] = jnp.where(lane == idx1, w1, 0.0) + jnp.where(lane == idx2, w2, 0.0)


def _outproj_router(yf, yml, x, w_out, g_fox, g_ffn, w_rg, b_rg, w_re, b_re, tm, prompt):
    n, d = x.shape
    wdt = BF16 if prompt else F32
    wf = w_out[0:WIDTH].astype(wdt)
    wm = w_out[WIDTH:2 * WIDTH].astype(wdt)
    pad = LANES - N_EXPERTS - N_GROUPS
    wr = jnp.concatenate([w_re, w_rg, jnp.zeros((d, pad), F32)], axis=1).astype(wdt)
    br = jnp.concatenate([b_re, b_rg, jnp.zeros((pad,), F32)]).reshape(1, LANES)
    full = lambda shape: pl.BlockSpec(shape, lambda i: (0,) * len(shape))
    rows = lambda width: pl.BlockSpec((tm, width), lambda i: (i, 0))
    if prompt:
        yf_spec = pl.BlockSpec((WIDTH, tm), lambda i: (0, i))
        gfox = g_fox.reshape(WIDTH, 1)
    else:
        yf_spec = rows(WIDTH)
        gfox = g_fox.reshape(1, WIDTH)
    return pl.pallas_call(
        functools.partial(_outproj_router_kernel, prompt=prompt),
        grid=(n // tm,),
        in_specs=[yf_spec, yf_spec, rows(d), full((WIDTH, d)), full((WIDTH, d)), full(gfox.shape),
                  full((1, d)), full((d, LANES)), full((1, LANES))],
        out_specs=(rows(d), rows(d), rows(LANES)),
        out_shape=(jax.ShapeDtypeStruct((n, d), F32), jax.ShapeDtypeStruct((n, d), BF16),
                   jax.ShapeDtypeStruct((n, LANES), F32)),
        compiler_params=_params(("arbitrary",)),
        name="outproj_router_prompt" if prompt else "outproj_router_sample",
    )(yf, yml, x, wf, wm, gfox, g_ffn.reshape(1, d), wr, br)


_MOE_EXPERTS_PER_STEP = 4


def _moe_kernel(xn_ref, comb_ref, x1_ref, wg_ref, wu_ref, wd_ref, gfin_ref, y_ref, acc_s):
    g = pl.program_id(1)
    per_step = wg_ref.shape[0]

    @pl.when(g == 0)
    def _():
        acc_s[...] = jnp.zeros(acc_s.shape, F32)

    xb = xn_ref[...]
    comb = comb_ref[...]
    lane = lax.broadcasted_iota(jnp.int32, comb.shape, 1)
    gates = [_mm(xb, wg_ref[j]) for j in range(per_step)]
    ups = [_mm(xb, wu_ref[j]) for j in range(per_step)]
    for j in range(per_step):
        he = gates[j] * jax.nn.sigmoid(gates[j]) * ups[j]
        out = _mm(he.astype(BF16), wd_ref[j])
        w_e = jnp.sum(jnp.where(lane == g * per_step + j, comb, 0.0), axis=1, keepdims=True)
        acc_s[...] += w_e * out

    @pl.when(g == pl.num_programs(1) - 1)
    def _():
        y_ref[...] = _rms(x1_ref[...] + acc_s[...], gfin_ref[...])


def _moe(xn, comb, x1, wg, wu, wd, g_final, tm):
    n, d = x1.shape
    de = wg.shape[2]
    eps = _MOE_EXPERTS_PER_STEP
    rows = lambda width: pl.BlockSpec((tm, width), lambda i, e: (i, 0))
    return pl.pallas_call(
        _moe_kernel,
        grid=(n // tm, N_EXPERTS // eps),
        in_specs=[rows(d), rows(LANES), rows(d),
                  pl.BlockSpec((eps, d, de), lambda i, e: (e, 0, 0)),
                  pl.BlockSpec((eps, d, de), lambda i, e: (e, 0, 0)),
                  pl.BlockSpec((eps, de, d), lambda i, e: (e, 0, 0)),
                  pl.BlockSpec((1, d), lambda i, e: (0, 0))],
        out_specs=rows(d),
        out_shape=jax.ShapeDtypeStruct((n, d), F32),
        scratch_shapes=[pltpu.VMEM((tm, d), F32)],
        compiler_params=_params(("arbitrary", "arbitrary")),
        name="moe",
    )(xn, comb, x1, wg, wu, wd, g_final.reshape(1, d))


def kernel(x_prompt, x_sample, cache_k, cache_v, cache_logf, state_conv, state_C, state_n, state_m,
           page_table, g_mix, w_in, b_fox_f, b_ml_i, b_ml_f, w_conv, b_conv, g_fox_out, g_ml_out,
           w_out, g_ffn, w_router_group, b_router_group, w_router_expert, b_router_expert,
           w_exp_gate, w_exp_up, w_exp_down, g_final):
    depth = w_in.shape[0]
    batch, seq, d = x_prompt.shape
    nb, dec_seq, _ = x_sample.shape
    assert depth == 1 and batch == 1 and dec_seq == 1
    l = 0
    gate_bias = _gate_bias(b_fox_f[l], b_ml_i[l], b_ml_f[l])
    wg, wu, wd = (w[l].astype(BF16) for w in (w_exp_gate, w_exp_up, w_exp_down))
    router = (w_router_group[l], b_router_group[l], w_router_expert[l], b_router_expert[l])

    t_attn = min(512, seq)
    (qa, ka, k_p, v_p, vt, gcol, grow, mq, mk, mvt, mot, tail) = _inproj_prompt(
        x_prompt[0], g_mix[l], w_in[l], gate_bias, w_conv[l], b_conv[l], tm=t_attn)
    y_fox_t = _fox_prompt(qa, ka, vt, tq=min(4096, seq))
    y_ml_t, c_p, n_p, m_p = _mlstm_prompt(mq, mk, mvt, mot, gcol, grow, g_ml_out[l])
    x1, xn, comb = _outproj_router(y_fox_t, y_ml_t, x_prompt[0], w_out[l], g_fox_out[l], g_ffn[l], *router,
                                   tm=min(512, seq), prompt=True)
    y_prompt = _moe(xn, comb, x1, wg, wu, wd, g_final, tm=min(1024, seq))

    xs = x_sample[:, 0, :]
    (q_s, k_s, v_s, g_s, mq_s, mk_s, mv_s, mo_s, conv_new) = _inproj_sample(
        xs, g_mix[l], w_in[l], gate_bias, w_conv[l], b_conv[l], jnp.transpose(state_conv[l], (1, 0, 2)))
    logf_s = g_s[:, _G_LOGF:_G_LOGF + N_HEADS]
    y_fox_s = _fox_sample(q_s, k_s, v_s, logf_s, cache_k[l:l + 1], cache_v[l:l + 1], cache_logf[l:l + 1],
                          page_table)
    y_ml_s, c_s, n_s, m_s = _mlstm_sample(mq_s, mk_s, mv_s, mo_s, g_s[:, _G_IG:_G_IG + N_HEADS],
                                          g_s[:, _G_LF:_G_LF + N_HEADS], state_C[l], state_n[l], state_m[l],
                                          g_ml_out[l])
    x1_s, xn_s, comb_s = _outproj_router(y_fox_s, y_ml_s, xs, w_out[l], g_fox_out[l], g_ffn[l], *router,
                                         tm=nb, prompt=False)
    y_sample = _moe(xn_s, comb_s, x1_s, wg, wu, wd, g_final, tm=nb)

    heads = lambda a, n: a.reshape(1, n, -1, N_HEADS, HEAD_DIM)
    return (
        y_prompt[None], y_sample[:, None, :],
        heads(k_p, 1), heads(v_p, 1), gcol[:, _G_LOGF:_G_LOGF + N_HEADS].reshape(1, 1, seq, N_HEADS),
        tail[SUBLANES - (CONV_WIDTH - 1):][None, None],
        c_p[None, None], n_p[None, None], m_p[None, None],
        heads(k_s, nb), heads(v_s, nb), logf_s.reshape(1, nb, 1, N_HEADS),
        jnp.transpose(conv_new, (1, 0, 2))[None],
        c_s[None], n_s[None], m_s[None],
    )
```
